```python
import math
import jax, jax.numpy as jnp
from jax import lax
import numpy as np

D_MODEL = 1024
BATCH = 32
SEQ = 2048
DEPTH = 2

N_BRANCHES = 4
BRANCH_WIDTH = D_MODEL // N_BRANCHES
MOBA_HEADS = 4
MOBA_HEAD_DIM = BRANCH_WIDTH // MOBA_HEADS
MOBA_BLOCK = 256
MOBA_TOPK = 3
Q_BLOCK = 128
CONV_WIDTH = 3
POOL_WINDOWS = (2, 4, 8, 16)
POOL_GROUP = BRANCH_WIDTH // len(POOL_WINDOWS)
DIFF_HEADS = 4
DIFF_V_DIM = BRANCH_WIDTH // DIFF_HEADS
DIFF_QK_DIM = DIFF_V_DIM // 2
REL_BUCKETS = 32
REL_MAX_DIST = 128
N_ATTN_HEADS = MOBA_HEADS + DIFF_HEADS
N_MIX_SLICES = 10
W_IN_WIDTH = N_MIX_SLICES * BRANCH_WIDTH + N_BRANCHES * D_MODEL
D_FF_DENSE = 2816
N_EXPERTS = 8
TOP_K_EXPERTS = 2
D_FF_EXPERT = 3584
N_DENSE = (DEPTH + 1) // 2
N_MOE = DEPTH // 2
RMS_EPS = 1e-6
SUBLN_EPS = 1e-5
NEG_INF = -1e30

kernel_name = "hybrid_moba_conv_pool_diffattn_moe"


def rmsnorm(x, g, eps=RMS_EPS):
    xf = x.astype(jnp.float32)
    r = lax.rsqrt(jnp.mean(xf * xf, axis=-1, keepdims=True) + eps)
    return (xf * r).astype(x.dtype) * g


def rel_bucket(dist):
    n = jnp.maximum(dist, 0)
    max_exact = REL_BUCKETS // 2
    nf = jnp.maximum(n, max_exact).astype(jnp.float32)
    large = max_exact + (jnp.log(nf / max_exact) / math.log(REL_MAX_DIST / max_exact)
                         * (REL_BUCKETS - max_exact)).astype(jnp.int32)
    large = jnp.minimum(large, REL_BUCKETS - 1)
    return jnp.where(n < max_exact, n, large)


def moba_attention(q, k, v, bias_tab):
    _, H, S, dh = q.shape
    n_blk = -(-S // MOBA_BLOCK)
    s_pad = n_blk * MOBA_BLOCK
    topk = min(MOBA_TOPK, n_blk)
    n_qc = s_pad // Q_BLOCK
    scale = dh ** -0.5
    pad = ((0, 0), (0, s_pad - S), (0, 0))
    head_ix = jnp.arange(H)[:, None, None, None]

    def per_seq(args):
        qb, kb, vb = args
        qb = jnp.pad(qb, pad) * scale
        k_blocks = jnp.pad(kb, pad).reshape(H, n_blk, MOBA_BLOCK, dh)
        v_blocks = jnp.pad(vb, pad).reshape(H, n_blk, MOBA_BLOCK, dh)
        k_mean = jnp.mean(k_blocks.astype(jnp.float32), axis=2).astype(kb.dtype)
        q_chunks = qb.reshape(H, n_qc, Q_BLOCK, dh).transpose(1, 0, 2, 3)

        def per_chunk(args2):
            qc, ci = args2
            t_pos = ci * Q_BLOCK + jnp.arange(Q_BLOCK)
            own = (ci * Q_BLOCK) // MOBA_BLOCK
            gate = jnp.einsum('hqd,hnd->hqn', qc, k_mean)
            gate = jnp.where(jnp.arange(n_blk) < own, gate, NEG_INF)
            _, sel = lax.top_k(gate, topk)
            valid = jnp.arange(topk) < own
            k_sel = jax.vmap(lambda kbh, ix: kbh[ix])(k_blocks, sel)
            v_sel = jax.vmap(lambda vbh, ix: vbh[ix])(v_blocks, sel)
            key_pos_sel = sel[..., None] * MOBA_BLOCK + jnp.arange(MOBA_BLOCK)
            bias_sel = bias_tab[head_ix, rel_bucket(t_pos[None, :, None, None] - key_pos_sel)]
            logit_sel = jnp.einsum('hqd,hqnkd->hqnk', qc, k_sel) + bias_sel
            logit_sel = jnp.where(valid[:, None], logit_sel, NEG_INF)
            k_own = lax.dynamic_index_in_dim(k_blocks, own, axis=1, keepdims=False)
            v_own = lax.dynamic_index_in_dim(v_blocks, own, axis=1, keepdims=False)
            rel_own = t_pos[:, None] - (own * MOBA_BLOCK + jnp.arange(MOBA_BLOCK))[None, :]
            logit_own = jnp.einsum('hqd,hkd->hqk', qc, k_own) + bias_tab[:, rel_bucket(rel_own)]
            logit_own = jnp.where(rel_own >= 0, logit_own, NEG_INF)
            logits = jnp.concatenate(
                [logit_sel.reshape(H, Q_BLOCK, topk * MOBA_BLOCK), logit_own], axis=-1)
            p = jax.nn.softmax(logits.astype(jnp.float32), axis=-1).astype(vb.dtype)
            p_sel = p[..., :topk * MOBA_BLOCK].reshape(H, Q_BLOCK, topk, MOBA_BLOCK)
            p_own = p[..., topk * MOBA_BLOCK:]
            return (jnp.einsum('hqnk,hqnkd->hqd', p_sel, v_sel)
                    + jnp.einsum('hqk,hkd->hqd', p_own, v_own))

        out = lax.map(per_chunk, (q_chunks, jnp.arange(n_qc)))
        return out.transpose(1, 0, 2, 3).reshape(H, s_pad, dh)[:, :S]

    return lax.map(per_seq, (q, k, v))


def diff_attention(q, k, v, bias_tab, lam, subln_g, lam_init):
    B_, H, _, S, dq = q.shape
    dv = v.shape[-1]
    n_qc = S // Q_BLOCK
    q = q * dq ** -0.5
    q_chunks = q.reshape(B_, H, 2, n_qc, Q_BLOCK, dq).transpose(3, 0, 1, 2, 4, 5)
    key_pos = jnp.arange(S)

    def per_chunk(args):
        qc, ci = args
        t_pos = ci * Q_BLOCK + jnp.arange(Q_BLOCK)
        rel = t_pos[:, None] - key_pos[None, :]
        bias = bias_tab[:, rel_bucket(rel)]
        logits = jnp.einsum('bhmqd,bhmsd->bhmqs', qc, k) + bias[None, :, None]
        logits = jnp.where(rel >= 0, logits, NEG_INF).astype(jnp.float32)
        p = jax.nn.softmax(logits, axis=-1)
        a = (p[:, :, 0] - lam * p[:, :, 1]).astype(v.dtype)
        return jnp.einsum('bhqs,bhsd->bhqd', a, v)

    o = lax.map(per_chunk, (q_chunks, jnp.arange(n_qc)))
    o = o.transpose(1, 2, 0, 3, 4).reshape(B_, H, S, dv)
    return rmsnorm(o, subln_g, SUBLN_EPS) * (1.0 - lam_init)


def short_conv(xt, b_gate, c_gate, w_conv):
    S = xt.shape[1]
    u = c_gate * xt
    up = jnp.pad(u, ((0, 0), (CONV_WIDTH - 1, 0), (0, 0)))
    conv = w_conv[0] * up[:, 0:S]
    for i in range(1, CONV_WIDTH):
        conv = conv + w_conv[i] * up[:, i:i + S]
    return b_gate * conv


def multiscale_pool(u, w_grp, scale):
    B_, S, _ = u.shape
    ug = u.reshape(B_, S, len(POOL_WINDOWS), POOL_GROUP)
    cs = jnp.cumsum(ug.astype(jnp.float32), axis=1)
    outs = []
    for g, w in enumerate(POOL_WINDOWS):
        c = cs[:, :, g]
        lag = jnp.pad(c, ((0, 0), (w, 0), (0, 0)))[:, :S]
        cnt = jnp.minimum(jnp.arange(1, S + 1), w).astype(jnp.float32)[None, :, None]
        outs.append((c - lag) / cnt)
    pooled = jnp.stack(outs, axis=2).astype(u.dtype) - ug
    y = jnp.einsum('bsgc,gcd->bsgd', pooled, w_grp).reshape(B_, S, BRANCH_WIDTH)
    return y * scale


def token_mixers(h, w_in, conv_w, pool_w, pool_scale, diff_lam, subln_g, branch_proj, w_out,
                 bias_table, layer_idx):
    B_, S, _ = h.shape
    u = h @ w_in
    split_pts = [BRANCH_WIDTH * i for i in range(1, N_MIX_SLICES + 1)]
    qa, ka, va, xb, bb, cb, pc, qd, kd, vd, gates = jnp.split(u, split_pts, axis=-1)

    def heads_a(t):
        return t.reshape(B_, S, MOBA_HEADS, MOBA_HEAD_DIM).transpose(0, 2, 1, 3)

    y_a = moba_attention(heads_a(qa), heads_a(ka), heads_a(va), bias_table[:, :MOBA_HEADS].T)
    y_a = y_a.transpose(0, 2, 1, 3).reshape(B_, S, BRANCH_WIDTH)

    y_b = short_conv(xb, bb, cb, conv_w)
    y_c = multiscale_pool(pc, pool_w, pool_scale)

    lam_init = 0.8 - 0.6 * math.exp(-0.3 * layer_idx)
    lam = (jnp.exp(jnp.sum(diff_lam[0] * diff_lam[1]).astype(jnp.float32))
           - jnp.exp(jnp.sum(diff_lam[2] * diff_lam[3]).astype(jnp.float32)) + lam_init)
    q_d = qd.reshape(B_, S, DIFF_HEADS, 2, DIFF_QK_DIM).transpose(0, 2, 3, 1, 4)
    k_d = kd.reshape(B_, S, DIFF_HEADS, 2, DIFF_QK_DIM).transpose(0, 2, 3, 1, 4)
    v_d = vd.reshape(B_, S, DIFF_HEADS, DIFF_V_DIM).transpose(0, 2, 1, 3)
    y_d = diff_attention(q_d, k_d, v_d, bias_table[:, MOBA_HEADS:].T, lam, subln_g, lam_init)
    y_d = y_d.transpose(0, 2, 1, 3).reshape(B_, S, BRANCH_WIDTH)

    g = jax.nn.sigmoid(gates.reshape(B_, S, N_BRANCHES, D_MODEL))
    merged = g[:, :, 0] * (y_a @ branch_proj[0])
    merged = merged + g[:, :, 1] * (y_b @ branch_proj[1])
    merged = merged + g[:, :, 2] * (y_c @ branch_proj[2])
    merged = merged + g[:, :, 3] * (y_d @ branch_proj[3])
    return merged @ w_out


def swiglu(h, w1, w3, w2):
    return (jax.nn.silu(h @ w1) * (h @ w3)) @ w2


def moe_swiglu(h, router, w1, w3, w2):
    logits = (h @ router).astype(jnp.float32)
    top_v, top_i = lax.top_k(logits, TOP_K_EXPERTS)
    top_w = jax.nn.softmax(top_v, axis=-1)
    out = jnp.zeros_like(h)
    for e in range(N_EXPERTS):
        gate_e = jnp.sum(jnp.where(top_i == e, top_w, 0.0), axis=-1, keepdims=True).astype(h.dtype)
        out = out + gate_e * swiglu(h, w1[e], w3[e], w2[e])
    return out


def setup_inputs(seed: int = 0) -> dict:
    key = jax.random.key(seed)
    ks = jax.random.split(key, 20)

    def nrm(k, shape, scale):
        return jax.random.normal(k, shape, jnp.float32) * scale

    return {
        "x": nrm(ks[0], (BATCH, SEQ, D_MODEL), 1.0),
        "bias_table": nrm(ks[1], (REL_BUCKETS, N_ATTN_HEADS), 0.5),
        "mix_norm_g": 1.0 + nrm(ks[2], (DEPTH, D_MODEL), 0.05),
        "w_in": nrm(ks[3], (DEPTH, D_MODEL, W_IN_WIDTH), D_MODEL ** -0.5),
        "conv_w": nrm(ks[4], (DEPTH, CONV_WIDTH, BRANCH_WIDTH), CONV_WIDTH ** -0.5),
        "pool_w": nrm(ks[5], (DEPTH, len(POOL_WINDOWS), POOL_GROUP, POOL_GROUP), POOL_GROUP ** -0.5),
        "pool_scale": 1.0 + nrm(ks[6], (DEPTH, BRANCH_WIDTH), 0.1),
        "diff_lambda": nrm(ks[7], (DEPTH, 4, DIFF_QK_DIM), 0.1),
        "diff_subln_g": 1.0 + nrm(ks[8], (DEPTH, DIFF_V_DIM), 0.05),
        "branch_proj": nrm(ks[9], (DEPTH, N_BRANCHES, BRANCH_WIDTH, D_MODEL), BRANCH_WIDTH ** -0.5),
        "w_out": nrm(ks[10], (DEPTH, D_MODEL, D_MODEL), D_MODEL ** -0.5),
        "ffn_norm_g": 1.0 + nrm(ks[11], (DEPTH, D_MODEL), 0.05),
        "dense_w1": nrm(ks[12], (N_DENSE, D_MODEL, D_FF_DENSE), D_MODEL ** -0.5),
        "dense_w3": nrm(ks[13], (N_DENSE, D_MODEL, D_FF_DENSE), D_MODEL ** -0.5),
        "dense_w2": nrm(ks[14], (N_DENSE, D_FF_DENSE, D_MODEL), D_FF_DENSE ** -0.5),
        "moe_router": nrm(ks[15], (N_MOE, D_MODEL, N_EXPERTS), D_MODEL ** -0.5),
        "moe_w1": nrm(ks[16], (N_MOE, N_EXPERTS, D_MODEL, D_FF_EXPERT), D_MODEL ** -0.5),
        "moe_w3": nrm(ks[17], (N_MOE, N_EXPERTS, D_MODEL, D_FF_EXPERT), D_MODEL ** -0.5),
        "moe_w2": nrm(ks[18], (N_MOE, N_EXPERTS, D_FF_EXPERT, D_MODEL), D_FF_EXPERT ** -0.5),
        "final_norm_g": 1.0 + nrm(ks[19], (D_MODEL,), 0.05),
    }


def reference(x, bias_table, mix_norm_g, w_in, conv_w, pool_w, pool_scale, diff_lambda,
              diff_subln_g, branch_proj, w_out, ffn_norm_g, dense_w1, dense_w3, dense_w2,
              moe_router, moe_w1, moe_w3, moe_w2, final_norm_g):
    for i in range(DEPTH):
        h = rmsnorm(x, mix_norm_g[i])
        x = x + token_mixers(h, w_in[i], conv_w[i], pool_w[i], pool_scale[i], diff_lambda[i],
                             diff_subln_g[i], branch_proj[i], w_out[i], bias_table, i)
        h = rmsnorm(x, ffn_norm_g[i])
        if i % 2 == 0:
            j = i // 2
            x = x + swiglu(h, dense_w1[j], dense_w3[j], dense_w2[j])
        else:
            j = i // 2
            x = x + moe_swiglu(h, moe_router[j], moe_w1[j], moe_w3[j], moe_w2[j])
    return rmsnorm(x, final_norm_g)
```

```python
import functools
import math

import numpy as np
import jax
import jax.numpy as jnp
from jax import lax
from jax.experimental import pallas as pl
from jax.experimental.pallas import tpu as pltpu

F32 = jnp.float32
BF16 = jnp.bfloat16

BRANCH_WIDTH = 256
N_MIX_SLICES = 10
HEAD_WIDTH = 64
N_HEADS = 4
DIFF_QK_DIM = 32
ATT_BLOCK = 256
MOBA_TOPK = 3
CONV_WIDTH = 3
POOL_WINDOWS = (2, 4, 8, 16)
POOL_GROUP = 64
HALO = 16
REL_BUCKETS = 32
REL_MAX_DIST = 128
TOP_K_EXPERTS = 2
RMS_EPS = 1e-6
SUBLN_EPS = 1e-5
NEG_INF = -1e30
V7X_VMEM_BYTES = 64 * 1024 * 1024
VMEM_LIMIT = V7X_VMEM_BYTES - 8 * 1024 * 1024

_TRANS_B = (((1,), (1,)), ((), ()))


def _params(*sem):
    return pltpu.CompilerParams(dimension_semantics=sem, vmem_limit_bytes=VMEM_LIMIT)


def _rms(x, g, eps):
    r = lax.rsqrt(jnp.mean(x * x, axis=-1, keepdims=True) + eps)
    return x * r * g


def _inproj_kernel(x_ref, g_ref, w_ref, o_ref):
    h = _rms(x_ref[...], g_ref[...], RMS_EPS).astype(BF16)
    o_ref[...] = jnp.dot(h, w_ref[...], preferred_element_type=F32)


def _inproj(x2, g, w, tm):
    n, d = x2.shape
    wn = w.shape[1]
    return pl.pallas_call(
        _inproj_kernel,
        grid=(n // tm,),
        in_specs=[pl.BlockSpec((tm, d), lambda i: (i, 0)),
                  pl.BlockSpec((1, d), lambda i: (0, 0)),
                  pl.BlockSpec((d, wn), lambda i: (0, 0))],
        out_specs=pl.BlockSpec((tm, wn), lambda i: (i, 0)),
        out_shape=jax.ShapeDtypeStruct((n, wn), F32),
        compiler_params=_params("parallel"),
        name="inproj",
    )(x2, g, w)


def _rel_bucket_np(dist):
    n = np.maximum(dist, 0)
    max_exact = REL_BUCKETS // 2
    nf = np.maximum(n, max_exact).astype(np.float32)
    large = max_exact + (np.log(nf / np.float32(max_exact)) / np.float32(math.log(REL_MAX_DIST / max_exact))
                         * np.float32(REL_BUCKETS - max_exact)).astype(np.int32)
    large = np.minimum(large, REL_BUCKETS - 1)
    return np.where(n < max_exact, n, large).astype(np.int32)


def _bucket_tiles():
    i = np.arange(ATT_BLOCK)[:, None]
    j = np.arange(ATT_BLOCK)[None, :]
    return np.stack([_rel_bucket_np(i - j), _rel_bucket_np(ATT_BLOCK + i - j)])


def _bias_tiles_kernel(tab_ref, bkt_ref, o_ref):
    h = pl.program_id(0)
    bkt = bkt_ref[...]
    acc = jnp.zeros(bkt.shape, F32)
    for b in range(REL_BUCKETS):
        acc = jnp.where(bkt == b, tab_ref[b, h], acc)
    o_ref[0] = acc


def _bias_tiles(bias_table):
    n_heads = bias_table.shape[1]
    bkt = jnp.asarray(_bucket_tiles())
    return pl.pallas_call(
        _bias_tiles_kernel,
        grid=(n_heads,),
        in_specs=[pl.BlockSpec(memory_space=pltpu.SMEM),
                  pl.BlockSpec((2, ATT_BLOCK, ATT_BLOCK), lambda h: (0, 0, 0))],
        out_specs=pl.BlockSpec((1, 2, ATT_BLOCK, ATT_BLOCK), lambda h: (h, 0, 0, 0)),
        out_shape=jax.ShapeDtypeStruct((n_heads, 2, ATT_BLOCK, ATT_BLOCK), F32),
        compiler_params=_params("arbitrary"),
        name="bias_tiles",
    )(bias_table, bkt)


def _moba_kernel(tab_ref, q_ref, k_ref, v_ref, tiles_ref, o_ref, kb_ref, vb_ref, kmean_ref, acc_ref):
    t = ATT_BLOCK
    n_blk = k_ref.shape[0] // t
    qb = pl.program_id(1)

    @pl.when(qb == 0)
    def _():
        kb_ref[...] = k_ref[...].astype(BF16)
        vb_ref[...] = v_ref[...].astype(BF16)
        kmean_ref[...] = jnp.zeros(kmean_ref.shape, F32)
        for j in range(n_blk):
            kmean_ref[j:j + 1, :] = jnp.mean(k_ref[j * t:(j + 1) * t, :], axis=0, keepdims=True)

    lane = lax.broadcasted_iota(jnp.int32, (t, BRANCH_WIDTH), 1)
    causal = (lax.broadcasted_iota(jnp.int32, (t, t), 0) >= lax.broadcasted_iota(jnp.int32, (t, t), 1))
    glane = lax.broadcasted_iota(jnp.int32, (t, kmean_ref.shape[0]), 1)
    past = glane < qb
    qf = q_ref[...] * (HEAD_WIDTH ** -0.5)
    own0 = pl.multiple_of(qb * t, t)
    k_own = kb_ref[pl.ds(own0, t), :]
    v_own = vb_ref[pl.ds(own0, t), :]
    out = jnp.zeros((t, BRANCH_WIDTH), F32)

    for h in range(N_HEADS):
        in_head = (lane >= h * HEAD_WIDTH) & (lane < (h + 1) * HEAD_WIDTH)
        qh = jnp.where(in_head, qf, 0.0)
        qh16 = qh.astype(BF16)
        gate = lax.dot_general(qh, kmean_ref[...], _TRANS_B, precision=lax.Precision.HIGHEST,
                               preferred_element_type=F32)
        far_bias = tab_ref[REL_BUCKETS - 1, h]

        s = lax.dot_general(qh16, k_own, _TRANS_B, preferred_element_type=F32) + tiles_ref[h, 0]
        s = jnp.where(causal, s, NEG_INF)
        m = jnp.max(s, axis=1, keepdims=True)
        p = jnp.exp(s - m)
        l = jnp.sum(p, axis=1, keepdims=True)
        acc_ref[...] = jnp.dot(p.astype(BF16), v_own, preferred_element_type=F32)

        def body(j, carry, qh16=qh16, gate=gate, far_bias=far_bias, h=h):
            m, l = carry
            gj = jnp.sum(jnp.where(glane == j, gate, 0.0), axis=1, keepdims=True)
            beats = past & ((gate > gj) | ((gate == gj) & (glane < j)))
            n_beat = jnp.sum(beats.astype(F32), axis=1, keepdims=True)
            keep = n_beat < MOBA_TOPK
            j0 = pl.multiple_of(j * t, t)
            s = lax.dot_general(qh16, kb_ref[pl.ds(j0, t), :], _TRANS_B, preferred_element_type=F32)
            bias = jnp.where(j == qb - 1, tiles_ref[h, 1], far_bias)
            s = jnp.where(keep, s + bias, NEG_INF)
            m_new = jnp.maximum(m, jnp.max(s, axis=1, keepdims=True))
            alpha = jnp.exp(m - m_new)
            p = jnp.exp(s - m_new)
            l_new = alpha * l + jnp.sum(p, axis=1, keepdims=True)
            acc_ref[...] = alpha * acc_ref[...] + jnp.dot(p.astype(BF16), vb_ref[pl.ds(j0, t), :],
                                                          preferred_element_type=F32)
            return m_new, l_new

        m, l = lax.fori_loop(0, qb, body, (m, l))
        out = jnp.where(in_head, acc_ref[...] / l, out)

    o_ref[...] = out.astype(o_ref.dtype)


def _moba(u, tiles, bias_table, batch, seq):
    t = ATT_BLOCK
    nq = seq // t
    return pl.pallas_call(
        _moba_kernel,
        grid=(batch, nq),
        in_specs=[pl.BlockSpec(memory_space=pltpu.SMEM),
                  pl.BlockSpec((t, BRANCH_WIDTH), lambda b, q: (b * nq + q, 0)),
                  pl.BlockSpec((seq, BRANCH_WIDTH), lambda b, q: (b, 1)),
                  pl.BlockSpec((seq, BRANCH_WIDTH), lambda b, q: (b, 2)),
                  pl.BlockSpec((N_HEADS, 2, t, t), lambda b, q: (0, 0, 0, 0))],
        out_specs=pl.BlockSpec((t, BRANCH_WIDTH), lambda b, q: (b * nq + q, 0)),
        out_shape=jax.ShapeDtypeStruct((batch * seq, BRANCH_WIDTH), BF16),
        scratch_shapes=[pltpu.VMEM((seq, BRANCH_WIDTH), BF16),
                        pltpu.VMEM((seq, BRANCH_WIDTH), BF16),
                        pltpu.VMEM((128, BRANCH_WIDTH), F32),
                        pltpu.VMEM((t, BRANCH_WIDTH), F32)],
        compiler_params=_params("arbitrary", "arbitrary"),
        name="moba",
    )(bias_table, u, u, u, tiles)


def _diff_kernel(tab_ref, lam_ref, g_ref, q_ref, k_ref, v_ref, tiles_ref, o_ref,
                 kb_ref, vb_ref, acc1_ref, acc2_ref, *, lam_init):
    t = ATT_BLOCK
    qb = pl.program_id(1)

    @pl.when(qb == 0)
    def _():
        kb_ref[...] = k_ref[...].astype(BF16)
        vb_ref[...] = v_ref[...].astype(BF16)

    lp = lam_ref[...]
    lam = (jnp.exp(jnp.sum(lp[0:1] * lp[1:2], axis=1, keepdims=True))
           - jnp.exp(jnp.sum(lp[2:3] * lp[3:4], axis=1, keepdims=True)) + lam_init)

    lane = lax.broadcasted_iota(jnp.int32, (t, BRANCH_WIDTH), 1)
    causal = (lax.broadcasted_iota(jnp.int32, (t, t), 0) >= lax.broadcasted_iota(jnp.int32, (t, t), 1))
    qf = q_ref[...] * (DIFF_QK_DIM ** -0.5)
    own0 = pl.multiple_of(qb * t, t)
    k_own = kb_ref[pl.ds(own0, t), :]
    v_own = vb_ref[pl.ds(own0, t), :]
    out = jnp.zeros((t, BRANCH_WIDTH), F32)
    accs = (acc1_ref, acc2_ref)

    for h in range(N_HEADS):
        in_head = (lane >= h * HEAD_WIDTH) & (lane < (h + 1) * HEAD_WIDTH)
        far_bias = tab_ref[REL_BUCKETS - 1, N_HEADS + h]
        qs = []
        for c in range(2):
            lo = h * HEAD_WIDTH + c * DIFF_QK_DIM
            qs.append(jnp.where((lane >= lo) & (lane < lo + DIFF_QK_DIM), qf, 0.0).astype(BF16))

        ms, ls = [], []
        for c in range(2):
            s = lax.dot_general(qs[c], k_own, _TRANS_B, preferred_element_type=F32) + tiles_ref[h, 0]
            s = jnp.where(causal, s, NEG_INF)
            m = jnp.max(s, axis=1, keepdims=True)
            p = jnp.exp(s - m)
            ms.append(m)
            ls.append(jnp.sum(p, axis=1, keepdims=True))
            accs[c][...] = jnp.dot(p.astype(BF16), v_own, preferred_element_type=F32)

        def body(j, carry, qs=qs, far_bias=far_bias, h=h):
            j0 = pl.multiple_of(j * t, t)
            kj = kb_ref[pl.ds(j0, t), :]
            vj = vb_ref[pl.ds(j0, t), :]
            bias = jnp.where(j == qb - 1, tiles_ref[h, 1], far_bias)
            new = []
            for c in range(2):
                m, l = carry[2 * c], carry[2 * c + 1]
                s = lax.dot_general(qs[c], kj, _TRANS_B, preferred_element_type=F32) + bias
                m_new = jnp.maximum(m, jnp.max(s, axis=1, keepdims=True))
                alpha = jnp.exp(m - m_new)
                p = jnp.exp(s - m_new)
                new += [m_new, alpha * l + jnp.sum(p, axis=1, keepdims=True)]
                accs[c][...] = alpha * accs[c][...] + jnp.dot(p.astype(BF16), vj, preferred_element_type=F32)
            return tuple(new)

        m1, l1, m2, l2 = lax.fori_loop(0, qb, body, (ms[0], ls[0], ms[1], ls[1]))
        o = acc1_ref[...] / l1 - lam * (acc2_ref[...] / l2)
        ms_h = jnp.sum(jnp.where(in_head, o * o, 0.0), axis=1, keepdims=True) * (1.0 / HEAD_WIDTH)
        o = o * lax.rsqrt(ms_h + SUBLN_EPS)
        out = jnp.where(in_head, o, out)

    o_ref[...] = (out * g_ref[...] * (1.0 - lam_init)).astype(o_ref.dtype)


def _diff(u, tiles, bias_table, lam_params, subln_g4, batch, seq, lam_init):
    t = ATT_BLOCK
    nq = seq // t
    return pl.pallas_call(
        functools.partial(_diff_kernel, lam_init=lam_init),
        grid=(batch, nq),
        in_specs=[pl.BlockSpec(memory_space=pltpu.SMEM),
                  pl.BlockSpec((4, DIFF_QK_DIM), lambda b, q: (0, 0)),
                  pl.BlockSpec((1, BRANCH_WIDTH), lambda b, q: (0, 0)),
                  pl.BlockSpec((t, BRANCH_WIDTH), lambda b, q: (b * nq + q, 7)),
                  pl.BlockSpec((seq, BRANCH_WIDTH), lambda b, q: (b, 8)),
                  pl.BlockSpec((seq, BRANCH_WIDTH), lambda b, q: (b, 9)),
                  pl.BlockSpec((N_HEADS, 2, t, t), lambda b, q: (1, 0, 0, 0))],
        out_specs=pl.BlockSpec((t, BRANCH_WIDTH), lambda b, q: (b * nq + q, 0)),
        out_shape=jax.ShapeDtypeStruct((batch * seq, BRANCH_WIDTH), BF16),
        scratch_shapes=[pltpu.VMEM((seq, BRANCH_WIDTH), BF16),
                        pltpu.VMEM((seq, BRANCH_WIDTH), BF16),
                        pltpu.VMEM((t, BRANCH_WIDTH), F32),
                        pltpu.VMEM((t, BRANCH_WIDTH), F32)],
        compiler_params=_params("arbitrary", "arbitrary"),
        name="diff_attn",
    )(bias_table, lam_params, subln_g4, u, u, u, tiles)


def _convpool_kernel(xb_ref, bb_ref, cb_ref, pc_ref, xh_ref, ch_ref, ph_ref, cw_ref, pw_ref, ps_ref,
                     o_ref, u_ref, s_ref, *, seq):
    tm = xb_ref.shape[0]
    pos0 = (pl.program_id(0) * tm) % seq
    has_history = pos0 > 0
    pos = pos0 + lax.broadcasted_iota(jnp.int32, (tm, 1), 0)
    lane = lax.broadcasted_iota(jnp.int32, (tm, BRANCH_WIDTH), 1)

    u_ref[0:HALO, :] = jnp.where(has_history, ch_ref[...] * xh_ref[...], 0.0)
    u_ref[HALO:, :] = cb_ref[...] * xb_ref[...]
    conv = cw_ref[CONV_WIDTH - 1:CONV_WIDTH, :] * u_ref[HALO:, :]
    for i in range(CONV_WIDTH - 1):
        shift = CONV_WIDTH - 1 - i
        conv = conv + cw_ref[i:i + 1, :] * u_ref[HALO - shift:HALO - shift + tm, :]
    o_ref[:, 0:BRANCH_WIDTH] = (bb_ref[...] * conv).astype(o_ref.dtype)

    s_ref[0:HALO, :] = jnp.where(has_history, ph_ref[...], 0.0)
    s_ref[HALO:, :] = pc_ref[...]
    pooled = jnp.zeros((tm, BRANCH_WIDTH), F32)
    done = 0
    for g, w in enumerate(POOL_WINDOWS):
        half = w // 2
        cur = s_ref[done + half:, :] + s_ref[done:HALO + tm - half, :]
        done += half
        s_ref[done:, :] = cur
        cnt = jnp.minimum(pos + 1, w).astype(F32)
        mean_w = s_ref[HALO:, :] / cnt
        pooled = jnp.where((lane >= g * POOL_GROUP) & (lane < (g + 1) * POOL_GROUP), mean_w, pooled)
    pooled = pooled - pc_ref[...]
    y_c = jnp.dot(pooled.astype(BF16), pw_ref[...], preferred_element_type=F32) * ps_ref[...]
    o_ref[:, BRANCH_WIDTH:] = y_c.astype(o_ref.dtype)


def _convpool(u, conv_w, pool_w_bd, pool_scale, seq, tm):
    n = u.shape[0]
    rows = lambda c: pl.BlockSpec((tm, BRANCH_WIDTH), lambda i, c=c: (i, c))
    halo = lambda c: pl.BlockSpec((HALO, BRANCH_WIDTH),
                                  lambda i, c=c: (jnp.maximum(i * (tm // HALO) - 1, 0), c))
    full = lambda a: pl.BlockSpec(a.shape, lambda i: (0,) * a.ndim)
    return pl.pallas_call(
        functools.partial(_convpool_kernel, seq=seq),
        grid=(n // tm,),
        in_specs=[rows(3), rows(4), rows(5), rows(6), halo(3), halo(5), halo(6),
                  full(conv_w), full(pool_w_bd), full(pool_scale)],
        out_specs=pl.BlockSpec((tm, 2 * BRANCH_WIDTH), lambda i: (i, 0)),
        out_shape=jax.ShapeDtypeStruct((n, 2 * BRANCH_WIDTH), BF16),
        scratch_shapes=[pltpu.VMEM((HALO + tm, BRANCH_WIDTH), F32),
                        pltpu.VMEM((HALO + tm, BRANCH_WIDTH), F32)],
        compiler_params=_params("parallel"),
        name="convpool",
    )(u, u, u, u, u, u, u, conv_w, pool_w_bd, pool_scale)


def _merge_kernel(x_ref, g_ref, wg_ref, ya_ref, ybc_ref, yd_ref, bp_ref, wo_ref, o_ref):
    d = x_ref.shape[1]
    x = x_ref[...]
    h = _rms(x, g_ref[...], RMS_EPS).astype(BF16)
    ys = (ya_ref[...], ybc_ref[:, 0:BRANCH_WIDTH], ybc_ref[:, BRANCH_WIDTH:], yd_ref[...])
    merged = jnp.zeros(x.shape, F32)
    for b, y in enumerate(ys):
        gate = jax.nn.sigmoid(jnp.dot(h, wg_ref[:, b * d:(b + 1) * d], preferred_element_type=F32))
        merged = merged + gate * jnp.dot(y, bp_ref[b * BRANCH_WIDTH:(b + 1) * BRANCH_WIDTH, :],
                                         preferred_element_type=F32)
    o_ref[...] = x + jnp.dot(merged.astype(BF16), wo_ref[...], preferred_element_type=F32)


def _merge(x2, g, w_gate, y_a, y_bc, y_d, bp, w_out, tm):
    n, d = x2.shape
    full = lambda a: pl.BlockSpec(a.shape, lambda i: (0,) * a.ndim)
    rows = lambda a: pl.BlockSpec((tm, a.shape[1]), lambda i: (i, 0))
    return pl.pallas_call(
        _merge_kernel,
        grid=(n // tm,),
        in_specs=[rows(x2), full(g), full(w_gate), rows(y_a), rows(y_bc), rows(y_d), full(bp), full(w_out)],
        out_specs=pl.BlockSpec((tm, d), lambda i: (i, 0)),
        out_shape=jax.ShapeDtypeStruct((n, d), F32),
        compiler_params=_params("parallel"),
        name="merge",
    )(x2, g, w_gate, y_a, y_bc, y_d, bp, w_out)


def _router_kernel(x_ref, g_ref, r_ref, o_ref):
    n_exp = o_ref.shape[1]
    h = _rms(x_ref[...], g_ref[...], RMS_EPS)
    logits = jnp.dot(h, r_ref[...], precision=lax.Precision.HIGHEST, preferred_element_type=F32)
    lane = lax.broadcasted_iota(jnp.int32, logits.shape, 1).astype(F32)
    logits = jnp.where(lane < n_exp, logits, -jnp.inf)
    big = float(logits.shape[1])
    m1 = jnp.max(logits, axis=1, keepdims=True)
    i1 = jnp.min(jnp.where(logits == m1, lane, big), axis=1, keepdims=True)
    rest = jnp.where(lane == i1, -jnp.inf, logits)
    m2 = jnp.max(rest, axis=1, keepdims=True)
    i2 = jnp.min(jnp.where(rest == m2, lane, big), axis=1, keepdims=True)
    e2 = jnp.exp(m2 - m1)
    w1 = 1.0 / (1.0 + e2)
    w2 = e2 / (1.0 + e2)
    gates = jnp.where(lane == i1, w1, 0.0) + jnp.where(lane == i2, w2, 0.0)
    o_ref[...] = gates[:, 0:n_exp]


def _router(x2, g, router_pad, n_exp, tm):
    n, d = x2.shape
    return pl.pallas_call(
        _router_kernel,
        grid=(n // tm,),
        in_specs=[pl.BlockSpec((tm, d), lambda i: (i, 0)),
                  pl.BlockSpec((1, d), lambda i: (0, 0)),
                  pl.BlockSpec(router_pad.shape, lambda i: (0, 0))],
        out_specs=pl.BlockSpec((tm, n_exp), lambda i: (i, 0)),
        out_shape=jax.ShapeDtypeStruct((n, n_exp), F32),
        compiler_params=_params("parallel"),
        name="router",
    )(x2, g, router_pad)


def _ffn_kernel(x_ref, g_ref, gates_ref, w1_ref, w3_ref, w2_ref, fg_ref, o_ref, h_ref, acc_ref,
                *, gated, final_norm):
    e = pl.program_id(1)
    c = pl.program_id(2)
    first = (e == 0) & (c == 0)
    last = (e == pl.num_programs(1) - 1) & (c == pl.num_programs(2) - 1)

    @pl.when(first)
    def _():
        h_ref[...] = _rms(x_ref[...], g_ref[...], RMS_EPS).astype(BF16)
        acc_ref[...] = jnp.zeros(acc_ref.shape, F32)

    h = h_ref[...]
    a = jnp.dot(h, w1_ref[0], preferred_element_type=F32)
    b = jnp.dot(h, w3_ref[0], preferred_element_type=F32)
    t = (a * jax.nn.sigmoid(a) * b).astype(BF16)
    y = jnp.dot(t, w2_ref[0], preferred_element_type=F32)
    if gated:
        gates = gates_ref[...]
        lane = lax.broadcasted_iota(jnp.int32, gates.shape, 1)
        y = y * jnp.sum(jnp.where(lane == e, gates, 0.0), axis=1, keepdims=True)
    acc_ref[...] += y

    @pl.when(last)
    def _():
        out = x_ref[...] + acc_ref[...]
        if final_norm:
            out = _rms(out, fg_ref[...], RMS_EPS)
        o_ref[...] = out


def _ffn(x2, g, gates, w1, w3, w2, final_g, tm, fc, gated, final_norm):
    n, d = x2.shape
    n_exp, _, d_ff = w1.shape
    if gated:
        gates_spec = pl.BlockSpec((tm, gates.shape[1]), lambda i, e, c: (i, 0))
    else:
        gates_spec = pl.BlockSpec(gates.shape, lambda i, e, c: (0, 0))
    return pl.pallas_call(
        functools.partial(_ffn_kernel, gated=gated, final_norm=final_norm),
        grid=(n // tm, n_exp, d_ff // fc),
        in_specs=[pl.BlockSpec((tm, d), lambda i, e, c: (i, 0)),
                  pl.BlockSpec((1, d), lambda i, e, c: (0, 0)),
                  gates_spec,
                  pl.BlockSpec((1, d, fc), lambda i, e, c: (e, 0, c)),
                  pl.BlockSpec((1, d, fc), lambda i, e, c: (e, 0, c)),
                  pl.BlockSpec((1, fc, d), lambda i, e, c: (e, c, 0)),
                  pl.BlockSpec((1, d), lambda i, e, c: (0, 0))],
        out_specs=pl.BlockSpec((tm, d), lambda i, e, c: (i, 0)),
        out_shape=jax.ShapeDtypeStruct((n, d), F32),
        scratch_shapes=[pltpu.VMEM((tm, d), BF16), pltpu.VMEM((tm, d), F32)],
        compiler_params=_params("parallel", "arbitrary", "arbitrary"),
        name="moe_ffn" if gated else "dense_ffn",
    )(x2, g, gates, w1, w3, w2, final_g)


def _block_diag(w):
    g, c, _ = w.shape
    eye = jnp.eye(g, dtype=w.dtype)
    return (eye[:, None, :, None] * w[:, :, None, :]).reshape(g * c, g * c)


def _pick_tile(n, target):
    t = min(n, target)
    while n % t:
        t //= 2
    return t


def kernel(x, bias_table, mix_norm_g, w_in, conv_w, pool_w, pool_scale, diff_lambda, diff_subln_g,
           branch_proj, w_out, ffn_norm_g, dense_w1, dense_w3, dense_w2, moe_router, moe_w1, moe_w3,
           moe_w2, final_norm_g):
    batch, seq, d = x.shape
    depth = w_in.shape[0]
    n_mix = N_MIX_SLICES * BRANCH_WIDTH
    n = batch * seq
    assert seq % ATT_BLOCK == 0 and d % 128 == 0
    tm = _pick_tile(seq, 512)

    x2 = x.reshape(n, d)
    tiles = _bias_tiles(bias_table)
    row = lambda v: v.reshape(1, -1)
    final_g = row(final_norm_g)

    for i in range(depth):
        last_layer = i == depth - 1
        g_mix = row(mix_norm_g[i])
        w_mix = w_in[i, :, :n_mix].astype(BF16)
        w_gate = w_in[i, :, n_mix:].astype(BF16)
        u = _inproj(x2, g_mix, w_mix, tm)

        lam_init = 0.8 - 0.6 * math.exp(-0.3 * i)
        y_a = _moba(u, tiles, bias_table, batch, seq)
        y_d = _diff(u, tiles, bias_table, diff_lambda[i], row(jnp.tile(diff_subln_g[i], N_HEADS)),
                    batch, seq, lam_init)
        y_bc = _convpool(u, conv_w[i], _block_diag(pool_w[i]).astype(BF16), row(pool_scale[i]), seq, tm)
        x2 = _merge(x2, g_mix, w_gate, y_a, y_bc, y_d,
                    branch_proj[i].reshape(-1, d).astype(BF16), w_out[i].astype(BF16), tm)

        g_ffn = row(ffn_norm_g[i])
        j = i // 2
        if i % 2 == 0:
            ones = jnp.ones((8, 128), F32)
            d_ff = dense_w1.shape[2]
            fc = d_ff // 2 if (d_ff // 2) % 128 == 0 else d_ff
            x2 = _ffn(x2, g_ffn, ones, dense_w1[j][None].astype(BF16), dense_w3[j][None].astype(BF16),
                      dense_w2[j][None].astype(BF16), final_g, tm, fc, gated=False, final_norm=last_layer)
        else:
            n_exp = moe_router.shape[2]
            router_pad = jnp.pad(moe_router[j], ((0, 0), (0, 128 - n_exp)))
            gates = _router(x2, g_ffn, router_pad, n_exp, tm)
            d_ff = moe_w1.shape[3]
            fc = _pick_tile(d_ff, 512)
            x2 = _ffn(x2, g_ffn, gates, moe_w1[j].astype(BF16), moe_w3[j].astype(BF16),
                      moe_w2[j].astype(BF16), final_g, _pick_tile(n, 1024), fc, gated=True,
                      final_norm=last_layer)

    return x2.reshape(batch, seq, d)
```

```python
import functools
import math

import numpy as np
import jax
import jax.numpy as jnp
from jax import lax
from jax.experimental import pallas as pl
from jax.experimental.pallas import tpu as pltpu

F32 = jnp.float32
BF16 = jnp.bfloat16

BRANCH_WIDTH = 256
N_MIX_SLICES = 10
HEAD_WIDTH = 64
N_HEADS = 4
DIFF_QK_DIM = 32
ATT_BLOCK = 256
MOBA_TOPK = 3
CONV_WIDTH = 3
POOL_WINDOWS = (2, 4, 8, 16)
POOL_GROUP = 64
HALO = 16
REL_BUCKETS = 32
REL_MAX_DIST = 128
TOP_K_EXPERTS = 2
RMS_EPS = 1e-6
SUBLN_EPS = 1e-5
NEG_INF = -1e30
V7X_VMEM_BYTES = 64 * 1024 * 1024
VMEM_LIMIT = V7X_VMEM_BYTES - 8 * 1024 * 1024

_TRANS_B = (((1,), (1,)), ((), ()))


def _params(*sem):
    return pltpu.CompilerParams(dimension_semantics=sem, vmem_limit_bytes=VMEM_LIMIT)


def _rms(x, g, eps):
    r = lax.rsqrt(jnp.mean(x * x, axis=-1, keepdims=True) + eps)
    return x * r * g


def _inproj_kernel(x_ref, g_ref, w_ref, o_ref):
    h = _rms(x_ref[...], g_ref[...], RMS_EPS).astype(BF16)
    o_ref[...] = jnp.dot(h, w_ref[...], preferred_element_type=F32)


def _inproj(x2, g, w, tm):
    n, d = x2.shape
    wn = w.shape[1]
    return pl.pallas_call(
        _inproj_kernel,
        grid=(n // tm,),
        in_specs=[pl.BlockSpec((tm, d), lambda i: (i, 0)),
                  pl.BlockSpec((1, d), lambda i: (0, 0)),
                  pl.BlockSpec((d, wn), lambda i: (0, 0))],
        out_specs=pl.BlockSpec((tm, wn), lambda i: (i, 0)),
        out_shape=jax.ShapeDtypeStruct((n, wn), F32),
        compiler_params=_params("parallel"),
        name="inproj",
    )(x2, g, w)


def _rel_bucket_np(dist):
    n = np.maximum(dist, 0)
    max_exact = REL_BUCKETS // 2
    nf = np.maximum(n, max_exact).astype(np.float32)
    large = max_exact + (np.log(nf / np.float32(max_exact)) / np.float32(math.log(REL_MAX_DIST / max_exact))
                         * np.float32(REL_BUCKETS - max_exact)).astype(np.int32)
    large = np.minimum(large, REL_BUCKETS - 1)
    return np.where(n < max_exact, n, large).astype(np.int32)


def _bucket_tiles():
    i = np.arange(ATT_BLOCK)[:, None]
    j = np.arange(ATT_BLOCK)[None, :]
    return np.stack([_rel_bucket_np(i - j), _rel_bucket_np(ATT_BLOCK + i - j)])


def _bias_tiles_kernel(tab_ref, bkt_ref, o_ref):
    h = pl.program_id(0)
    bkt = bkt_ref[...]
    acc = jnp.zeros(bkt.shape, F32)
    for b in range(REL_BUCKETS):
        acc = jnp.where(bkt == b, tab_ref[b, h], acc)
    o_ref[0] = acc - tab_ref[REL_BUCKETS - 1, h]


def _bias_tiles(bias_table):
    n_heads = bias_table.shape[1]
    bkt = jnp.asarray(_bucket_tiles())
    return pl.pallas_call(
        _bias_tiles_kernel,
        grid=(n_heads,),
        in_specs=[pl.BlockSpec(memory_space=pltpu.SMEM),
                  pl.BlockSpec((2, ATT_BLOCK, ATT_BLOCK), lambda h: (0, 0, 0))],
        out_specs=pl.BlockSpec((1, 2, ATT_BLOCK, ATT_BLOCK), lambda h: (h, 0, 0, 0)),
        out_shape=jax.ShapeDtypeStruct((n_heads, 2, ATT_BLOCK, ATT_BLOCK), F32),
        compiler_params=_params("arbitrary"),
        name="bias_tiles",
    )(bias_table, bkt)


def _attend(q16, kj, vj, acc_ref, slot, state, bias=None, keep=None):
    s = lax.dot_general(q16, kj, _TRANS_B, preferred_element_type=F32)
    if bias is not None:
        s = s + bias
    if keep is not None:
        s = jnp.where(keep, s, NEG_INF)
    m_tile = jnp.max(s, axis=1, keepdims=True)
    if state is None:
        p = jnp.exp(s - m_tile)
        acc_ref[slot] = jnp.dot(p.astype(BF16), vj, preferred_element_type=F32)
        return m_tile, jnp.sum(p, axis=1, keepdims=True)
    m, l = state
    m_new = jnp.maximum(m, m_tile)
    alpha = jnp.exp(m - m_new)
    p = jnp.exp(s - m_new)
    acc_ref[slot] = alpha * acc_ref[slot] + jnp.dot(p.astype(BF16), vj, preferred_element_type=F32)
    return m_new, alpha * l + jnp.sum(p, axis=1, keepdims=True)


def _head_lanes(lane, h):
    return (lane >= h * HEAD_WIDTH) & (lane < (h + 1) * HEAD_WIDTH)


def _moba_kernel(q_ref, k_ref, v_ref, tiles_ref, o_ref, kb_ref, vb_ref, kmean_ref, acc_ref):
    t = ATT_BLOCK
    n_blk = k_ref.shape[0] // t
    qb = pl.program_id(1)

    @pl.when(qb == 0)
    def _():
        kb_ref[...] = k_ref[...].astype(BF16)
        vb_ref[...] = v_ref[...].astype(BF16)
        kmean_ref[...] = jnp.zeros(kmean_ref.shape, F32)
        for j in range(n_blk):
            kmean_ref[j:j + 1, :] = jnp.mean(k_ref[j * t:(j + 1) * t, :], axis=0, keepdims=True)

    lane = lax.broadcasted_iota(jnp.int32, (t, BRANCH_WIDTH), 1)
    causal = (lax.broadcasted_iota(jnp.int32, (t, t), 0) >= lax.broadcasted_iota(jnp.int32, (t, t), 1))
    glane = lax.broadcasted_iota(jnp.int32, (t, kmean_ref.shape[0]), 1)
    past = glane < qb
    qf = q_ref[...] * (HEAD_WIDTH ** -0.5)
    heads = range(N_HEADS)
    in_head = [_head_lanes(lane, h) for h in heads]
    q32 = [jnp.where(in_head[h], qf, 0.0) for h in heads]
    q16 = [q.astype(BF16) for q in q32]
    gate = [lax.dot_general(q32[h], kmean_ref[...], _TRANS_B, precision=lax.Precision.HIGHEST,
                            preferred_element_type=F32) for h in heads]

    def keep_block(h, j):
        gj = jnp.sum(jnp.where(glane == j, gate[h], 0.0), axis=1, keepdims=True)
        beats = past & ((gate[h] > gj) | ((gate[h] == gj) & (glane < j)))
        return jnp.sum(beats.astype(F32), axis=1, keepdims=True) < MOBA_TOPK

    def kv(j):
        j0 = pl.multiple_of(j * t, t)
        return kb_ref[pl.ds(j0, t), :], vb_ref[pl.ds(j0, t), :]

    k_own, v_own = kv(qb)
    state = [_attend(q16[h], k_own, v_own, acc_ref, h, None, bias=tiles_ref[h, 0], keep=causal)
             for h in heads]
    has_prev = qb >= 1
    jp = jnp.maximum(qb - 1, 0)
    k_prev, v_prev = kv(jp)
    state = [_attend(q16[h], k_prev, v_prev, acc_ref, h, state[h], bias=tiles_ref[h, 1],
                     keep=keep_block(h, jp) & has_prev) for h in heads]

    def body(j, carry):
        kj, vj = kv(j)
        new = [_attend(q16[h], kj, vj, acc_ref, h, (carry[2 * h], carry[2 * h + 1]), keep=keep_block(h, j))
               for h in heads]
        return tuple(x for st in new for x in st)

    flat = lax.fori_loop(0, qb - 1, body, tuple(x for st in state for x in st))
    out = jnp.zeros((t, BRANCH_WIDTH), F32)
    for h in heads:
        out = jnp.where(in_head[h], acc_ref[h] / flat[2 * h + 1], out)
    o_ref[...] = out.astype(o_ref.dtype)


def _moba(u, tiles, batch, seq):
    t = ATT_BLOCK
    nq = seq // t
    return pl.pallas_call(
        _moba_kernel,
        grid=(batch, nq),
        in_specs=[pl.BlockSpec((t, BRANCH_WIDTH), lambda b, q: (b * nq + q, 0)),
                  pl.BlockSpec((seq, BRANCH_WIDTH), lambda b, q: (b, 1)),
                  pl.BlockSpec((seq, BRANCH_WIDTH), lambda b, q: (b, 2)),
                  pl.BlockSpec((N_HEADS, 2, t, t), lambda b, q: (0, 0, 0, 0))],
        out_specs=pl.BlockSpec((t, BRANCH_WIDTH), lambda b, q: (b * nq + q, 0)),
        out_shape=jax.ShapeDtypeStruct((batch * seq, BRANCH_WIDTH), BF16),
        scratch_shapes=[pltpu.VMEM((seq, BRANCH_WIDTH), BF16),
                        pltpu.VMEM((seq, BRANCH_WIDTH), BF16),
                        pltpu.VMEM((128, BRANCH_WIDTH), F32),
                        pltpu.VMEM((N_HEADS, t, BRANCH_WIDTH), F32)],
        compiler_params=_params("arbitrary", "arbitrary"),
        name="moba",
    )(u, u, u, tiles)


def _diff_kernel(lam_ref, g_ref, q_ref, k_ref, v_ref, tiles_ref, o_ref, kb_ref, vb_ref, acc_ref, *, lam_init):
    t = ATT_BLOCK
    qb = pl.program_id(1)

    @pl.when(qb == 0)
    def _():
        kb_ref[...] = k_ref[...].astype(BF16)
        vb_ref[...] = v_ref[...].astype(BF16)

    lp = lam_ref[...]
    lam = (jnp.exp(jnp.sum(lp[0:1] * lp[1:2], axis=1, keepdims=True))
           - jnp.exp(jnp.sum(lp[2:3] * lp[3:4], axis=1, keepdims=True)) + lam_init)

    lane = lax.broadcasted_iota(jnp.int32, (t, BRANCH_WIDTH), 1)
    causal = (lax.broadcasted_iota(jnp.int32, (t, t), 0) >= lax.broadcasted_iota(jnp.int32, (t, t), 1))
    qf = q_ref[...] * (DIFF_QK_DIM ** -0.5)
    chains = range(2 * N_HEADS)
    q16 = []
    for h in range(N_HEADS):
        for c in range(2):
            lo = h * HEAD_WIDTH + c * DIFF_QK_DIM
            q16.append(jnp.where((lane >= lo) & (lane < lo + DIFF_QK_DIM), qf, 0.0).astype(BF16))

    def kv(j):
        j0 = pl.multiple_of(j * t, t)
        return kb_ref[pl.ds(j0, t), :], vb_ref[pl.ds(j0, t), :]

    k_own, v_own = kv(qb)
    state = [_attend(q16[i], k_own, v_own, acc_ref, i, None, bias=tiles_ref[i // 2, 0], keep=causal)
             for i in chains]
    has_prev = qb >= 1
    k_prev, v_prev = kv(jnp.maximum(qb - 1, 0))
    state = [_attend(q16[i], k_prev, v_prev, acc_ref, i, state[i], bias=tiles_ref[i // 2, 1], keep=has_prev)
             for i in chains]

    def body(j, carry):
        kj, vj = kv(j)
        new = [_attend(q16[i], kj, vj, acc_ref, i, (carry[2 * i], carry[2 * i + 1])) for i in chains]
        return tuple(x for st in new for x in st)

    flat = lax.fori_loop(0, qb - 1, body, tuple(x for st in state for x in st))
    out = jnp.zeros((t, BRANCH_WIDTH), F32)
    for h in range(N_HEADS):
        in_head = _head_lanes(lane, h)
        o = acc_ref[2 * h] / flat[4 * h + 1] - lam * (acc_ref[2 * h + 1] / flat[4 * h + 3])
        ms_h = jnp.sum(jnp.where(in_head, o * o, 0.0), axis=1, keepdims=True) * (1.0 / HEAD_WIDTH)
        out = jnp.where(in_head, o * lax.rsqrt(ms_h + SUBLN_EPS), out)
    o_ref[...] = (out * g_ref[...] * (1.0 - lam_init)).astype(o_ref.dtype)


def _diff(u, tiles, lam_params, subln_g4, batch, seq, lam_init):
    t = ATT_BLOCK
    nq = seq // t
    return pl.pallas_call(
        functools.partial(_diff_kernel, lam_init=lam_init),
        grid=(batch, nq),
        in_specs=[pl.BlockSpec((4, DIFF_QK_DIM), lambda b, q: (0, 0)),
                  pl.BlockSpec((1, BRANCH_WIDTH), lambda b, q: (0, 0)),
                  pl.BlockSpec((t, BRANCH_WIDTH), lambda b, q: (b * nq + q, 7)),
                  pl.BlockSpec((seq, BRANCH_WIDTH), lambda b, q: (b, 8)),
                  pl.BlockSpec((seq, BRANCH_WIDTH), lambda b, q: (b, 9)),
                  pl.BlockSpec((N_HEADS, 2, t, t), lambda b, q: (1, 0, 0, 0))],
        out_specs=pl.BlockSpec((t, BRANCH_WIDTH), lambda b, q: (b * nq + q, 0)),
        out_shape=jax.ShapeDtypeStruct((batch * seq, BRANCH_WIDTH), BF16),
        scratch_shapes=[pltpu.VMEM((seq, BRANCH_WIDTH), BF16),
                        pltpu.VMEM((seq, BRANCH_WIDTH), BF16),
                        pltpu.VMEM((2 * N_HEADS, t, BRANCH_WIDTH), F32)],
        compiler_params=_params("arbitrary", "arbitrary"),
        name="diff_attn",
    )(lam_params, subln_g4, u, u, u, tiles)


def _convpool_kernel(xb_ref, bb_ref, cb_ref, pc_ref, xh_ref, ch_ref, ph_ref, cw_ref, pw_ref, ps_ref,
                     o_ref, u_ref, s_ref, *, seq):
    tm = xb_ref.shape[0]
    pos0 = (pl.program_id(0) * tm) % seq
    has_history = pos0 > 0
    pos = pos0 + lax.broadcasted_iota(jnp.int32, (tm, 1), 0)
    lane = lax.broadcasted_iota(jnp.int32, (tm, BRANCH_WIDTH), 1)

    u_ref[0:HALO, :] = jnp.where(has_history, ch_ref[...] * xh_ref[...], 0.0)
    u_ref[HALO:, :] = cb_ref[...] * xb_ref[...]
    conv = cw_ref[CONV_WIDTH - 1:CONV_WIDTH, :] * u_ref[HALO:, :]
    for i in range(CONV_WIDTH - 1):
        shift = CONV_WIDTH - 1 - i
        conv = conv + cw_ref[i:i + 1, :] * u_ref[HALO - shift:HALO - shift + tm, :]
    o_ref[:, 0:BRANCH_WIDTH] = (bb_ref[...] * conv).astype(o_ref.dtype)

    s_ref[0:HALO, :] = jnp.where(has_history, ph_ref[...], 0.0)
    s_ref[HALO:, :] = pc_ref[...]
    pooled = jnp.zeros((tm, BRANCH_WIDTH), F32)
    done = 0
    for g, w in enumerate(POOL_WINDOWS):
        half = w // 2
        cur = s_ref[done + half:, :] + s_ref[done:HALO + tm - half, :]
        done += half
        s_ref[done:, :] = cur
        cnt = jnp.minimum(pos + 1, w).astype(F32)
        mean_w = s_ref[HALO:, :] / cnt
        pooled = jnp.where((lane >= g * POOL_GROUP) & (lane < (g + 1) * POOL_GROUP), mean_w, pooled)
    pooled = pooled - pc_ref[...]
    y_c = jnp.dot(pooled.astype(BF16), pw_ref[...], preferred_element_type=F32) * ps_ref[...]
    o_ref[:, BRANCH_WIDTH:] = y_c.astype(o_ref.dtype)


def _convpool(u, conv_w, pool_w_bd, pool_scale, seq, tm):
    n = u.shape[0]
    rows = lambda c: pl.BlockSpec((tm, BRANCH_WIDTH), lambda i, c=c: (i, c))
    halo = lambda c: pl.BlockSpec((HALO, BRANCH_WIDTH),
                                  lambda i, c=c: (jnp.maximum(i * (tm // HALO) - 1, 0), c))
    full = lambda a: pl.BlockSpec(a.shape, lambda i: (0,) * a.ndim)
    return pl.pallas_call(
        functools.partial(_convpool_kernel, seq=seq),
        grid=(n // tm,),
        in_specs=[rows(3), rows(4), rows(5), rows(6), halo(3), halo(5), halo(6),
                  full(conv_w), full(pool_w_bd), full(pool_scale)],
        out_specs=pl.BlockSpec((tm, 2 * BRANCH_WIDTH), lambda i: (i, 0)),
        out_shape=jax.ShapeDtypeStruct((n, 2 * BRANCH_WIDTH), BF16),
        scratch_shapes=[pltpu.VMEM((HALO + tm, BRANCH_WIDTH), F32),
                        pltpu.VMEM((HALO + tm, BRANCH_WIDTH), F32)],
        compiler_params=_params("parallel"),
        name="convpool",
    )(u, u, u, u, u, u, u, conv_w, pool_w_bd, pool_scale)


def _merge_kernel(x_ref, g_ref, wg_ref, ya_ref, ybc_ref, yd_ref, bp_ref, wo_ref, o_ref):
    d = x_ref.shape[1]
    x = x_ref[...]
    h = _rms(x, g_ref[...], RMS_EPS).astype(BF16)
    ys = (ya_ref[...], ybc_ref[:, 0:BRANCH_WIDTH], ybc_ref[:, BRANCH_WIDTH:], yd_ref[...])
    merged = jnp.zeros(x.shape, F32)
    for b, y in enumerate(ys):
        gate = jax.nn.sigmoid(jnp.dot(h, wg_ref[:, b * d:(b + 1) * d], preferred_element_type=F32))
        merged = merged + gate * jnp.dot(y, bp_ref[b * BRANCH_WIDTH:(b + 1) * BRANCH_WIDTH, :],
                                         preferred_element_type=F32)
    o_ref[...] = x + jnp.dot(merged.astype(BF16), wo_ref[...], preferred_element_type=F32)


def _merge(x2, g, w_gate, y_a, y_bc, y_d, bp, w_out, tm):
    n, d = x2.shape
    full = lambda a: pl.BlockSpec(a.shape, lambda i: (0,) * a.ndim)
    rows = lambda a: pl.BlockSpec((tm, a.shape[1]), lambda i: (i, 0))
    return pl.pallas_call(
        _merge_kernel,
        grid=(n // tm,),
        in_specs=[rows(x2), full(g), full(w_gate), rows(y_a), rows(y_bc), rows(y_d), full(bp), full(w_out)],
        out_specs=pl.BlockSpec((tm, d), lambda i: (i, 0)),
        out_shape=jax.ShapeDtypeStruct((n, d), F32),
        compiler_params=_params("parallel"),
        name="merge",
    )(x2, g, w_gate, y_a, y_bc, y_d, bp, w_out)


def _router_kernel(x_ref, g_ref, r_ref, o_ref):
    n_exp = o_ref.shape[1]
    h = _rms(x_ref[...], g_ref[...], RMS_EPS)
    logits = jnp.dot(h, r_ref[...], precision=lax.Precision.HIGHEST, preferred_element_type=F32)
    lane = lax.broadcasted_iota(jnp.int32, logits.shape, 1).astype(F32)
    logits = jnp.where(lane < n_exp, logits, -jnp.inf)
    big = float(logits.shape[1])
    m1 = jnp.max(logits, axis=1, keepdims=True)
    i1 = jnp.min(jnp.where(logits == m1, lane, big), axis=1, keepdims=True)
    rest = jnp.where(lane == i1, -jnp.inf, logits)
    m2 = jnp.max(rest, axis=1, keepdims=True)
    i2 = jnp.min(jnp.where(rest == m2, lane, big), axis=1, keepdims=True)
    e2 = jnp.exp(m2 - m1)
    w1 = 1.0 / (1.0 + e2)
    w2 = e2 / (1.0 + e2)
    gates = jnp.where(lane == i1, w1, 0.0) + jnp.where(lane == i2, w2, 0.0)
    o_ref[...] = gates[:, 0:n_exp]


def _router(x2, g, router_pad, n_exp, tm):
    n, d = x2.shape
    return pl.pallas_call(
        _router_kernel,
        grid=(n // tm,),
        in_specs=[pl.BlockSpec((tm, d), lambda i: (i, 0)),
                  pl.BlockSpec((1, d), lambda i: (0, 0)),
                  pl.BlockSpec(router_pad.shape, lambda i: (0, 0))],
        out_specs=pl.BlockSpec((tm, n_exp), lambda i: (i, 0)),
        out_shape=jax.ShapeDtypeStruct((n, n_exp), F32),
        compiler_params=_params("parallel"),
        name="router",
    )(x2, g, router_pad)


def _ffn_kernel(x_ref, g_ref, gates_ref, w1_ref, w3_ref, w2_ref, fg_ref, o_ref, h_ref, acc_ref,
                *, gated, final_norm):
    e = pl.program_id(1)
    c = pl.program_id(2)
    first = (e == 0) & (c == 0)
    last = (e == pl.num_programs(1) - 1) & (c == pl.num_programs(2) - 1)

    @pl.when(first)
    def _():
        h_ref[...] = _rms(x_ref[...], g_ref[...], RMS_EPS).astype(BF16)
        acc_ref[...] = jnp.zeros(acc_ref.shape, F32)

    h = h_ref[...]
    a = jnp.dot(h, w1_ref[0], preferred_element_type=F32)
    b = jnp.dot(h, w3_ref[0], preferred_element_type=F32)
    t = (a * jax.nn.sigmoid(a) * b).astype(BF16)
    y = jnp.dot(t, w2_ref[0], preferred_element_type=F32)
    if gated:
        gates = gates_ref[...]
        lane = lax.broadcasted_iota(jnp.int32, gates.shape, 1)
        y = y * jnp.sum(jnp.where(lane == e, gates, 0.0), axis=1, keepdims=True)
    acc_ref[...] += y

    @pl.when(last)
    def _():
        out = x_ref[...] + acc_ref[...]
        if final_norm:
            out = _rms(out, fg_ref[...], RMS_EPS)
        o_ref[...] = out


def _ffn(x2, g, gates, w1, w3, w2, final_g, tm, fc, gated, final_norm):
    n, d = x2.shape
    n_exp, _, d_ff = w1.shape
    if gated:
        gates_spec = pl.BlockSpec((tm, gates.shape[1]), lambda i, e, c: (i, 0))
    else:
        gates_spec = pl.BlockSpec(gates.shape, lambda i, e, c: (0, 0))
    return pl.pallas_call(
        functools.partial(_ffn_kernel, gated=gated, final_norm=final_norm),
        grid=(n // tm, n_exp, d_ff // fc),
        in_specs=[pl.BlockSpec((tm, d), lambda i, e, c: (i, 0)),
                  pl.BlockSpec((1, d), lambda i, e, c: (0, 0)),
                  gates_spec,
                  pl.BlockSpec((1, d, fc), lambda i, e, c: (e, 0, c)),
                  pl.BlockSpec((1, d, fc), lambda i, e, c: (e, 0, c)),
                  pl.BlockSpec((1, fc, d), lambda i, e, c: (e, c, 0)),
                  pl.BlockSpec((1, d), lambda i, e, c: (0, 0))],
        out_specs=pl.BlockSpec((tm, d), lambda i, e, c: (i, 0)),
        out_shape=jax.ShapeDtypeStruct((n, d), F32),
        scratch_shapes=[pltpu.VMEM((tm, d), BF16), pltpu.VMEM((tm, d), F32)],
        compiler_params=_params("parallel", "arbitrary", "arbitrary"),
        name="moe_ffn" if gated else "dense_ffn",
    )(x2, g, gates, w1, w3, w2, final_g)


def _block_diag(w):
    g, c, _ = w.shape
    eye = jnp.eye(g, dtype=w.dtype)
    return (eye[:, None, :, None] * w[:, :, None, :]).reshape(g * c, g * c)


def _pick_tile(n, target):
    t = min(n, target)
    while n % t:
        t //= 2
    return t


def kernel(x, bias_table, mix_norm_g, w_in, conv_w, pool_w, pool_scale, diff_lambda, diff_subln_g,
           branch_proj, w_out, ffn_norm_g, dense_w1, dense_w3, dense_w2, moe_router, moe_w1, moe_w3,
           moe_w2, final_norm_g):
    batch, seq, d = x.shape
    depth = w_in.shape[0]
    n_mix = N_MIX_SLICES * BRANCH_WIDTH
    n = batch * seq
    assert seq % ATT_BLOCK == 0 and d % 128 == 0
    tm = _pick_tile(seq, 512)

    x2 = x.reshape(n, d)
    tiles = _bias_tiles(bias_table)
    row = lambda v: v.reshape(1, -1)
    final_g = row(final_norm_g)

    for i in range(depth):
        last_layer = i == depth - 1
        g_mix = row(mix_norm_g[i])
        w_mix = w_in[i, :, :n_mix].astype(BF16)
        w_gate = w_in[i, :, n_mix:].astype(BF16)
        u = _inproj(x2, g_mix, w_mix, tm)

        lam_init = 0.8 - 0.6 * math.exp(-0.3 * i)
        y_a = _moba(u, tiles, batch, seq)
        y_d = _diff(u, tiles, diff_lambda[i], row(jnp.tile(diff_subln_g[i], N_HEADS)), batch, seq, lam_init)
        y_bc = _convpool(u, conv_w[i], _block_diag(pool_w[i]).astype(BF16), row(pool_scale[i]), seq, tm)
        x2 = _merge(x2, g_mix, w_gate, y_a, y_bc, y_d,
                    branch_proj[i].reshape(-1, d).astype(BF16), w_out[i].astype(BF16), tm)

        g_ffn = row(ffn_norm_g[i])
        j = i // 2
        if i % 2 == 0:
            ones = jnp.ones((8, 128), F32)
            d_ff = dense_w1.shape[2]
            fc = d_ff // 2 if (d_ff // 2) % 128 == 0 else d_ff
            x2 = _ffn(x2, g_ffn, ones, dense_w1[j][None].astype(BF16), dense_w3[j][None].astype(BF16),
                      dense_w2[j][None].astype(BF16), final_g, tm, fc, gated=False, final_norm=last_layer)
        else:
            n_exp = moe_router.shape[2]
            router_pad = jnp.pad(moe_router[j], ((0, 0), (0, 128 - n_exp)))
            gates = _router(x2, g_ffn, router_pad, n_exp, tm)
            d_ff = moe_w1.shape[3]
            fc = _pick_tile(d_ff, 512)
            x2 = _ffn(x2, g_ffn, gates, moe_w1[j].astype(BF16), moe_w3[j].astype(BF16),
                      moe_w2[j].astype(BF16), final_g, _pick_tile(n, 1024), fc, gated=True,
                      final_norm=last_layer)

    return x2.reshape(batch, seq, d)
```

```python
import functools
import math

import numpy as np
import jax
import jax.numpy as jnp
from jax import lax
from jax.experimental import pallas as pl
from jax.experimental.pallas import tpu as pltpu
from jax.experimental.pallas import tpu_sc as plsc

F32 = jnp.float32
BF16 = jnp.bfloat16

BRANCH_WIDTH = 256
N_MIX_SLICES = 10
HEAD_WIDTH = 64
N_HEADS = 4
DIFF_QK_DIM = 32
ATT_BLOCK = 256
MOBA_TOPK = 3
CONV_WIDTH = 3
POOL_WINDOWS = (2, 4, 8, 16)
POOL_GROUP = 64
HALO = 16
REL_BUCKETS = 32
REL_MAX_DIST = 128
TOP_K_EXPERTS = 2
RMS_EPS = 1e-6
SUBLN_EPS = 1e-5
NEG_INF = -1e30
V7X_VMEM_BYTES = 64 * 1024 * 1024
VMEM_LIMIT = V7X_VMEM_BYTES - 8 * 1024 * 1024

_TRANS_B = (((1,), (1,)), ((), ()))


def _params(*sem):
    return pltpu.CompilerParams(dimension_semantics=sem, vmem_limit_bytes=VMEM_LIMIT)


def _rms(x, g, eps):
    r = lax.rsqrt(jnp.mean(x * x, axis=-1, keepdims=True) + eps)
    return x * r * g


def _inproj_kernel(x_ref, g_ref, w_ref, o_ref):
    h = _rms(x_ref[...], g_ref[...], RMS_EPS).astype(BF16)
    o_ref[...] = jnp.dot(h, w_ref[...], preferred_element_type=F32)


def _inproj(x2, g, w, tm):
    n, d = x2.shape
    wn = w.shape[1]
    return pl.pallas_call(
        _inproj_kernel,
        grid=(n // tm,),
        in_specs=[pl.BlockSpec((tm, d), lambda i: (i, 0)),
                  pl.BlockSpec((1, d), lambda i: (0, 0)),
                  pl.BlockSpec((d, wn), lambda i: (0, 0))],
        out_specs=pl.BlockSpec((tm, wn), lambda i: (i, 0)),
        out_shape=jax.ShapeDtypeStruct((n, wn), F32),
        compiler_params=_params("parallel"),
        name="inproj",
    )(x2, g, w)


def _rel_bucket_np(dist):
    n = np.maximum(dist, 0)
    max_exact = REL_BUCKETS // 2
    nf = np.maximum(n, max_exact).astype(np.float32)
    large = max_exact + (np.log(nf / np.float32(max_exact)) / np.float32(math.log(REL_MAX_DIST / max_exact))
                         * np.float32(REL_BUCKETS - max_exact)).astype(np.int32)
    large = np.minimum(large, REL_BUCKETS - 1)
    return np.where(n < max_exact, n, large).astype(np.int32)


def _bucket_tiles():
    i = np.arange(ATT_BLOCK)[:, None]
    j = np.arange(ATT_BLOCK)[None, :]
    return np.stack([_rel_bucket_np(i - j), _rel_bucket_np(ATT_BLOCK + i - j)])


def _bias_tiles_kernel(tab_ref, bkt_ref, o_ref):
    h = pl.program_id(0)
    bkt = bkt_ref[...]
    acc = jnp.zeros(bkt.shape, F32)
    for b in range(REL_BUCKETS):
        acc = jnp.where(bkt == b, tab_ref[b, h], acc)
    o_ref[0] = acc - tab_ref[REL_BUCKETS - 1, h]


def _bias_tiles(bias_table):
    n_heads = bias_table.shape[1]
    bkt = jnp.asarray(_bucket_tiles())
    return pl.pallas_call(
        _bias_tiles_kernel,
        grid=(n_heads,),
        in_specs=[pl.BlockSpec(memory_space=pltpu.SMEM),
                  pl.BlockSpec((2, ATT_BLOCK, ATT_BLOCK), lambda h: (0, 0, 0))],
        out_specs=pl.BlockSpec((1, 2, ATT_BLOCK, ATT_BLOCK), lambda h: (h, 0, 0, 0)),
        out_shape=jax.ShapeDtypeStruct((n_heads, 2, ATT_BLOCK, ATT_BLOCK), F32),
        compiler_params=_params("arbitrary"),
        name="bias_tiles",
    )(bias_table, bkt)


def _attend(q16, kj, vj, acc_ref, slot, state, bias=None, keep=None):
    s = lax.dot_general(q16, kj, _TRANS_B, preferred_element_type=F32)
    if bias is not None:
        s = s + bias
    if keep is not None:
        s = jnp.where(keep, s, NEG_INF)
    m_tile = jnp.max(s, axis=1, keepdims=True)
    if state is None:
        p = jnp.exp(s - m_tile)
        acc_ref[slot] = jnp.dot(p.astype(BF16), vj, preferred_element_type=F32)
        return m_tile, jnp.sum(p, axis=1, keepdims=True)
    m, l = state
    m_new = jnp.maximum(m, m_tile)
    alpha = jnp.exp(m - m_new)
    p = jnp.exp(s - m_new)
    acc_ref[slot] = alpha * acc_ref[slot] + jnp.dot(p.astype(BF16), vj, preferred_element_type=F32)
    return m_new, alpha * l + jnp.sum(p, axis=1, keepdims=True)


def _head_lanes(lane, h):
    return (lane >= h * HEAD_WIDTH) & (lane < (h + 1) * HEAD_WIDTH)


def _moba_kernel(q_ref, k_ref, v_ref, tiles_ref, o_ref, kb_ref, vb_ref, kmean_ref, acc_ref):
    t = ATT_BLOCK
    n_blk = k_ref.shape[0] // t
    qb = pl.program_id(1)

    @pl.when(qb == 0)
    def _():
        kb_ref[...] = k_ref[...].astype(BF16)
        vb_ref[...] = v_ref[...].astype(BF16)
        kmean_ref[...] = jnp.zeros(kmean_ref.shape, F32)
        for j in range(n_blk):
            kmean_ref[j:j + 1, :] = jnp.mean(k_ref[j * t:(j + 1) * t, :], axis=0, keepdims=True)

    lane = lax.broadcasted_iota(jnp.int32, (t, BRANCH_WIDTH), 1)
    causal = (lax.broadcasted_iota(jnp.int32, (t, t), 0) >= lax.broadcasted_iota(jnp.int32, (t, t), 1))
    glane = lax.broadcasted_iota(jnp.int32, (t, kmean_ref.shape[0]), 1)
    past = glane < qb
    qf = q_ref[...] * (HEAD_WIDTH ** -0.5)
    heads = range(N_HEADS)
    in_head = [_head_lanes(lane, h) for h in heads]
    q32 = [jnp.where(in_head[h], qf, 0.0) for h in heads]
    q16 = [q.astype(BF16) for q in q32]
    gate = [lax.dot_general(q32[h], kmean_ref[...], _TRANS_B, precision=lax.Precision.HIGHEST,
                            preferred_element_type=F32) for h in heads]

    def keep_block(h, j):
        gj = jnp.sum(jnp.where(glane == j, gate[h], 0.0), axis=1, keepdims=True)
        beats = past & ((gate[h] > gj) | ((gate[h] == gj) & (glane < j)))
        return jnp.sum(beats.astype(F32), axis=1, keepdims=True) < MOBA_TOPK

    def kv(j):
        j0 = pl.multiple_of(j * t, t)
        return kb_ref[pl.ds(j0, t), :], vb_ref[pl.ds(j0, t), :]

    k_own, v_own = kv(qb)
    state = [_attend(q16[h], k_own, v_own, acc_ref, h, None, bias=tiles_ref[h, 0], keep=causal)
             for h in heads]
    has_prev = qb >= 1
    jp = jnp.maximum(qb - 1, 0)
    k_prev, v_prev = kv(jp)
    state = [_attend(q16[h], k_prev, v_prev, acc_ref, h, state[h], bias=tiles_ref[h, 1],
                     keep=keep_block(h, jp) & has_prev) for h in heads]

    def body(j, carry):
        kj, vj = kv(j)
        new = [_attend(q16[h], kj, vj, acc_ref, h, (carry[2 * h], carry[2 * h + 1]), keep=keep_block(h, j))
               for h in heads]
        return tuple(x for st in new for x in st)

    flat = lax.fori_loop(0, qb - 1, body, tuple(x for st in state for x in st))
    out = jnp.zeros((t, BRANCH_WIDTH), F32)
    for h in heads:
        out = jnp.where(in_head[h], acc_ref[h] / flat[2 * h + 1], out)
    o_ref[...] = out.astype(o_ref.dtype)


def _moba(u, tiles, batch, seq):
    t = ATT_BLOCK
    nq = seq // t
    return pl.pallas_call(
        _moba_kernel,
        grid=(batch, nq),
        in_specs=[pl.BlockSpec((t, BRANCH_WIDTH), lambda b, q: (b * nq + q, 0)),
                  pl.BlockSpec((seq, BRANCH_WIDTH), lambda b, q: (b, 1)),
                  pl.BlockSpec((seq, BRANCH_WIDTH), lambda b, q: (b, 2)),
                  pl.BlockSpec((N_HEADS, 2, t, t), lambda b, q: (0, 0, 0, 0))],
        out_specs=pl.BlockSpec((t, BRANCH_WIDTH), lambda b, q: (b * nq + q, 0)),
        out_shape=jax.ShapeDtypeStruct((batch * seq, BRANCH_WIDTH), BF16),
        scratch_shapes=[pltpu.VMEM((seq, BRANCH_WIDTH), BF16),
                        pltpu.VMEM((seq, BRANCH_WIDTH), BF16),
                        pltpu.VMEM((128, BRANCH_WIDTH), F32),
                        pltpu.VMEM((N_HEADS, t, BRANCH_WIDTH), F32)],
        compiler_params=_params("arbitrary", "arbitrary"),
        name="moba",
    )(u, u, u, tiles)


def _diff_kernel(lam_ref, g_ref, q_ref, k_ref, v_ref, tiles_ref, o_ref, kb_ref, vb_ref, acc_ref, *, lam_init):
    t = ATT_BLOCK
    qb = pl.program_id(1)

    @pl.when(qb == 0)
    def _():
        kb_ref[...] = k_ref[...].astype(BF16)
        vb_ref[...] = v_ref[...].astype(BF16)

    lp = lam_ref[...]
    lam = (jnp.exp(jnp.sum(lp[0:1] * lp[1:2], axis=1, keepdims=True))
           - jnp.exp(jnp.sum(lp[2:3] * lp[3:4], axis=1, keepdims=True)) + lam_init)

    lane = lax.broadcasted_iota(jnp.int32, (t, BRANCH_WIDTH), 1)
    causal = (lax.broadcasted_iota(jnp.int32, (t, t), 0) >= lax.broadcasted_iota(jnp.int32, (t, t), 1))
    qf = q_ref[...] * (DIFF_QK_DIM ** -0.5)
    chains = range(2 * N_HEADS)
    q16 = []
    for h in range(N_HEADS):
        for c in range(2):
            lo = h * HEAD_WIDTH + c * DIFF_QK_DIM
            q16.append(jnp.where((lane >= lo) & (lane < lo + DIFF_QK_DIM), qf, 0.0).astype(BF16))

    def kv(j):
        j0 = pl.multiple_of(j * t, t)
        return kb_ref[pl.ds(j0, t), :], vb_ref[pl.ds(j0, t), :]

    k_own, v_own = kv(qb)
    state = [_attend(q16[i], k_own, v_own, acc_ref, i, None, bias=tiles_ref[i // 2, 0], keep=causal)
             for i in chains]
    has_prev = qb >= 1
    k_prev, v_prev = kv(jnp.maximum(qb - 1, 0))
    state = [_attend(q16[i], k_prev, v_prev, acc_ref, i, state[i], bias=tiles_ref[i // 2, 1], keep=has_prev)
             for i in chains]

    def body(j, carry):
        kj, vj = kv(j)
        new = [_attend(q16[i], kj, vj, acc_ref, i, (carry[2 * i], carry[2 * i + 1])) for i in chains]
        return tuple(x for st in new for x in st)

    flat = lax.fori_loop(0, qb - 1, body, tuple(x for st in state for x in st))
    out = jnp.zeros((t, BRANCH_WIDTH), F32)
    for h in range(N_HEADS):
        in_head = _head_lanes(lane, h)
        o = acc_ref[2 * h] / flat[4 * h + 1] - lam * (acc_ref[2 * h + 1] / flat[4 * h + 3])
        ms_h = jnp.sum(jnp.where(in_head, o * o, 0.0), axis=1, keepdims=True) * (1.0 / HEAD_WIDTH)
        out = jnp.where(in_head, o * lax.rsqrt(ms_h + SUBLN_EPS), out)
    o_ref[...] = (out * g_ref[...] * (1.0 - lam_init)).astype(o_ref.dtype)


def _diff(u, tiles, lam_params, subln_g4, batch, seq, lam_init):
    t = ATT_BLOCK
    nq = seq // t
    return pl.pallas_call(
        functools.partial(_diff_kernel, lam_init=lam_init),
        grid=(batch, nq),
        in_specs=[pl.BlockSpec((4, DIFF_QK_DIM), lambda b, q: (0, 0)),
                  pl.BlockSpec((1, BRANCH_WIDTH), lambda b, q: (0, 0)),
                  pl.BlockSpec((t, BRANCH_WIDTH), lambda b, q: (b * nq + q, 7)),
                  pl.BlockSpec((seq, BRANCH_WIDTH), lambda b, q: (b, 8)),
                  pl.BlockSpec((seq, BRANCH_WIDTH), lambda b, q: (b, 9)),
                  pl.BlockSpec((N_HEADS, 2, t, t), lambda b, q: (1, 0, 0, 0))],
        out_specs=pl.BlockSpec((t, BRANCH_WIDTH), lambda b, q: (b * nq + q, 0)),
        out_shape=jax.ShapeDtypeStruct((batch * seq, BRANCH_WIDTH), BF16),
        scratch_shapes=[pltpu.VMEM((seq, BRANCH_WIDTH), BF16),
                        pltpu.VMEM((seq, BRANCH_WIDTH), BF16),
                        pltpu.VMEM((2 * N_HEADS, t, BRANCH_WIDTH), F32)],
        compiler_params=_params("arbitrary", "arbitrary"),
        name="diff_attn",
    )(lam_params, subln_g4, u, u, u, tiles)


def _convpool_kernel(xb_ref, bb_ref, cb_ref, pc_ref, xh_ref, ch_ref, ph_ref, cw_ref, pw_ref, ps_ref,
                     o_ref, u_ref, s_ref, *, seq):
    tm = xb_ref.shape[0]
    pos0 = (pl.program_id(0) * tm) % seq
    has_history = pos0 > 0
    pos = pos0 + lax.broadcasted_iota(jnp.int32, (tm, 1), 0)
    lane = lax.broadcasted_iota(jnp.int32, (tm, BRANCH_WIDTH), 1)

    u_ref[0:HALO, :] = jnp.where(has_history, ch_ref[...] * xh_ref[...], 0.0)
    u_ref[HALO:, :] = cb_ref[...] * xb_ref[...]
    conv = cw_ref[CONV_WIDTH - 1:CONV_WIDTH, :] * u_ref[HALO:, :]
    for i in range(CONV_WIDTH - 1):
        shift = CONV_WIDTH - 1 - i
        conv = conv + cw_ref[i:i + 1, :] * u_ref[HALO - shift:HALO - shift + tm, :]
    o_ref[:, 0:BRANCH_WIDTH] = (bb_ref[...] * conv).astype(o_ref.dtype)

    s_ref[0:HALO, :] = jnp.where(has_history, ph_ref[...], 0.0)
    s_ref[HALO:, :] = pc_ref[...]
    pooled = jnp.zeros((tm, BRANCH_WIDTH), F32)
    done = 0
    for g, w in enumerate(POOL_WINDOWS):
        half = w // 2
        cur = s_ref[done + half:, :] + s_ref[done:HALO + tm - half, :]
        done += half
        s_ref[done:, :] = cur
        cnt = jnp.minimum(pos + 1, w).astype(F32)
        mean_w = s_ref[HALO:, :] / cnt
        pooled = jnp.where((lane >= g * POOL_GROUP) & (lane < (g + 1) * POOL_GROUP), mean_w, pooled)
    pooled = pooled - pc_ref[...]
    y_c = jnp.dot(pooled.astype(BF16), pw_ref[...], preferred_element_type=F32) * ps_ref[...]
    o_ref[:, BRANCH_WIDTH:] = y_c.astype(o_ref.dtype)


def _convpool(u, conv_w, pool_w_bd, pool_scale, seq, tm):
    n = u.shape[0]
    rows = lambda c: pl.BlockSpec((tm, BRANCH_WIDTH), lambda i, c=c: (i, c))
    halo = lambda c: pl.BlockSpec((HALO, BRANCH_WIDTH),
                                  lambda i, c=c: (jnp.maximum(i * (tm // HALO) - 1, 0), c))
    full = lambda a: pl.BlockSpec(a.shape, lambda i: (0,) * a.ndim)
    return pl.pallas_call(
        functools.partial(_convpool_kernel, seq=seq),
        grid=(n // tm,),
        in_specs=[rows(3), rows(4), rows(5), rows(6), halo(3), halo(5), halo(6),
                  full(conv_w), full(pool_w_bd), full(pool_scale)],
        out_specs=pl.BlockSpec((tm, 2 * BRANCH_WIDTH), lambda i: (i, 0)),
        out_shape=jax.ShapeDtypeStruct((n, 2 * BRANCH_WIDTH), BF16),
        scratch_shapes=[pltpu.VMEM((HALO + tm, BRANCH_WIDTH), F32),
                        pltpu.VMEM((HALO + tm, BRANCH_WIDTH), F32)],
        compiler_params=_params("parallel"),
        name="convpool",
    )(u, u, u, u, u, u, u, conv_w, pool_w_bd, pool_scale)


def _merge_kernel(x_ref, g_ref, wg_ref, ya_ref, ybc_ref, yd_ref, bp_ref, wo_ref, o_ref):
    d = x_ref.shape[1]
    x = x_ref[...]
    h = _rms(x, g_ref[...], RMS_EPS).astype(BF16)
    ys = (ya_ref[...], ybc_ref[:, 0:BRANCH_WIDTH], ybc_ref[:, BRANCH_WIDTH:], yd_ref[...])
    merged = jnp.zeros(x.shape, F32)
    for b, y in enumerate(ys):
        gate = jax.nn.sigmoid(jnp.dot(h, wg_ref[:, b * d:(b + 1) * d], preferred_element_type=F32))
        merged = merged + gate * jnp.dot(y, bp_ref[b * BRANCH_WIDTH:(b + 1) * BRANCH_WIDTH, :],
                                         preferred_element_type=F32)
    o_ref[...] = x + jnp.dot(merged.astype(BF16), wo_ref[...], preferred_element_type=F32)


def _merge(x2, g, w_gate, y_a, y_bc, y_d, bp, w_out, tm):
    n, d = x2.shape
    full = lambda a: pl.BlockSpec(a.shape, lambda i: (0,) * a.ndim)
    rows = lambda a: pl.BlockSpec((tm, a.shape[1]), lambda i: (i, 0))
    return pl.pallas_call(
        _merge_kernel,
        grid=(n // tm,),
        in_specs=[rows(x2), full(g), full(w_gate), rows(y_a), rows(y_bc), rows(y_d), full(bp), full(w_out)],
        out_specs=pl.BlockSpec((tm, d), lambda i: (i, 0)),
        out_shape=jax.ShapeDtypeStruct((n, d), F32),
        compiler_params=_params("parallel"),
        name="merge",
    )(x2, g, w_gate, y_a, y_bc, y_d, bp, w_out)


ROUTE_COLS = 8


def _pack_bf16_pairs(h):
    c = h.shape[1] // 2
    bits = lax.bitcast_convert_type(h.astype(BF16).astype(F32), jnp.uint32)
    return (bits[:, :c] >> 16) | (bits[:, c:] & jnp.uint32(0xFFFF0000))


def _unpack_bf16_pairs(w):
    lo = lax.bitcast_convert_type(w << 16, F32).astype(BF16)
    hi = lax.bitcast_convert_type(w & jnp.uint32(0xFFFF0000), F32).astype(BF16)
    return jnp.concatenate([lo, hi], axis=1)


def _router_kernel(x_ref, g_ref, r_ref, tri_ref, idx_ref, wts_ref, hpk_ref, cnt_ref, seen_ref, *, n_exp):
    @pl.when(pl.program_id(0) == 0)
    def _():
        seen_ref[...] = jnp.zeros(seen_ref.shape, F32)

    h = _rms(x_ref[...], g_ref[...], RMS_EPS)
    logits = jnp.dot(h, r_ref[...], precision=lax.Precision.HIGHEST, preferred_element_type=F32)
    lane = lax.broadcasted_iota(jnp.int32, logits.shape, 1).astype(F32)
    logits = jnp.where(lane < n_exp, logits, -jnp.inf)
    big = float(logits.shape[1])
    m1 = jnp.max(logits, axis=1, keepdims=True)
    i1 = jnp.min(jnp.where(logits == m1, lane, big), axis=1, keepdims=True)
    rest = jnp.where(lane == i1, -jnp.inf, logits)
    m2 = jnp.max(rest, axis=1, keepdims=True)
    i2 = jnp.min(jnp.where(rest == m2, lane, big), axis=1, keepdims=True)
    e2 = jnp.exp(m2 - m1)
    w1 = 1.0 / (1.0 + e2)
    w2 = e2 / (1.0 + e2)

    chosen = (lane == i1) | (lane == i2)
    before = seen_ref[...] + jnp.dot(tri_ref[...], chosen.astype(BF16), preferred_element_type=F32)
    r1 = jnp.sum(jnp.where(lane == i1, before, 0.0), axis=1, keepdims=True)
    r2 = jnp.sum(jnp.where(lane == i2, before, 0.0), axis=1, keepdims=True)
    seen_ref[...] += jnp.sum(chosen.astype(F32), axis=0, keepdims=True)
    cnt_ref[...] = seen_ref[...]

    col = lax.broadcasted_iota(jnp.int32, idx_ref.shape, 1)
    pick = lambda a, b, c, d: jnp.where(col == 0, a, jnp.where(col == 1, b, jnp.where(col == 2, c, d)))
    idx_ref[...] = pick(i1, i2, r1, jnp.where(col == 3, r2, 0.0)).astype(jnp.int32)
    wts_ref[...] = pick(w1, w2, 0.0, 0.0)
    hpk_ref[...] = _pack_bf16_pairs(h)


def _router(x2, g, router_pad, n_exp, tm):
    n, d = x2.shape
    tri = jnp.asarray(np.tril(np.ones((tm, tm), np.float32), -1), BF16)
    full = lambda a: pl.BlockSpec(a.shape, lambda i: (0,) * a.ndim)
    rows = lambda w: pl.BlockSpec((tm, w), lambda i: (i, 0))
    return pl.pallas_call(
        functools.partial(_router_kernel, n_exp=n_exp),
        grid=(n // tm,),
        in_specs=[rows(d), full(g), full(router_pad), full(tri)],
        out_specs=[rows(ROUTE_COLS), rows(ROUTE_COLS), rows(d // 2),
                   pl.BlockSpec((1, router_pad.shape[1]), lambda i: (0, 0))],
        out_shape=[jax.ShapeDtypeStruct((n, ROUTE_COLS), jnp.int32),
                   jax.ShapeDtypeStruct((n, ROUTE_COLS), F32),
                   jax.ShapeDtypeStruct((n, d // 2), jnp.uint32),
                   jax.ShapeDtypeStruct((1, router_pad.shape[1]), F32)],
        scratch_shapes=[pltpu.VMEM((1, router_pad.shape[1]), F32)],
        compiler_params=_params("arbitrary"),
        name="router",
    )(x2, g, router_pad, tri)


V7X_SC_CORES = 2
V7X_SC_SUBCORES = 16
SC_GATHER_BYTES = 256 * 1024
SC_MAX_INDEX_VECTOR = 128


def _sc_gather(table, idx):
    n_rows, width = idx.shape[0], table.shape[1]
    workers = V7X_SC_CORES * V7X_SC_SUBCORES
    chunk = min(SC_MAX_INDEX_VECTOR, SC_GATHER_BYTES // (width * table.dtype.itemsize))
    assert table.dtype.itemsize == 4 and n_rows % (workers * chunk) == 0 and chunk % 8 == 0
    per_worker = n_rows // workers
    mesh = plsc.VectorSubcoreMesh(core_axis_name="c", subcore_axis_name="s",
                                  num_cores=V7X_SC_CORES, num_subcores=V7X_SC_SUBCORES)

    def body(table_hbm, idx_hbm, out_hbm, idx_v, rows_v, sem):
        base = (lax.axis_index("s") * V7X_SC_CORES + lax.axis_index("c")) * per_worker

        @pl.loop(0, per_worker // chunk)
        def _(i):
            off = base + i * chunk
            pltpu.sync_copy(idx_hbm.at[pl.ds(off, chunk)], idx_v)
            pltpu.async_copy(table_hbm.at[idx_v], rows_v, sem).wait()
            pltpu.sync_copy(rows_v, out_hbm.at[pl.ds(off, chunk)])

    return pl.kernel(
        body,
        out_type=jax.ShapeDtypeStruct((n_rows, width), table.dtype),
        mesh=mesh,
        scratch_types=[pltpu.VMEM((chunk,), jnp.int32), pltpu.VMEM((chunk, width), table.dtype),
                       pltpu.SemaphoreType.DMA],
        name="sc_gather",
    )(table, idx)


def _swiglu_chunk(h, w1, w3, w2):
    a = jnp.dot(h, w1, preferred_element_type=F32)
    b = jnp.dot(h, w3, preferred_element_type=F32)
    t = (a * jax.nn.sigmoid(a) * b).astype(BF16)
    return jnp.dot(t, w2, preferred_element_type=F32)


def _ffn_kernel(x_ref, g_ref, w1_ref, w3_ref, w2_ref, fg_ref, o_ref, h_ref, acc_ref, *, final_norm):
    c = pl.program_id(1)

    @pl.when(c == 0)
    def _():
        h_ref[...] = _rms(x_ref[...], g_ref[...], RMS_EPS).astype(BF16)
        acc_ref[...] = jnp.zeros(acc_ref.shape, F32)

    acc_ref[...] += _swiglu_chunk(h_ref[...], w1_ref[...], w3_ref[...], w2_ref[...])

    @pl.when(c == pl.num_programs(1) - 1)
    def _():
        out = x_ref[...] + acc_ref[...]
        if final_norm:
            out = _rms(out, fg_ref[...], RMS_EPS)
        o_ref[...] = out


def _ffn(x2, g, w1, w3, w2, final_g, tm, fc, final_norm):
    n, d = x2.shape
    d_ff = w1.shape[1]
    return pl.pallas_call(
        functools.partial(_ffn_kernel, final_norm=final_norm),
        grid=(n // tm, d_ff // fc),
        in_specs=[pl.BlockSpec((tm, d), lambda i, c: (i, 0)),
                  pl.BlockSpec((1, d), lambda i, c: (0, 0)),
                  pl.BlockSpec((d, fc), lambda i, c: (0, c)),
                  pl.BlockSpec((d, fc), lambda i, c: (0, c)),
                  pl.BlockSpec((fc, d), lambda i, c: (c, 0)),
                  pl.BlockSpec((1, d), lambda i, c: (0, 0))],
        out_specs=pl.BlockSpec((tm, d), lambda i, c: (i, 0)),
        out_shape=jax.ShapeDtypeStruct((n, d), F32),
        scratch_shapes=[pltpu.VMEM((tm, d), BF16), pltpu.VMEM((tm, d), F32)],
        compiler_params=_params("parallel", "arbitrary"),
        name="dense_ffn",
    )(x2, g, w1, w3, w2, final_g)


def _expert_ffn_kernel(tile_ref, exp_ref, lo_ref, hi_ref, xs_ref, w1_ref, w3_ref, w2_ref, o_ref, h_ref, acc_ref):
    it = pl.program_id(0)
    c = pl.program_id(1)
    rows = o_ref.shape[0]
    lo, hi = lo_ref[it], hi_ref[it]

    @pl.when(hi > lo)
    def _():
        @pl.when(c == 0)
        def _():
            h_ref[...] = _unpack_bf16_pairs(xs_ref[...])

        y = _swiglu_chunk(h_ref[...], w1_ref[0], w3_ref[0], w2_ref[0])

        @pl.when(c == 0)
        def _():
            acc_ref[...] = y

        @pl.when(c > 0)
        def _():
            acc_ref[...] += y

        @pl.when(c == pl.num_programs(1) - 1)
        def _():
            first_row = lo - tile_ref[it] * rows

            @pl.when(first_row == 0)
            def _():
                o_ref[...] = acc_ref[...]

            @pl.when(first_row > 0)
            def _():
                row = lax.broadcasted_iota(jnp.int32, o_ref.shape, 0)
                o_ref[...] = jnp.where(row >= first_row, acc_ref[...], o_ref[...])


def _expert_ffn(xs, items, w1, w3, w2, rows, fc):
    n_pairs, half = xs.shape
    d = 2 * half
    d_ff = w1.shape[2]
    item_tile, item_expert, item_lo, item_hi = items
    grid_spec = pltpu.PrefetchScalarGridSpec(
        num_scalar_prefetch=4,
        grid=(item_tile.shape[0], d_ff // fc),
        in_specs=[pl.BlockSpec((rows, half), lambda i, c, t, e, lo, hi: (t[i], 0)),
                  pl.BlockSpec((1, d, fc), lambda i, c, t, e, lo, hi: (e[i], 0, c)),
                  pl.BlockSpec((1, d, fc), lambda i, c, t, e, lo, hi: (e[i], 0, c)),
                  pl.BlockSpec((1, fc, d), lambda i, c, t, e, lo, hi: (e[i], c, 0))],
        out_specs=pl.BlockSpec((rows, d), lambda i, c, t, e, lo, hi: (t[i], 0)),
        scratch_shapes=[pltpu.VMEM((rows, d), BF16), pltpu.VMEM((rows, d), F32)],
    )
    return pl.pallas_call(
        _expert_ffn_kernel,
        grid_spec=grid_spec,
        out_shape=jax.ShapeDtypeStruct((n_pairs, d), F32),
        compiler_params=_params("arbitrary", "arbitrary"),
        name="expert_ffn",
    )(item_tile, item_expert, item_lo, item_hi, xs, w1, w3, w2)


def _work_items(counts, n_pairs, rows):
    n_exp = counts.shape[0]
    n_tiles = n_pairs // rows
    ends = jnp.cumsum(counts)
    starts = ends - counts
    lo = jnp.sort(jnp.concatenate([jnp.arange(n_tiles, dtype=jnp.int32) * rows, starts[1:]]))
    hi = jnp.concatenate([lo[1:], jnp.full((1,), n_pairs, jnp.int32)])
    tile = jnp.minimum(lo // rows, n_tiles - 1)
    expert = jnp.minimum(jnp.searchsorted(ends, lo, side="right").astype(jnp.int32), n_exp - 1)
    return (tile, expert, lo, hi), starts


def _combine_kernel(x_ref, y1_ref, y2_ref, w_ref, fg_ref, o_ref, *, final_norm):
    w = w_ref[...]
    out = x_ref[...] + (w[:, 0:1] * y1_ref[...] + w[:, 1:2] * y2_ref[...])
    if final_norm:
        out = _rms(out, fg_ref[...], RMS_EPS)
    o_ref[...] = out


def _combine(x2, y_pairs, wts, final_g, tm, final_norm):
    n, d = x2.shape
    nt = n // tm
    return pl.pallas_call(
        functools.partial(_combine_kernel, final_norm=final_norm),
        grid=(nt,),
        in_specs=[pl.BlockSpec((tm, d), lambda i: (i, 0)),
                  pl.BlockSpec((tm, d), lambda i: (i, 0)),
                  pl.BlockSpec((tm, d), lambda i: (i + nt, 0)),
                  pl.BlockSpec((tm, wts.shape[1]), lambda i: (i, 0)),
                  pl.BlockSpec((1, d), lambda i: (0, 0))],
        out_specs=pl.BlockSpec((tm, d), lambda i: (i, 0)),
        out_shape=jax.ShapeDtypeStruct((n, d), F32),
        compiler_params=_params("parallel"),
        name="moe_combine",
    )(x2, y_pairs, y_pairs, wts, final_g)


def _moe(x2, g_ffn, router, w1, w3, w2, final_g, tm, final_norm):
    n, d = x2.shape
    n_exp = router.shape[1]
    router_pad = jnp.pad(router, ((0, 0), (0, 128 - n_exp)))
    idx, wts, h_packed, seen = _router(x2, g_ffn, router_pad, n_exp, tm)

    rows = _pick_tile(2 * n, 1024)
    counts = seen[0, :n_exp].astype(jnp.int32)
    items, starts = _work_items(counts, 2 * n, rows)
    pos = jnp.concatenate([starts[idx[:, 0]] + idx[:, 2], starts[idx[:, 1]] + idx[:, 3]])
    token = jnp.tile(jnp.arange(n, dtype=jnp.int32), 2)
    src = jnp.zeros((2 * n,), jnp.int32).at[pos].set(token, unique_indices=True, mode="promise_in_bounds")

    xs = _sc_gather(h_packed, src)
    ys = _expert_ffn(xs, items, w1, w3, w2, rows, _pick_tile(w1.shape[2], 512))
    y_pairs = _sc_gather(ys, pos)
    return _combine(x2, y_pairs, wts, final_g, tm, final_norm)


def _block_diag(w):
    g, c, _ = w.shape
    eye = jnp.eye(g, dtype=w.dtype)
    return (eye[:, None, :, None] * w[:, :, None, :]).reshape(g * c, g * c)


def _pick_tile(n, target):
    t = min(n, target)
    while n % t:
        t //= 2
    return t


def kernel(x, bias_table, mix_norm_g, w_in, conv_w, pool_w, pool_scale, diff_lambda, diff_subln_g,
           branch_proj, w_out, ffn_norm_g, dense_w1, dense_w3, dense_w2, moe_router, moe_w1, moe_w3,
           moe_w2, final_norm_g):
    batch, seq, d = x.shape
    depth = w_in.shape[0]
    n_mix = N_MIX_SLICES * BRANCH_WIDTH
    n = batch * seq
    assert seq % ATT_BLOCK == 0 and d % 128 == 0
    tm = _pick_tile(seq, 512)

    x2 = x.reshape(n, d)
    tiles = _bias_tiles(bias_table)
    row = lambda v: v.reshape(1, -1)
    final_g = row(final_norm_g)

    for i in range(depth):
        last_layer = i == depth - 1
        g_mix = row(mix_norm_g[i])
        w_mix = w_in[i, :, :n_mix].astype(BF16)
        w_gate = w_in[i, :, n_mix:].astype(BF16)
        u = _inproj(x2, g_mix, w_mix, tm)

        lam_init = 0.8 - 0.6 * math.exp(-0.3 * i)
        y_a = _moba(u, tiles, batch, seq)
        y_d = _diff(u, tiles, diff_lambda[i], row(jnp.tile(diff_subln_g[i], N_HEADS)), batch, seq, lam_init)
        y_bc = _convpool(u, conv_w[i], _block_diag(pool_w[i]).astype(BF16), row(pool_scale[i]), seq, tm)
        x2 = _merge(x2, g_mix, w_gate, y_a, y_bc, y_d,
                    branch_proj[i].reshape(-1, d).astype(BF16), w_out[i].astype(BF16), tm)

        g_ffn = row(ffn_norm_g[i])
        j = i // 2
        if i % 2 == 0:
            d_ff = dense_w1.shape[2]
            fc = d_ff // 2 if (d_ff // 2) % 128 == 0 else d_ff
            x2 = _ffn(x2, g_ffn, dense_w1[j].astype(BF16), dense_w3[j].astype(BF16),
                      dense_w2[j].astype(BF16), final_g, tm, fc, final_norm=last_layer)
        else:
            x2 = _moe(x2, g_ffn, moe_router[j], moe_w1[j].astype(BF16), moe_w3[j].astype(BF16),
                      moe_w2[j].astype(BF16), final_g, tm, final_norm=last_layer)

    return x2.reshape(batch, seq, d)
```

```python
import functools
import math

import numpy as np
import jax
import jax.numpy as jnp
from jax import lax
from jax.experimental import pallas as pl
from jax.experimental.pallas import tpu as pltpu
from jax.experimental.pallas import tpu_sc as plsc

F32 = jnp.float32
BF16 = jnp.bfloat16

BRANCH_WIDTH = 256
N_MIX_SLICES = 10
HEAD_WIDTH = 64
N_HEADS = 4
DIFF_QK_DIM = 32
ATT_BLOCK = 256
MOBA_TOPK = 3
CONV_WIDTH = 3
POOL_WINDOWS = (2, 4, 8, 16)
POOL_GROUP = 64
HALO = 16
REL_BUCKETS = 32
REL_MAX_DIST = 128
TOP_K_EXPERTS = 2
RMS_EPS = 1e-6
SUBLN_EPS = 1e-5
NEG_INF = -1e30
V7X_VMEM_BYTES = 64 * 1024 * 1024
VMEM_LIMIT = V7X_VMEM_BYTES - 8 * 1024 * 1024

_TRANS_B = (((1,), (1,)), ((), ()))


def _params(*sem):
    return pltpu.CompilerParams(dimension_semantics=sem, vmem_limit_bytes=VMEM_LIMIT)


def _rms(x, g, eps):
    r = lax.rsqrt(jnp.mean(x * x, axis=-1, keepdims=True) + eps)
    return x * r * g


def _inproj_kernel(x_ref, g_ref, w_ref, o_ref):
    h = _rms(x_ref[...], g_ref[...], RMS_EPS).astype(BF16)
    o_ref[...] = jnp.dot(h, w_ref[...], preferred_element_type=F32)


def _inproj(x2, g, w, tm):
    n, d = x2.shape
    wn = w.shape[1]
    return pl.pallas_call(
        _inproj_kernel,
        grid=(n // tm,),
        in_specs=[pl.BlockSpec((tm, d), lambda i: (i, 0)),
                  pl.BlockSpec((1, d), lambda i: (0, 0)),
                  pl.BlockSpec((d, wn), lambda i: (0, 0))],
        out_specs=pl.BlockSpec((tm, wn), lambda i: (i, 0)),
        out_shape=jax.ShapeDtypeStruct((n, wn), F32),
        compiler_params=_params("parallel"),
        name="inproj",
    )(x2, g, w)


def _rel_bucket_np(dist):
    n = np.maximum(dist, 0)
    max_exact = REL_BUCKETS // 2
    nf = np.maximum(n, max_exact).astype(np.float32)
    large = max_exact + (np.log(nf / np.float32(max_exact)) / np.float32(math.log(REL_MAX_DIST / max_exact))
                         * np.float32(REL_BUCKETS - max_exact)).astype(np.int32)
    large = np.minimum(large, REL_BUCKETS - 1)
    return np.where(n < max_exact, n, large).astype(np.int32)


def _bucket_tiles():
    i = np.arange(ATT_BLOCK)[:, None]
    j = np.arange(ATT_BLOCK)[None, :]
    return np.stack([_rel_bucket_np(i - j), _rel_bucket_np(ATT_BLOCK + i - j)])


def _bias_tiles_kernel(tab_ref, bkt_ref, o_ref):
    h = pl.program_id(0)
    bkt = bkt_ref[...]
    acc = jnp.zeros(bkt.shape, F32)
    for b in range(REL_BUCKETS):
        acc = jnp.where(bkt == b, tab_ref[b, h], acc)
    o_ref[0] = acc - tab_ref[REL_BUCKETS - 1, h]


def _bias_tiles(bias_table):
    n_heads = bias_table.shape[1]
    bkt = jnp.asarray(_bucket_tiles())
    return pl.pallas_call(
        _bias_tiles_kernel,
        grid=(n_heads,),
        in_specs=[pl.BlockSpec(memory_space=pltpu.SMEM),
                  pl.BlockSpec((2, ATT_BLOCK, ATT_BLOCK), lambda h: (0, 0, 0))],
        out_specs=pl.BlockSpec((1, 2, ATT_BLOCK, ATT_BLOCK), lambda h: (h, 0, 0, 0)),
        out_shape=jax.ShapeDtypeStruct((n_heads, 2, ATT_BLOCK, ATT_BLOCK), F32),
        compiler_params=_params("arbitrary"),
        name="bias_tiles",
    )(bias_table, bkt)


LANE_TILE = 128


def _head_lanes(lane, h):
    return (lane >= h * HEAD_WIDTH) & (lane < (h + 1) * HEAD_WIDTH)


def _fold(op, s):
    out = s[:, :LANE_TILE]
    for c in range(LANE_TILE, s.shape[1], LANE_TILE):
        out = op(out, s[:, c:c + LANE_TILE])
    return out


def _stage_keys_values(k_ref, v_ref, kb_ref, vm_ref):
    t = ATT_BLOCK
    kb_ref[:, 0:BRANCH_WIDTH] = k_ref[...].astype(BF16)
    lane = lax.broadcasted_iota(jnp.int32, (t, BRANCH_WIDTH), 1)
    for j in range(vm_ref.shape[0]):
        vj = v_ref[j * t:(j + 1) * t, :]
        for h in range(N_HEADS):
            vm_ref[j, h * t:(h + 1) * t, :] = jnp.where(_head_lanes(lane, h), vj, 0.0).astype(BF16)


def _softmax_attend(qb, q16, head_of, groups, kb_ref, vm_ref, tiles_ref, s_ref, m_ref, l_ref, acc_ref):
    t = ATT_BLOCK
    chains = range(len(q16))
    causal = (lax.broadcasted_iota(jnp.int32, (t, t), 0) >= lax.broadcasted_iota(jnp.int32, (t, t), 1))

    def keys(j):
        return kb_ref[pl.ds(pl.multiple_of(j * t, t), t), :]

    def score(i, kj):
        return lax.dot_general(q16[i], kj, _TRANS_B, preferred_element_type=F32)

    def put(i, j, s, first):
        s_ref[i, j] = s
        m_ref[i] = _fold(jnp.maximum, s) if first else jnp.maximum(m_ref[i], _fold(jnp.maximum, s))

    k_own = keys(qb)
    for i in chains:
        put(i, qb, jnp.where(causal, score(i, k_own) + tiles_ref[head_of(i), 0], NEG_INF), True)

    @pl.when(qb >= 1)
    def _():
        k_prev = keys(qb - 1)
        for i in chains:
            put(i, qb - 1, score(i, k_prev) + tiles_ref[head_of(i), 1], False)

    def far(j, carry):
        kj = keys(j)
        for i in chains:
            put(i, j, score(i, kj), False)
        return carry

    lax.fori_loop(0, qb - 1, far, 0)

    for i in chains:
        m_ref[i] = jnp.broadcast_to(jnp.max(m_ref[i], axis=1, keepdims=True), (t, LANE_TILE))

    def accumulate(j, first):
        vj = vm_ref[j]
        for g, members in enumerate(groups):
            parts = []
            for i in members:
                s = s_ref[i, j]
                p = [jnp.exp(s[:, c:c + LANE_TILE] - m_ref[i]) for c in range(0, t, LANE_TILE)]
                row_sum = functools.reduce(jnp.add, p)
                l_ref[i] = row_sum if first else l_ref[i] + row_sum
                parts += [x.astype(BF16) for x in p]
            pv = jnp.dot(jnp.concatenate(parts, axis=1), vj, preferred_element_type=F32)
            acc_ref[g] = pv if first else acc_ref[g] + pv

    accumulate(qb, True)

    def rest(j, carry):
        accumulate(j, False)
        return carry

    lax.fori_loop(0, qb, rest, 0)


def _per_head_lanes(lane, cols):
    out = jnp.broadcast_to(cols[0], lane.shape)
    for h in range(1, N_HEADS):
        out = jnp.where(_head_lanes(lane, h), cols[h], out)
    return out


def _attention_scratch(seq, n_chains, n_groups, key_width=BRANCH_WIDTH):
    t = ATT_BLOCK
    n_blk = seq // t
    return [pltpu.VMEM((seq, key_width), BF16),
            pltpu.VMEM((n_blk, N_HEADS * t, BRANCH_WIDTH), BF16),
            pltpu.VMEM((n_chains, n_blk, t, t), F32),
            pltpu.VMEM((n_chains, t, LANE_TILE), F32),
            pltpu.VMEM((n_chains, t, LANE_TILE), F32),
            pltpu.VMEM((n_groups, t, BRANCH_WIDTH), F32)]


def _moba_kernel(q_ref, k_ref, v_ref, tiles_ref, o_ref, kb_ref, vm_ref, s_ref, m_ref, l_ref, acc_ref, kmean_ref):
    t = ATT_BLOCK
    n_blk = k_ref.shape[0] // t
    qb = pl.program_id(1)

    heads = range(N_HEADS)
    assert n_blk <= LANE_TILE

    @pl.when(qb == 0)
    def _():
        _stage_keys_values(k_ref, v_ref, kb_ref, vm_ref)
        onehot_lane = lax.broadcasted_iota(jnp.int32, (t, LANE_TILE), 1)
        ch = lax.broadcasted_iota(jnp.int32, (1, BRANCH_WIDTH), 1)
        kmean_ref[...] = jnp.zeros(kmean_ref.shape, F32)
        for j in range(n_blk):
            kb_ref[j * t:(j + 1) * t, BRANCH_WIDTH:] = (onehot_lane == j).astype(BF16)
            mean_j = jnp.mean(k_ref[j * t:(j + 1) * t, :], axis=0, keepdims=True)
            for h in heads:
                kmean_ref[h * LANE_TILE + j:h * LANE_TILE + j + 1, :] = jnp.where(_head_lanes(ch, h), mean_j, 0.0)

    lane = lax.broadcasted_iota(jnp.int32, (t, BRANCH_WIDTH), 1)
    glane = lax.broadcasted_iota(jnp.int32, (t, LANE_TILE), 1).astype(F32)
    past = glane < qb.astype(F32)
    qf = q_ref[...] * (HEAD_WIDTH ** -0.5)
    gate = lax.dot_general(qf, kmean_ref[...], _TRANS_B, precision=lax.Precision.HIGHEST,
                           preferred_element_type=F32)
    q_aug = []
    for h in heads:
        g = gate[:, h * LANE_TILE:(h + 1) * LANE_TILE]
        avail = past
        for _ in range(MOBA_TOPK):
            best = jnp.max(jnp.where(avail, g, -jnp.inf), axis=1, keepdims=True)
            first = jnp.min(jnp.where(avail & (g == best), glane, float(LANE_TILE)), axis=1, keepdims=True)
            avail = avail & (glane != first)
        drop = jnp.where(avail, NEG_INF, 0.0).astype(BF16)
        q_aug.append(jnp.concatenate([jnp.where(_head_lanes(lane, h), qf, 0.0).astype(BF16), drop], axis=1))

    _softmax_attend(qb, q_aug, lambda i: i, [list(heads)], kb_ref, vm_ref, tiles_ref, s_ref, m_ref, l_ref, acc_ref)
    row_sums = _per_head_lanes(lane, [jnp.sum(l_ref[h], axis=1, keepdims=True) for h in heads])
    o_ref[...] = (acc_ref[0] / row_sums).astype(o_ref.dtype)


def _moba(u, tiles, batch, seq):
    t = ATT_BLOCK
    nq = seq // t
    return pl.pallas_call(
        _moba_kernel,
        grid=(batch, nq),
        in_specs=[pl.BlockSpec((t, BRANCH_WIDTH), lambda b, q: (b * nq + q, 0)),
                  pl.BlockSpec((seq, BRANCH_WIDTH), lambda b, q: (b, 1)),
                  pl.BlockSpec((seq, BRANCH_WIDTH), lambda b, q: (b, 2)),
                  pl.BlockSpec((N_HEADS, 2, t, t), lambda b, q: (0, 0, 0, 0))],
        out_specs=pl.BlockSpec((t, BRANCH_WIDTH), lambda b, q: (b * nq + q, 0)),
        out_shape=jax.ShapeDtypeStruct((batch * seq, BRANCH_WIDTH), BF16),
        scratch_shapes=(_attention_scratch(seq, N_HEADS, 1, BRANCH_WIDTH + LANE_TILE)
                        + [pltpu.VMEM((N_HEADS * LANE_TILE, BRANCH_WIDTH), F32)]),
        compiler_params=_params("arbitrary", "arbitrary"),
        name="moba",
    )(u, u, u, tiles)


def _diff_kernel(lam_ref, g_ref, q_ref, k_ref, v_ref, tiles_ref, o_ref, kb_ref, vm_ref, s_ref, m_ref, l_ref,
                 acc_ref, *, lam_init):
    t = ATT_BLOCK
    qb = pl.program_id(1)

    @pl.when(qb == 0)
    def _():
        _stage_keys_values(k_ref, v_ref, kb_ref, vm_ref)

    lp = lam_ref[...]
    lam = (jnp.exp(jnp.sum(lp[0:1] * lp[1:2], axis=1, keepdims=True))
           - jnp.exp(jnp.sum(lp[2:3] * lp[3:4], axis=1, keepdims=True)) + lam_init)

    lane = lax.broadcasted_iota(jnp.int32, (t, BRANCH_WIDTH), 1)
    qf = q_ref[...] * (DIFF_QK_DIM ** -0.5)
    q16 = []
    for h in range(N_HEADS):
        for c in range(2):
            lo = h * HEAD_WIDTH + c * DIFF_QK_DIM
            q16.append(jnp.where((lane >= lo) & (lane < lo + DIFF_QK_DIM), qf, 0.0).astype(BF16))
    groups = [[2 * h + c for h in range(N_HEADS)] for c in range(2)]
    _softmax_attend(qb, q16, lambda i: i // 2, groups, kb_ref, vm_ref, tiles_ref, s_ref, m_ref, l_ref, acc_ref)

    row_sums = [_per_head_lanes(lane, [jnp.sum(l_ref[i], axis=1, keepdims=True) for i in members])
                for members in groups]
    o = acc_ref[0] / row_sums[0] - lam * (acc_ref[1] / row_sums[1])
    sq = o * o
    mean_sq = _per_head_lanes(lane, [jnp.sum(jnp.where(_head_lanes(lane, h), sq, 0.0), axis=1, keepdims=True)
                                     for h in range(N_HEADS)]) * (1.0 / HEAD_WIDTH)
    o_ref[...] = (o * lax.rsqrt(mean_sq + SUBLN_EPS) * g_ref[...] * (1.0 - lam_init)).astype(o_ref.dtype)


def _diff(u, tiles, lam_params, subln_g4, batch, seq, lam_init):
    t = ATT_BLOCK
    nq = seq // t
    return pl.pallas_call(
        functools.partial(_diff_kernel, lam_init=lam_init),
        grid=(batch, nq),
        in_specs=[pl.BlockSpec((4, DIFF_QK_DIM), lambda b, q: (0, 0)),
                  pl.BlockSpec((1, BRANCH_WIDTH), lambda b, q: (0, 0)),
                  pl.BlockSpec((t, BRANCH_WIDTH), lambda b, q: (b * nq + q, 7)),
                  pl.BlockSpec((seq, BRANCH_WIDTH), lambda b, q: (b, 8)),
                  pl.BlockSpec((seq, BRANCH_WIDTH), lambda b, q: (b, 9)),
                  pl.BlockSpec((N_HEADS, 2, t, t), lambda b, q: (1, 0, 0, 0))],
        out_specs=pl.BlockSpec((t, BRANCH_WIDTH), lambda b, q: (b * nq + q, 0)),
        out_shape=jax.ShapeDtypeStruct((batch * seq, BRANCH_WIDTH), BF16),
        scratch_shapes=_attention_scratch(seq, 2 * N_HEADS, 2),
        compiler_params=_params("arbitrary", "arbitrary"),
        name="diff_attn",
    )(lam_params, subln_g4, u, u, u, tiles)


def _convpool_kernel(xb_ref, bb_ref, cb_ref, pc_ref, xh_ref, ch_ref, ph_ref, cw_ref, pw_ref, ps_ref,
                     o_ref, u_ref, s_ref, *, seq):
    tm = xb_ref.shape[0]
    pos0 = (pl.program_id(0) * tm) % seq
    has_history = pos0 > 0
    pos = pos0 + lax.broadcasted_iota(jnp.int32, (tm, 1), 0)
    lane = lax.broadcasted_iota(jnp.int32, (tm, BRANCH_WIDTH), 1)

    u_ref[0:HALO, :] = jnp.where(has_history, ch_ref[...] * xh_ref[...], 0.0)
    u_ref[HALO:, :] = cb_ref[...] * xb_ref[...]
    conv = cw_ref[CONV_WIDTH - 1:CONV_WIDTH, :] * u_ref[HALO:, :]
    for i in range(CONV_WIDTH - 1):
        shift = CONV_WIDTH - 1 - i
        conv = conv + cw_ref[i:i + 1, :] * u_ref[HALO - shift:HALO - shift + tm, :]
    o_ref[:, 0:BRANCH_WIDTH] = (bb_ref[...] * conv).astype(o_ref.dtype)

    s_ref[0:HALO, :] = jnp.where(has_history, ph_ref[...], 0.0)
    s_ref[HALO:, :] = pc_ref[...]
    pooled = jnp.zeros((tm, BRANCH_WIDTH), F32)
    done = 0
    for g, w in enumerate(POOL_WINDOWS):
        half = w // 2
        cur = s_ref[done + half:, :] + s_ref[done:HALO + tm - half, :]
        done += half
        s_ref[done:, :] = cur
        cnt = jnp.minimum(pos + 1, w).astype(F32)
        mean_w = s_ref[HALO:, :] / cnt
        pooled = jnp.where((lane >= g * POOL_GROUP) & (lane < (g + 1) * POOL_GROUP), mean_w, pooled)
    pooled = pooled - pc_ref[...]
    y_c = jnp.dot(pooled.astype(BF16), pw_ref[...], preferred_element_type=F32) * ps_ref[...]
    o_ref[:, BRANCH_WIDTH:] = y_c.astype(o_ref.dtype)


def _convpool(u, conv_w, pool_w_bd, pool_scale, seq, tm):
    n = u.shape[0]
    rows = lambda c: pl.BlockSpec((tm, BRANCH_WIDTH), lambda i, c=c: (i, c))
    halo = lambda c: pl.BlockSpec((HALO, BRANCH_WIDTH),
                                  lambda i, c=c: (jnp.maximum(i * (tm // HALO) - 1, 0), c))
    full = lambda a: pl.BlockSpec(a.shape, lambda i: (0,) * a.ndim)
    return pl.pallas_call(
        functools.partial(_convpool_kernel, seq=seq),
        grid=(n // tm,),
        in_specs=[rows(3), rows(4), rows(5), rows(6), halo(3), halo(5), halo(6),
                  full(conv_w), full(pool_w_bd), full(pool_scale)],
        out_specs=pl.BlockSpec((tm, 2 * BRANCH_WIDTH), lambda i: (i, 0)),
        out_shape=jax.ShapeDtypeStruct((n, 2 * BRANCH_WIDTH), BF16),
        scratch_shapes=[pltpu.VMEM((HALO + tm, BRANCH_WIDTH), F32),
                        pltpu.VMEM((HALO + tm, BRANCH_WIDTH), F32)],
        compiler_params=_params("parallel"),
        name="convpool",
    )(u, u, u, u, u, u, u, conv_w, pool_w_bd, pool_scale)


def _merge_kernel(x_ref, g_ref, wg_ref, ya_ref, ybc_ref, yd_ref, bp_ref, wo_ref, o_ref):
    d = x_ref.shape[1]
    x = x_ref[...]
    h = _rms(x, g_ref[...], RMS_EPS).astype(BF16)
    ys = (ya_ref[...], ybc_ref[:, 0:BRANCH_WIDTH], ybc_ref[:, BRANCH_WIDTH:], yd_ref[...])
    merged = jnp.zeros(x.shape, F32)
    for b, y in enumerate(ys):
        gate = jax.nn.sigmoid(jnp.dot(h, wg_ref[:, b * d:(b + 1) * d], preferred_element_type=F32))
        merged = merged + gate * jnp.dot(y, bp_ref[b * BRANCH_WIDTH:(b + 1) * BRANCH_WIDTH, :],
                                         preferred_element_type=F32)
    o_ref[...] = x + jnp.dot(merged.astype(BF16), wo_ref[...], preferred_element_type=F32)


def _merge(x2, g, w_gate, y_a, y_bc, y_d, bp, w_out, tm):
    n, d = x2.shape
    full = lambda a: pl.BlockSpec(a.shape, lambda i: (0,) * a.ndim)
    rows = lambda a: pl.BlockSpec((tm, a.shape[1]), lambda i: (i, 0))
    return pl.pallas_call(
        _merge_kernel,
        grid=(n // tm,),
        in_specs=[rows(x2), full(g), full(w_gate), rows(y_a), rows(y_bc), rows(y_d), full(bp), full(w_out)],
        out_specs=pl.BlockSpec((tm, d), lambda i: (i, 0)),
        out_shape=jax.ShapeDtypeStruct((n, d), F32),
        compiler_params=_params("parallel"),
        name="merge",
    )(x2, g, w_gate, y_a, y_bc, y_d, bp, w_out)


ROUTE_COLS = 8


def _pack_bf16_pairs(h):
    c = h.shape[1] // 2
    bits = lax.bitcast_convert_type(h.astype(BF16).astype(F32), jnp.uint32)
    return (bits[:, :c] >> 16) | (bits[:, c:] & jnp.uint32(0xFFFF0000))


def _unpack_bf16_pairs(w):
    lo = lax.bitcast_convert_type(w << 16, F32)
    hi = lax.bitcast_convert_type(w & jnp.uint32(0xFFFF0000), F32)
    return jnp.concatenate([lo, hi], axis=1)


def _router_kernel(x_ref, g_ref, r_ref, tri_ref, idx_ref, wts_ref, hpk_ref, cnt_ref, seen_ref, *, n_exp):
    @pl.when(pl.program_id(0) == 0)
    def _():
        seen_ref[...] = jnp.zeros(seen_ref.shape, F32)

    h = _rms(x_ref[...], g_ref[...], RMS_EPS)
    logits = jnp.dot(h, r_ref[...], precision=lax.Precision.HIGHEST, preferred_element_type=F32)
    lane = lax.broadcasted_iota(jnp.int32, logits.shape, 1).astype(F32)
    logits = jnp.where(lane < n_exp, logits, -jnp.inf)
    big = float(logits.shape[1])
    m1 = jnp.max(logits, axis=1, keepdims=True)
    i1 = jnp.min(jnp.where(logits == m1, lane, big), axis=1, keepdims=True)
    rest = jnp.where(lane == i1, -jnp.inf, logits)
    m2 = jnp.max(rest, axis=1, keepdims=True)
    i2 = jnp.min(jnp.where(rest == m2, lane, big), axis=1, keepdims=True)
    e2 = jnp.exp(m2 - m1)
    w1 = 1.0 / (1.0 + e2)
    w2 = e2 / (1.0 + e2)

    chosen = (lane == i1) | (lane == i2)
    before = seen_ref[...] + jnp.dot(tri_ref[...], chosen.astype(BF16), preferred_element_type=F32)
    r1 = jnp.sum(jnp.where(lane == i1, before, 0.0), axis=1, keepdims=True)
    r2 = jnp.sum(jnp.where(lane == i2, before, 0.0), axis=1, keepdims=True)
    seen_ref[...] += jnp.sum(chosen.astype(F32), axis=0, keepdims=True)
    cnt_ref[...] = seen_ref[...]

    col = lax.broadcasted_iota(jnp.int32, idx_ref.shape, 1)
    pick = lambda a, b, c, d: jnp.where(col == 0, a, jnp.where(col == 1, b, jnp.where(col == 2, c, d)))
    idx_ref[...] = pick(i1, i2, r1, jnp.where(col == 3, r2, 0.0)).astype(jnp.int32)
    wts_ref[...] = pick(w1, w2, 0.0, 0.0)
    hpk_ref[...] = _pack_bf16_pairs(h)


def _router(x2, g, router_pad, n_exp, tm):
    n, d = x2.shape
    tri = jnp.asarray(np.tril(np.ones((tm, tm), np.float32), -1), BF16)
    full = lambda a: pl.BlockSpec(a.shape, lambda i: (0,) * a.ndim)
    rows = lambda w: pl.BlockSpec((tm, w), lambda i: (i, 0))
    return pl.pallas_call(
        functools.partial(_router_kernel, n_exp=n_exp),
        grid=(n // tm,),
        in_specs=[rows(d), full(g), full(router_pad), full(tri)],
        out_specs=[rows(ROUTE_COLS), rows(ROUTE_COLS), rows(d // 2),
                   pl.BlockSpec((1, router_pad.shape[1]), lambda i: (0, 0))],
        out_shape=[jax.ShapeDtypeStruct((n, ROUTE_COLS), jnp.int32),
                   jax.ShapeDtypeStruct((n, ROUTE_COLS), F32),
                   jax.ShapeDtypeStruct((n, d // 2), jnp.uint32),
                   jax.ShapeDtypeStruct((1, router_pad.shape[1]), F32)],
        scratch_shapes=[pltpu.VMEM((1, router_pad.shape[1]), F32)],
        compiler_params=_params("arbitrary"),
        name="router",
    )(x2, g, router_pad, tri)


V7X_SC_CORES = 2
V7X_SC_SUBCORES = 16
SC_GATHER_BYTES = 256 * 1024
SC_MAX_INDEX_VECTOR = 128


def _sc_gather(table, idx):
    n_rows, width = idx.shape[0], table.shape[1]
    workers = V7X_SC_CORES * V7X_SC_SUBCORES
    chunk = min(SC_MAX_INDEX_VECTOR, SC_GATHER_BYTES // (width * table.dtype.itemsize))
    assert table.dtype.itemsize == 4 and n_rows % (workers * chunk) == 0 and chunk % 8 == 0
    per_worker = n_rows // workers
    mesh = plsc.VectorSubcoreMesh(core_axis_name="c", subcore_axis_name="s",
                                  num_cores=V7X_SC_CORES, num_subcores=V7X_SC_SUBCORES)

    def body(table_hbm, idx_hbm, out_hbm, idx_v, rows_v, sem):
        base = (lax.axis_index("s") * V7X_SC_CORES + lax.axis_index("c")) * per_worker

        @pl.loop(0, per_worker // chunk)
        def _(i):
            off = base + i * chunk
            pltpu.sync_copy(idx_hbm.at[pl.ds(off, chunk)], idx_v)
            pltpu.async_copy(table_hbm.at[idx_v], rows_v, sem).wait()
            pltpu.sync_copy(rows_v, out_hbm.at[pl.ds(off, chunk)])

    return pl.kernel(
        body,
        out_type=jax.ShapeDtypeStruct((n_rows, width), table.dtype),
        mesh=mesh,
        scratch_types=[pltpu.VMEM((chunk,), jnp.int32), pltpu.VMEM((chunk, width), table.dtype),
                       pltpu.SemaphoreType.DMA],
        name="sc_gather",
    )(table, idx)


def _sc_scatter_pairs(table, pos):
    n, width = table.shape
    workers = V7X_SC_CORES * V7X_SC_SUBCORES
    chunk = min(SC_MAX_INDEX_VECTOR, SC_GATHER_BYTES // (width * table.dtype.itemsize))
    assert table.dtype.itemsize == 4 and n % (workers * chunk) == 0 and chunk % 8 == 0
    per_worker = n // workers
    mesh = plsc.VectorSubcoreMesh(core_axis_name="c", subcore_axis_name="s",
                                  num_cores=V7X_SC_CORES, num_subcores=V7X_SC_SUBCORES)

    def body(table_hbm, pos_hbm, out_hbm, idx_a, idx_b, rows_v, sem):
        base = (lax.axis_index("s") * V7X_SC_CORES + lax.axis_index("c")) * per_worker

        @pl.loop(0, per_worker // chunk)
        def _(i):
            off = base + i * chunk
            pltpu.sync_copy(table_hbm.at[pl.ds(off, chunk)], rows_v)
            pltpu.sync_copy(pos_hbm.at[pl.ds(off, chunk)], idx_a)
            pltpu.sync_copy(pos_hbm.at[pl.ds(n + off, chunk)], idx_b)
            pltpu.async_copy(rows_v, out_hbm.at[idx_a], sem).wait()
            pltpu.async_copy(rows_v, out_hbm.at[idx_b], sem).wait()

    return pl.kernel(
        body,
        out_type=jax.ShapeDtypeStruct((2 * n, width), table.dtype),
        mesh=mesh,
        scratch_types=[pltpu.VMEM((chunk,), jnp.int32), pltpu.VMEM((chunk,), jnp.int32),
                       pltpu.VMEM((chunk, width), table.dtype), pltpu.SemaphoreType.DMA],
        name="sc_scatter_pairs",
    )(table, pos)


def _swiglu_chunk(h, w1, w3, w2):
    a = jnp.dot(h, w1, preferred_element_type=F32)
    b = jnp.dot(h, w3, preferred_element_type=F32)
    t = (a * jax.nn.sigmoid(a) * b).astype(BF16)
    return jnp.dot(t, w2, preferred_element_type=F32)


def _ffn_kernel(x_ref, g_ref, w1_ref, w3_ref, w2_ref, fg_ref, o_ref, h_ref, acc_ref, *, final_norm):
    c = pl.program_id(1)

    @pl.when(c == 0)
    def _():
        h_ref[...] = _rms(x_ref[...], g_ref[...], RMS_EPS).astype(BF16)
        acc_ref[...] = jnp.zeros(acc_ref.shape, F32)

    acc_ref[...] += _swiglu_chunk(h_ref[...], w1_ref[...], w3_ref[...], w2_ref[...])

    @pl.when(c == pl.num_programs(1) - 1)
    def _():
        out = x_ref[...] + acc_ref[...]
        if final_norm:
            out = _rms(out, fg_ref[...], RMS_EPS)
        o_ref[...] = out


def _ffn(x2, g, w1, w3, w2, final_g, tm, fc, final_norm):
    n, d = x2.shape
    d_ff = w1.shape[1]
    return pl.pallas_call(
        functools.partial(_ffn_kernel, final_norm=final_norm),
        grid=(n // tm, d_ff // fc),
        in_specs=[pl.BlockSpec((tm, d), lambda i, c: (i, 0)),
                  pl.BlockSpec((1, d), lambda i, c: (0, 0)),
                  pl.BlockSpec((d, fc), lambda i, c: (0, c)),
                  pl.BlockSpec((d, fc), lambda i, c: (0, c)),
                  pl.BlockSpec((fc, d), lambda i, c: (c, 0)),
                  pl.BlockSpec((1, d), lambda i, c: (0, 0))],
        out_specs=pl.BlockSpec((tm, d), lambda i, c: (i, 0)),
        out_shape=jax.ShapeDtypeStruct((n, d), F32),
        scratch_shapes=[pltpu.VMEM((tm, d), BF16), pltpu.VMEM((tm, d), F32)],
        compiler_params=_params("parallel", "arbitrary"),
        name="dense_ffn",
    )(x2, g, w1, w3, w2, final_g)


def _expert_ffn_kernel(tile_ref, exp_ref, lo_ref, hi_ref, xs_ref, w1_ref, w3_ref, w2_ref, o_ref, h_ref, acc_ref):
    it = pl.program_id(0)
    c = pl.program_id(1)
    rows = o_ref.shape[0]
    lo, hi = lo_ref[it], hi_ref[it]

    @pl.when(hi > lo)
    def _():
        @pl.when(c == 0)
        def _():
            h_ref[...] = _unpack_bf16_pairs(xs_ref[...]).astype(BF16)

        y = _swiglu_chunk(h_ref[...], w1_ref[0], w3_ref[0], w2_ref[0])

        @pl.when(c == 0)
        def _():
            acc_ref[...] = y

        @pl.when(c > 0)
        def _():
            acc_ref[...] += y

        @pl.when(c == pl.num_programs(1) - 1)
        def _():
            first_row = lo - tile_ref[it] * rows
            packed = _pack_bf16_pairs(acc_ref[...])

            @pl.when(first_row == 0)
            def _():
                o_ref[...] = packed

            @pl.when(first_row > 0)
            def _():
                row = lax.broadcasted_iota(jnp.int32, o_ref.shape, 0)
                o_ref[...] = jnp.where(row >= first_row, packed, o_ref[...])


def _expert_ffn(xs, items, w1, w3, w2, rows, fc):
    n_pairs, half = xs.shape
    d = 2 * half
    d_ff = w1.shape[2]
    item_tile, item_expert, item_lo, item_hi = items
    grid_spec = pltpu.PrefetchScalarGridSpec(
        num_scalar_prefetch=4,
        grid=(item_tile.shape[0], d_ff // fc),
        in_specs=[pl.BlockSpec((rows, half), lambda i, c, t, e, lo, hi: (t[i], 0)),
                  pl.BlockSpec((1, d, fc), lambda i, c, t, e, lo, hi: (e[i], 0, c)),
                  pl.BlockSpec((1, d, fc), lambda i, c, t, e, lo, hi: (e[i], 0, c)),
                  pl.BlockSpec((1, fc, d), lambda i, c, t, e, lo, hi: (e[i], c, 0))],
        out_specs=pl.BlockSpec((rows, half), lambda i, c, t, e, lo, hi: (t[i], 0)),
        scratch_shapes=[pltpu.VMEM((rows, d), BF16), pltpu.VMEM((rows, d), F32)],
    )
    return pl.pallas_call(
        _expert_ffn_kernel,
        grid_spec=grid_spec,
        out_shape=jax.ShapeDtypeStruct((n_pairs, half), jnp.uint32),
        compiler_params=_params("arbitrary", "arbitrary"),
        name="expert_ffn",
    )(item_tile, item_expert, item_lo, item_hi, xs, w1, w3, w2)


def _work_items(counts, n_pairs, rows):
    n_exp = counts.shape[0]
    n_tiles = n_pairs // rows
    ends = jnp.cumsum(counts)
    starts = ends - counts
    lo = jnp.sort(jnp.concatenate([jnp.arange(n_tiles, dtype=jnp.int32) * rows, starts[1:]]))
    hi = jnp.concatenate([lo[1:], jnp.full((1,), n_pairs, jnp.int32)])
    tile = jnp.minimum(lo // rows, n_tiles - 1)
    expert = jnp.minimum(jnp.searchsorted(ends, lo, side="right").astype(jnp.int32), n_exp - 1)
    return (tile, expert, lo, hi), starts


def _combine_kernel(x_ref, y1_ref, y2_ref, w_ref, fg_ref, o_ref, *, final_norm):
    w = w_ref[...]
    out = x_ref[...] + (w[:, 0:1] * _unpack_bf16_pairs(y1_ref[...]) + w[:, 1:2] * _unpack_bf16_pairs(y2_ref[...]))
    if final_norm:
        out = _rms(out, fg_ref[...], RMS_EPS)
    o_ref[...] = out


def _combine(x2, y_pairs, wts, final_g, tm, final_norm):
    n, d = x2.shape
    nt = n // tm
    return pl.pallas_call(
        functools.partial(_combine_kernel, final_norm=final_norm),
        grid=(nt,),
        in_specs=[pl.BlockSpec((tm, d), lambda i: (i, 0)),
                  pl.BlockSpec((tm, d // 2), lambda i: (i, 0)),
                  pl.BlockSpec((tm, d // 2), lambda i: (i + nt, 0)),
                  pl.BlockSpec((tm, wts.shape[1]), lambda i: (i, 0)),
                  pl.BlockSpec((1, d), lambda i: (0, 0))],
        out_specs=pl.BlockSpec((tm, d), lambda i: (i, 0)),
        out_shape=jax.ShapeDtypeStruct((n, d), F32),
        compiler_params=_params("parallel"),
        name="moe_combine",
    )(x2, y_pairs, y_pairs, wts, final_g)


def _moe(x2, g_ffn, router, w1, w3, w2, final_g, tm, final_norm):
    n, d = x2.shape
    n_exp = router.shape[1]
    router_pad = jnp.pad(router, ((0, 0), (0, 128 - n_exp)))
    idx, wts, h_packed, seen = _router(x2, g_ffn, router_pad, n_exp, tm)

    rows = _pick_tile(2 * n, 1024)
    counts = seen[0, :n_exp].astype(jnp.int32)
    items, starts = _work_items(counts, 2 * n, rows)
    pos = jnp.concatenate([starts[idx[:, 0]] + idx[:, 2], starts[idx[:, 1]] + idx[:, 3]])

    xs = _sc_scatter_pairs(h_packed, pos)
    ys = _expert_ffn(xs, items, w1, w3, w2, rows, _pick_tile(w1.shape[2], 512))
    y_pairs = _sc_gather(ys, pos)
    return _combine(x2, y_pairs, wts, final_g, tm, final_norm)


def _block_diag(w):
    g, c, _ = w.shape
    eye = jnp.eye(g, dtype=w.dtype)
    return (eye[:, None, :, None] * w[:, :, None, :]).reshape(g * c, g * c)


def _pick_tile(n, target):
    t = min(n, target)
    while n % t:
        t //= 2
    return t


def kernel(x, bias_table, mix_norm_g, w_in, conv_w, pool_w, pool_scale, diff_lambda, diff_subln_g,
           branch_proj, w_out, ffn_norm_g, dense_w1, dense_w3, dense_w2, moe_router, moe_w1, moe_w3,
           moe_w2, final_norm_g):
    batch, seq, d = x.shape
    depth = w_in.shape[0]
    n_mix = N_MIX_SLICES * BRANCH_WIDTH
    n = batch * seq
    assert seq % ATT_BLOCK == 0 and d % 128 == 0
    tm = _pick_tile(seq, 512)

    x2 = x.reshape(n, d)
    tiles = _bias_tiles(bias_table)
    row = lambda v: v.reshape(1, -1)
    final_g = row(final_norm_g)

    for i in range(depth):
        last_layer = i == depth - 1
        g_mix = row(mix_norm_g[i])
        w_mix = w_in[i, :, :n_mix].astype(BF16)
        w_gate = w_in[i, :, n_mix:].astype(BF16)
        u = _inproj(x2, g_mix, w_mix, tm)

        lam_init = 0.8 - 0.6 * math.exp(-0.3 * i)
        y_a = _moba(u, tiles, batch, seq)
        y_d = _diff(u, tiles, diff_lambda[i], row(jnp.tile(diff_subln_g[i], N_HEADS)), batch, seq, lam_init)
        y_bc = _convpool(u, conv_w[i], _block_diag(pool_w[i]).astype(BF16), row(pool_scale[i]), seq, tm)
        x2 = _merge(x2, g_mix, w_gate, y_a, y_bc, y_d,
                    branch_proj[i].reshape(-1, d).astype(BF16), w_out[i].astype(BF16), tm)

        g_ffn = row(ffn_norm_g[i])
        j = i // 2
        if i % 2 == 0:
            d_ff = dense_w1.shape[2]
            fc = d_ff // 2 if (d_ff // 2) % 128 == 0 else d_ff
            x2 = _ffn(x2, g_ffn, dense_w1[j].astype(BF16), dense_w3[j].astype(BF16),
                      dense_w2[j].astype(BF16), final_g, tm, fc, final_norm=last_layer)
        else:
            x2 = _moe(x2, g_ffn, moe_router[j], moe_w1[j].astype(BF16), moe_w3[j].astype(BF16),
                      moe_w2[j].astype(BF16), final_g, tm, final_norm=last_layer)

    return x2.reshape(batch, seq, d)
```

```python
import functools
import math

import numpy as np
import jax
import jax.numpy as jnp
from jax import lax
from jax.experimental import pallas as pl
from jax.experimental.pallas import tpu as pltpu
from jax.experimental.pallas import tpu_sc as plsc

F32 = jnp.float32
BF16 = jnp.bfloat16

BRANCH_WIDTH = 256
N_MIX_SLICES = 10
HEAD_WIDTH = 64
N_HEADS = 4
DIFF_QK_DIM = 32
ATT_BLOCK = 256
MOBA_TOPK = 3
CONV_WIDTH = 3
POOL_WINDOWS = (2, 4, 8, 16)
POOL_GROUP = 64
HALO = 16
REL_BUCKETS = 32
REL_MAX_DIST = 128
TOP_K_EXPERTS = 2
RMS_EPS = 1e-6
SUBLN_EPS = 1e-5
NEG_INF = -1e30
LOG2_E = math.log2(math.e)
V7X_VMEM_BYTES = 64 * 1024 * 1024
VMEM_LIMIT = V7X_VMEM_BYTES - 8 * 1024 * 1024

_TRANS_B = (((1,), (1,)), ((), ()))


def _params(*sem):
    return pltpu.CompilerParams(dimension_semantics=sem, vmem_limit_bytes=VMEM_LIMIT)


def _rms(x, g, eps):
    r = lax.rsqrt(jnp.mean(x * x, axis=-1, keepdims=True) + eps)
    return x * r * g


def _inproj_kernel(x_ref, g_ref, w_ref, o_ref):
    h = _rms(x_ref[...], g_ref[...], RMS_EPS).astype(BF16)
    o_ref[...] = jnp.dot(h, w_ref[...], preferred_element_type=F32)


def _inproj(x2, g, w, tm):
    n, d = x2.shape
    wn = w.shape[1]
    return pl.pallas_call(
        _inproj_kernel,
        grid=(n // tm,),
        in_specs=[pl.BlockSpec((tm, d), lambda i: (i, 0)),
                  pl.BlockSpec((1, d), lambda i: (0, 0)),
                  pl.BlockSpec((d, wn), lambda i: (0, 0))],
        out_specs=pl.BlockSpec((tm, wn), lambda i: (i, 0)),
        out_shape=jax.ShapeDtypeStruct((n, wn), F32),
        compiler_params=_params("parallel"),
        name="inproj",
    )(x2, g, w)


def _rel_bucket_np(dist):
    n = np.maximum(dist, 0)
    max_exact = REL_BUCKETS // 2
    nf = np.maximum(n, max_exact).astype(np.float32)
    large = max_exact + (np.log(nf / np.float32(max_exact)) / np.float32(math.log(REL_MAX_DIST / max_exact))
                         * np.float32(REL_BUCKETS - max_exact)).astype(np.int32)
    large = np.minimum(large, REL_BUCKETS - 1)
    return np.where(n < max_exact, n, large).astype(np.int32)


def _bucket_tiles():
    i = np.arange(ATT_BLOCK)[:, None]
    j = np.arange(ATT_BLOCK)[None, :]
    return np.stack([_rel_bucket_np(i - j), _rel_bucket_np(ATT_BLOCK + i - j)])


def _bias_tiles_kernel(tab_ref, bkt_ref, o_ref):
    h = pl.program_id(0)
    bkt = bkt_ref[...]
    acc = jnp.zeros(bkt.shape, F32)
    for b in range(REL_BUCKETS):
        acc = jnp.where(bkt == b, tab_ref[b, h], acc)
    o_ref[0] = (acc - tab_ref[REL_BUCKETS - 1, h]) * LOG2_E


def _bias_tiles(bias_table):
    n_heads = bias_table.shape[1]
    bkt = jnp.asarray(_bucket_tiles())
    return pl.pallas_call(
        _bias_tiles_kernel,
        grid=(n_heads,),
        in_specs=[pl.BlockSpec(memory_space=pltpu.SMEM),
                  pl.BlockSpec((2, ATT_BLOCK, ATT_BLOCK), lambda h: (0, 0, 0))],
        out_specs=pl.BlockSpec((1, 2, ATT_BLOCK, ATT_BLOCK), lambda h: (h, 0, 0, 0)),
        out_shape=jax.ShapeDtypeStruct((n_heads, 2, ATT_BLOCK, ATT_BLOCK), F32),
        compiler_params=_params("arbitrary"),
        name="bias_tiles",
    )(bias_table, bkt)


LANE_TILE = 128
SUBLANES = 8


def _head_lanes(lane, h):
    return (lane >= h * HEAD_WIDTH) & (lane < (h + 1) * HEAD_WIDTH)


def _fold(op, s):
    out = s[:, :LANE_TILE]
    for c in range(LANE_TILE, s.shape[1], LANE_TILE):
        out = op(out, s[:, c:c + LANE_TILE])
    return out


def _stage_keys_values(k_ref, v_ref, kb_ref, vm_ref):
    t = ATT_BLOCK
    kb_ref[:, 0:BRANCH_WIDTH] = k_ref[...].astype(BF16)
    lane = lax.broadcasted_iota(jnp.int32, (t, BRANCH_WIDTH), 1)
    for j in range(vm_ref.shape[0]):
        vj = v_ref[j * t:(j + 1) * t, :]
        for h in range(N_HEADS):
            vm_ref[j, h * t:(h + 1) * t, :] = jnp.where(_head_lanes(lane, h), vj, 0.0).astype(BF16)


def _softmax_attend(qb, q16, head_of, groups, kb_ref, vm_ref, tiles_ref, s_ref, m_ref, l_ref, acc_ref):
    t = ATT_BLOCK
    chains = range(len(q16))
    causal = (lax.broadcasted_iota(jnp.int32, (t, t), 0) >= lax.broadcasted_iota(jnp.int32, (t, t), 1))

    def keys(j):
        return kb_ref[pl.ds(pl.multiple_of(j * t, t), t), :]

    def score(i, kj):
        return lax.dot_general(q16[i], kj, _TRANS_B, preferred_element_type=F32)

    def put(i, j, s, first):
        s_ref[i, j] = s
        m_ref[i] = _fold(jnp.maximum, s) if first else jnp.maximum(m_ref[i], _fold(jnp.maximum, s))

    def in_pairs(count, fn):
        def pair(p, carry):
            fn([2 * p, 2 * p + 1])
            return carry

        lax.fori_loop(0, count >> 1, pair, 0)

        @pl.when((count & 1) == 1)
        def _():
            fn([count - 1])

    k_own = keys(qb)
    for i in chains:
        put(i, qb, jnp.where(causal, score(i, k_own) + tiles_ref[head_of(i), 0], NEG_INF), True)

    @pl.when(qb >= 1)
    def _():
        k_prev = keys(qb - 1)
        for i in chains:
            put(i, qb - 1, score(i, k_prev) + tiles_ref[head_of(i), 1], False)

    def far(js):
        ks = [keys(j) for j in js]
        for i in chains:
            ss = [score(i, kj) for kj in ks]
            for j, s in zip(js, ss):
                s_ref[i, j] = s
            m_ref[i] = functools.reduce(jnp.maximum, [m_ref[i]] + [_fold(jnp.maximum, s) for s in ss])

    in_pairs(jnp.maximum(qb - 1, 0), far)

    for i in chains:
        m_ref[i] = jnp.broadcast_to(jnp.max(m_ref[i], axis=1, keepdims=True), (t, LANE_TILE))

    def accumulate(js, first=False):
        for g, members in enumerate(groups):
            parts = [[] for _ in js]
            for i in members:
                sums = []
                for a, j in enumerate(js):
                    s = s_ref[i, j]
                    p = [jnp.exp2(s[:, c:c + LANE_TILE] - m_ref[i]) for c in range(0, t, LANE_TILE)]
                    sums += p
                    parts[a] += [x.astype(BF16) for x in p]
                row_sum = functools.reduce(jnp.add, sums)
                l_ref[i] = row_sum if first else l_ref[i] + row_sum
            lhs = jnp.concatenate([x for tile_parts in parts for x in tile_parts], axis=1)
            rhs = jnp.concatenate([vm_ref[j] for j in js], axis=0) if len(js) > 1 else vm_ref[js[0]]
            pv = jnp.dot(lhs, rhs, preferred_element_type=F32)
            acc_ref[g] = pv if first else acc_ref[g] + pv

    accumulate([qb], first=True)
    in_pairs(qb, accumulate)


def _per_head_lanes(lane, cols):
    out = jnp.broadcast_to(cols[0], lane.shape)
    for h in range(1, N_HEADS):
        out = jnp.where(_head_lanes(lane, h), cols[h], out)
    return out


def _attention_scratch(seq, n_chains, n_groups, key_width=BRANCH_WIDTH):
    t = ATT_BLOCK
    n_blk = seq // t
    return [pltpu.VMEM((seq, key_width), BF16),
            pltpu.VMEM((n_blk, N_HEADS * t, BRANCH_WIDTH), BF16),
            pltpu.VMEM((n_chains, n_blk, t, t), F32),
            pltpu.VMEM((n_chains, t, LANE_TILE), F32),
            pltpu.VMEM((n_chains, t, LANE_TILE), F32),
            pltpu.VMEM((n_groups, t, BRANCH_WIDTH), F32)]


def _moba_kernel(q_ref, k_ref, v_ref, tiles_ref, o_ref, kb_ref, vm_ref, s_ref, m_ref, l_ref, acc_ref, kmean_ref):
    t = ATT_BLOCK
    n_blk = k_ref.shape[0] // t
    qb = pl.program_id(1)

    heads = range(N_HEADS)
    blk_rows = kmean_ref.shape[0] // N_HEADS
    assert n_blk <= blk_rows <= LANE_TILE

    @pl.when(qb == 0)
    def _():
        _stage_keys_values(k_ref, v_ref, kb_ref, vm_ref)
        onehot_lane = lax.broadcasted_iota(jnp.int32, (t, LANE_TILE), 1)
        ch = lax.broadcasted_iota(jnp.int32, (1, BRANCH_WIDTH), 1)
        kmean_ref[...] = jnp.zeros(kmean_ref.shape, F32)
        for j in range(n_blk):
            kb_ref[j * t:(j + 1) * t, BRANCH_WIDTH:] = (onehot_lane == j).astype(BF16)
            mean_j = jnp.mean(k_ref[j * t:(j + 1) * t, :], axis=0, keepdims=True)
            for h in heads:
                kmean_ref[h * blk_rows + j:h * blk_rows + j + 1, :] = jnp.where(_head_lanes(ch, h), mean_j, 0.0)

    lane = lax.broadcasted_iota(jnp.int32, (t, BRANCH_WIDTH), 1)
    q = q_ref[...]
    gate = lax.dot_general(kmean_ref[...], q * (HEAD_WIDTH ** -0.5), _TRANS_B, precision=lax.Precision.HIGHEST,
                           preferred_element_type=F32)
    blk = lax.broadcasted_iota(jnp.int32, (blk_rows, t), 0).astype(F32)
    drops = []
    for h in heads:
        g = gate[h * blk_rows:(h + 1) * blk_rows, :]
        avail = blk < qb.astype(F32)
        for _ in range(MOBA_TOPK):
            best = jnp.max(jnp.where(avail, g, -jnp.inf), axis=0, keepdims=True)
            first = jnp.min(jnp.where(avail & (g == best), blk, float(blk_rows)), axis=0, keepdims=True)
            avail = avail & (blk != first)
        drops += [jnp.where(avail, NEG_INF, 0.0), jnp.zeros((LANE_TILE - blk_rows, t), F32)]
    eye = (lax.broadcasted_iota(jnp.int32, (t, t), 0) == lax.broadcasted_iota(jnp.int32, (t, t), 1)).astype(BF16)
    drop = lax.dot_general(eye, jnp.concatenate(drops, axis=0).astype(BF16), _TRANS_B,
                           preferred_element_type=F32).astype(BF16)
    qf = q * (HEAD_WIDTH ** -0.5 * LOG2_E)
    q_aug = [jnp.concatenate([jnp.where(_head_lanes(lane, h), qf, 0.0).astype(BF16),
                              drop[:, h * LANE_TILE:(h + 1) * LANE_TILE]], axis=1) for h in heads]

    _softmax_attend(qb, q_aug, lambda i: i, [list(heads)], kb_ref, vm_ref, tiles_ref, s_ref, m_ref, l_ref, acc_ref)
    row_sums = _per_head_lanes(lane, [jnp.sum(l_ref[h], axis=1, keepdims=True) for h in heads])
    o_ref[...] = (acc_ref[0] / row_sums).astype(o_ref.dtype)


def _moba(u, tiles, batch, seq):
    t = ATT_BLOCK
    nq = seq // t
    return pl.pallas_call(
        _moba_kernel,
        grid=(batch, nq),
        in_specs=[pl.BlockSpec((t, BRANCH_WIDTH), lambda b, q: (b * nq + q, 0)),
                  pl.BlockSpec((seq, BRANCH_WIDTH), lambda b, q: (b, 1)),
                  pl.BlockSpec((seq, BRANCH_WIDTH), lambda b, q: (b, 2)),
                  pl.BlockSpec((N_HEADS, 2, t, t), lambda b, q: (0, 0, 0, 0))],
        out_specs=pl.BlockSpec((t, BRANCH_WIDTH), lambda b, q: (b * nq + q, 0)),
        out_shape=jax.ShapeDtypeStruct((batch * seq, BRANCH_WIDTH), BF16),
        scratch_shapes=(_attention_scratch(seq, N_HEADS, 1, BRANCH_WIDTH + LANE_TILE)
                        + [pltpu.VMEM((N_HEADS * SUBLANES * pl.cdiv(nq, SUBLANES), BRANCH_WIDTH), F32)]),
        compiler_params=_params("arbitrary", "arbitrary"),
        name="moba",
    )(u, u, u, tiles)


def _diff_kernel(lam_ref, g_ref, q_ref, k_ref, v_ref, tiles_ref, o_ref, kb_ref, vm_ref, s_ref, m_ref, l_ref,
                 acc_ref, *, lam_init):
    t = ATT_BLOCK
    qb = pl.program_id(1)

    @pl.when(qb == 0)
    def _():
        _stage_keys_values(k_ref, v_ref, kb_ref, vm_ref)

    lp = lam_ref[...]
    lam = (jnp.exp(jnp.sum(lp[0:1] * lp[1:2], axis=1, keepdims=True))
           - jnp.exp(jnp.sum(lp[2:3] * lp[3:4], axis=1, keepdims=True)) + lam_init)

    lane = lax.broadcasted_iota(jnp.int32, (t, BRANCH_WIDTH), 1)
    qf = q_ref[...] * (DIFF_QK_DIM ** -0.5 * LOG2_E)
    q16 = []
    for h in range(N_HEADS):
        for c in range(2):
            lo = h * HEAD_WIDTH + c * DIFF_QK_DIM
            q16.append(jnp.where((lane >= lo) & (lane < lo + DIFF_QK_DIM), qf, 0.0).astype(BF16))
    groups = [[2 * h + c for h in range(N_HEADS)] for c in range(2)]
    _softmax_attend(qb, q16, lambda i: i // 2, groups, kb_ref, vm_ref, tiles_ref, s_ref, m_ref, l_ref, acc_ref)

    row_sums = [_per_head_lanes(lane, [jnp.sum(l_ref[i], axis=1, keepdims=True) for i in members])
                for members in groups]
    o = acc_ref[0] / row_sums[0] - lam * (acc_ref[1] / row_sums[1])
    sq = o * o
    mean_sq = _per_head_lanes(lane, [jnp.sum(jnp.where(_head_lanes(lane, h), sq, 0.0), axis=1, keepdims=True)
                                     for h in range(N_HEADS)]) * (1.0 / HEAD_WIDTH)
    o_ref[...] = (o * lax.rsqrt(mean_sq + SUBLN_EPS) * g_ref[...] * (1.0 - lam_init)).astype(o_ref.dtype)


def _diff(u, tiles, lam_params, subln_g4, batch, seq, lam_init):
    t = ATT_BLOCK
    nq = seq // t
    return pl.pallas_call(
        functools.partial(_diff_kernel, lam_init=lam_init),
        grid=(batch, nq),
        in_specs=[pl.BlockSpec((4, DIFF_QK_DIM), lambda b, q: (0, 0)),
                  pl.BlockSpec((1, BRANCH_WIDTH), lambda b, q: (0, 0)),
                  pl.BlockSpec((t, BRANCH_WIDTH), lambda b, q: (b * nq + q, 7)),
                  pl.BlockSpec((seq, BRANCH_WIDTH), lambda b, q: (b, 8)),
                  pl.BlockSpec((seq, BRANCH_WIDTH), lambda b, q: (b, 9)),
                  pl.BlockSpec((N_HEADS, 2, t, t), lambda b, q: (1, 0, 0, 0))],
        out_specs=pl.BlockSpec((t, BRANCH_WIDTH), lambda b, q: (b * nq + q, 0)),
        out_shape=jax.ShapeDtypeStruct((batch * seq, BRANCH_WIDTH), BF16),
        scratch_shapes=_attention_scratch(seq, 2 * N_HEADS, 2),
        compiler_params=_params("arbitrary", "arbitrary"),
        name="diff_attn",
    )(lam_params, subln_g4, u, u, u, tiles)


def _convpool_kernel(xb_ref, bb_ref, cb_ref, pc_ref, xh_ref, ch_ref, ph_ref, cw_ref, pw_ref, ps_ref,
                     o_ref, u_ref, s_ref, *, seq):
    tm = xb_ref.shape[0]
    pos0 = (pl.program_id(0) * tm) % seq
    has_history = pos0 > 0
    pos = pos0 + lax.broadcasted_iota(jnp.int32, (tm, 1), 0)
    lane = lax.broadcasted_iota(jnp.int32, (tm, BRANCH_WIDTH), 1)

    u_ref[0:HALO, :] = jnp.where(has_history, ch_ref[...] * xh_ref[...], 0.0)
    u_ref[HALO:, :] = cb_ref[...] * xb_ref[...]
    conv = cw_ref[CONV_WIDTH - 1:CONV_WIDTH, :] * u_ref[HALO:, :]
    for i in range(CONV_WIDTH - 1):
        shift = CONV_WIDTH - 1 - i
        conv = conv + cw_ref[i:i + 1, :] * u_ref[HALO - shift:HALO - shift + tm, :]
    o_ref[:, 0:BRANCH_WIDTH] = (bb_ref[...] * conv).astype(o_ref.dtype)

    s_ref[0:HALO, :] = jnp.where(has_history, ph_ref[...], 0.0)
    s_ref[HALO:, :] = pc_ref[...]
    pooled = jnp.zeros((tm, BRANCH_WIDTH), F32)
    done = 0
    for g, w in enumerate(POOL_WINDOWS):
        half = w // 2
        cur = s_ref[done + half:, :] + s_ref[done:HALO + tm - half, :]
        done += half
        s_ref[done:, :] = cur
        cnt = jnp.minimum(pos + 1, w).astype(F32)
        mean_w = s_ref[HALO:, :] / cnt
        pooled = jnp.where((lane >= g * POOL_GROUP) & (lane < (g + 1) * POOL_GROUP), mean_w, pooled)
    pooled = pooled - pc_ref[...]
    y_c = jnp.dot(pooled.astype(BF16), pw_ref[...], preferred_element_type=F32) * ps_ref[...]
    o_ref[:, BRANCH_WIDTH:] = y_c.astype(o_ref.dtype)


def _convpool(u, conv_w, pool_w_bd, pool_scale, seq, tm):
    n = u.shape[0]
    rows = lambda c: pl.BlockSpec((tm, BRANCH_WIDTH), lambda i, c=c: (i, c))
    halo = lambda c: pl.BlockSpec((HALO, BRANCH_WIDTH),
                                  lambda i, c=c: (jnp.maximum(i * (tm // HALO) - 1, 0), c))
    full = lambda a: pl.BlockSpec(a.shape, lambda i: (0,) * a.ndim)
    return pl.pallas_call(
        functools.partial(_convpool_kernel, seq=seq),
        grid=(n // tm,),
        in_specs=[rows(3), rows(4), rows(5), rows(6), halo(3), halo(5), halo(6),
                  full(conv_w), full(pool_w_bd), full(pool_scale)],
        out_specs=pl.BlockSpec((tm, 2 * BRANCH_WIDTH), lambda i: (i, 0)),
        out_shape=jax.ShapeDtypeStruct((n, 2 * BRANCH_WIDTH), BF16),
        scratch_shapes=[pltpu.VMEM((HALO + tm, BRANCH_WIDTH), F32),
                        pltpu.VMEM((HALO + tm, BRANCH_WIDTH), F32)],
        compiler_params=_params("parallel"),
        name="convpool",
    )(u, u, u, u, u, u, u, conv_w, pool_w_bd, pool_scale)


def _merge_kernel(x_ref, g_ref, wg_ref, ya_ref, ybc_ref, yd_ref, bp_ref, wo_ref, o_ref):
    d = x_ref.shape[1]
    x = x_ref[...]
    h = _rms(x, g_ref[...], RMS_EPS).astype(BF16)
    ys = (ya_ref[...], ybc_ref[:, 0:BRANCH_WIDTH], ybc_ref[:, BRANCH_WIDTH:], yd_ref[...])
    merged = jnp.zeros(x.shape, F32)
    for b, y in enumerate(ys):
        gate = jax.nn.sigmoid(jnp.dot(h, wg_ref[:, b * d:(b + 1) * d], preferred_element_type=F32))
        merged = merged + gate * jnp.dot(y, bp_ref[b * BRANCH_WIDTH:(b + 1) * BRANCH_WIDTH, :],
                                         preferred_element_type=F32)
    o_ref[...] = x + jnp.dot(merged.astype(BF16), wo_ref[...], preferred_element_type=F32)


def _merge(x2, g, w_gate, y_a, y_bc, y_d, bp, w_out, tm):
    n, d = x2.shape
    full = lambda a: pl.BlockSpec(a.shape, lambda i: (0,) * a.ndim)
    rows = lambda a: pl.BlockSpec((tm, a.shape[1]), lambda i: (i, 0))
    return pl.pallas_call(
        _merge_kernel,
        grid=(n // tm,),
        in_specs=[rows(x2), full(g), full(w_gate), rows(y_a), rows(y_bc), rows(y_d), full(bp), full(w_out)],
        out_specs=pl.BlockSpec((tm, d), lambda i: (i, 0)),
        out_shape=jax.ShapeDtypeStruct((n, d), F32),
        compiler_params=_params("parallel"),
        name="merge",
    )(x2, g, w_gate, y_a, y_bc, y_d, bp, w_out)


ROUTE_COLS = 8


def _pack_bf16_pairs(h):
    c = h.shape[1] // 2
    bits = lax.bitcast_convert_type(h.astype(BF16).astype(F32), jnp.uint32)
    return (bits[:, :c] >> 16) | (bits[:, c:] & jnp.uint32(0xFFFF0000))


def _unpack_bf16_pairs(w):
    lo = lax.bitcast_convert_type(w << 16, F32)
    hi = lax.bitcast_convert_type(w & jnp.uint32(0xFFFF0000), F32)
    return jnp.concatenate([lo, hi], axis=1)


def _router_kernel(x_ref, g_ref, r_ref, tri_ref, idx_ref, wts_ref, hpk_ref, cnt_ref, seen_ref, *, n_exp):
    @pl.when(pl.program_id(0) == 0)
    def _():
        seen_ref[...] = jnp.zeros(seen_ref.shape, F32)

    h = _rms(x_ref[...], g_ref[...], RMS_EPS)
    logits = jnp.dot(h, r_ref[...], precision=lax.Precision.HIGHEST, preferred_element_type=F32)
    lane = lax.broadcasted_iota(jnp.int32, logits.shape, 1).astype(F32)
    logits = jnp.where(lane < n_exp, logits, -jnp.inf)
    big = float(logits.shape[1])
    m1 = jnp.max(logits, axis=1, keepdims=True)
    i1 = jnp.min(jnp.where(logits == m1, lane, big), axis=1, keepdims=True)
    rest = jnp.where(lane == i1, -jnp.inf, logits)
    m2 = jnp.max(rest, axis=1, keepdims=True)
    i2 = jnp.min(jnp.where(rest == m2, lane, big), axis=1, keepdims=True)
    e2 = jnp.exp(m2 - m1)
    w1 = 1.0 / (1.0 + e2)
    w2 = e2 / (1.0 + e2)

    chosen = (lane == i1) | (lane == i2)
    before = seen_ref[...] + jnp.dot(tri_ref[...], chosen.astype(BF16), preferred_element_type=F32)
    r1 = jnp.sum(jnp.where(lane == i1, before, 0.0), axis=1, keepdims=True)
    r2 = jnp.sum(jnp.where(lane == i2, before, 0.0), axis=1, keepdims=True)
    seen_ref[...] += jnp.sum(chosen.astype(F32), axis=0, keepdims=True)
    cnt_ref[...] = seen_ref[...]

    col = lax.broadcasted_iota(jnp.int32, idx_ref.shape, 1)
    pick = lambda a, b, c, d: jnp.where(col == 0, a, jnp.where(col == 1, b, jnp.where(col == 2, c, d)))
    idx_ref[...] = pick(i1, i2, r1, jnp.where(col == 3, r2, 0.0)).astype(jnp.int32)
    wts_ref[...] = pick(w1, w2, 0.0, 0.0)
    hpk_ref[...] = _pack_bf16_pairs(h)


def _router(x2, g, router_pad, n_exp, tm):
    n, d = x2.shape
    tri = jnp.asarray(np.tril(np.ones((tm, tm), np.float32), -1), BF16)
    full = lambda a: pl.BlockSpec(a.shape, lambda i: (0,) * a.ndim)
    rows = lambda w: pl.BlockSpec((tm, w), lambda i: (i, 0))
    return pl.pallas_call(
        functools.partial(_router_kernel, n_exp=n_exp),
        grid=(n // tm,),
        in_specs=[rows(d), full(g), full(router_pad), full(tri)],
        out_specs=[rows(ROUTE_COLS), rows(ROUTE_COLS), rows(d // 2),
                   pl.BlockSpec((1, router_pad.shape[1]), lambda i: (0, 0))],
        out_shape=[jax.ShapeDtypeStruct((n, ROUTE_COLS), jnp.int32),
                   jax.ShapeDtypeStruct((n, ROUTE_COLS), F32),
                   jax.ShapeDtypeStruct((n, d // 2), jnp.uint32),
                   jax.ShapeDtypeStruct((1, router_pad.shape[1]), F32)],
        scratch_shapes=[pltpu.VMEM((1, router_pad.shape[1]), F32)],
        compiler_params=_params("arbitrary"),
        name="router",
    )(x2, g, router_pad, tri)


V7X_SC_CORES = 2
V7X_SC_SUBCORES = 16
SC_GATHER_BYTES = 256 * 1024
SC_MAX_INDEX_VECTOR = 128


def _sc_gather(table, idx):
    n_rows, width = idx.shape[0], table.shape[1]
    workers = V7X_SC_CORES * V7X_SC_SUBCORES
    chunk = min(SC_MAX_INDEX_VECTOR, SC_GATHER_BYTES // (width * table.dtype.itemsize))
    assert table.dtype.itemsize == 4 and n_rows % (workers * chunk) == 0 and chunk % 8 == 0
    per_worker = n_rows // workers
    mesh = plsc.VectorSubcoreMesh(core_axis_name="c", subcore_axis_name="s",
                                  num_cores=V7X_SC_CORES, num_subcores=V7X_SC_SUBCORES)

    def body(table_hbm, idx_hbm, out_hbm, idx_v, rows_v, sem):
        base = (lax.axis_index("s") * V7X_SC_CORES + lax.axis_index("c")) * per_worker

        @pl.loop(0, per_worker // chunk)
        def _(i):
            off = base + i * chunk
            pltpu.sync_copy(idx_hbm.at[pl.ds(off, chunk)], idx_v)
            pltpu.async_copy(table_hbm.at[idx_v], rows_v, sem).wait()
            pltpu.sync_copy(rows_v, out_hbm.at[pl.ds(off, chunk)])

    return pl.kernel(
        body,
        out_type=jax.ShapeDtypeStruct((n_rows, width), table.dtype),
        mesh=mesh,
        scratch_types=[pltpu.VMEM((chunk,), jnp.int32), pltpu.VMEM((chunk, width), table.dtype),
                       pltpu.SemaphoreType.DMA],
        name="sc_gather",
    )(table, idx)


def _sc_scatter_pairs(table, pos):
    n, width = table.shape
    workers = V7X_SC_CORES * V7X_SC_SUBCORES
    chunk = min(SC_MAX_INDEX_VECTOR, SC_GATHER_BYTES // (width * table.dtype.itemsize))
    assert table.dtype.itemsize == 4 and n % (workers * chunk) == 0 and chunk % 8 == 0
    per_worker = n // workers
    mesh = plsc.VectorSubcoreMesh(core_axis_name="c", subcore_axis_name="s",
                                  num_cores=V7X_SC_CORES, num_subcores=V7X_SC_SUBCORES)

    def body(table_hbm, pos_hbm, out_hbm, idx_a, idx_b, rows_v, sem):
        base = (lax.axis_index("s") * V7X_SC_CORES + lax.axis_index("c")) * per_worker

        @pl.loop(0, per_worker // chunk)
        def _(i):
            off = base + i * chunk
            pltpu.sync_copy(table_hbm.at[pl.ds(off, chunk)], rows_v)
            pltpu.sync_copy(pos_hbm.at[pl.ds(off, chunk)], idx_a)
            pltpu.sync_copy(pos_hbm.at[pl.ds(n + off, chunk)], idx_b)
            pltpu.async_copy(rows_v, out_hbm.at[idx_a], sem).wait()
            pltpu.async_copy(rows_v, out_hbm.at[idx_b], sem).wait()

    return pl.kernel(
        body,
        out_type=jax.ShapeDtypeStruct((2 * n, width), table.dtype),
        mesh=mesh,
        scratch_types=[pltpu.VMEM((chunk,), jnp.int32), pltpu.VMEM((chunk,), jnp.int32),
                       pltpu.VMEM((chunk, width), table.dtype), pltpu.SemaphoreType.DMA],
        name="sc_scatter_pairs",
    )(table, pos)


def _swiglu_chunk(h, w1, w3, w2):
    a = jnp.dot(h, w1, preferred_element_type=F32)
    b = jnp.dot(h, w3, preferred_element_type=F32)
    t = (a * jax.nn.sigmoid(a) * b).astype(BF16)
    return jnp.dot(t, w2, preferred_element_type=F32)


def _ffn_kernel(x_ref, g_ref, w1_ref, w3_ref, w2_ref, fg_ref, o_ref, *, fc, final_norm):
    d_ff = w1_ref.shape[1]
    x = x_ref[...]
    h = _rms(x, g_ref[...], RMS_EPS).astype(BF16)
    acc = None
    for c0 in range(0, d_ff, fc):
        c1 = min(c0 + fc, d_ff)
        y = _swiglu_chunk(h, w1_ref[:, c0:c1], w3_ref[:, c0:c1], w2_ref[c0:c1, :])
        acc = y if acc is None else acc + y
    out = x + acc
    if final_norm:
        out = _rms(out, fg_ref[...], RMS_EPS)
    o_ref[...] = out


def _ffn(x2, g, w1, w3, w2, final_g, tm, fc, final_norm):
    n, d = x2.shape
    resident = lambda a: pl.BlockSpec(a.shape, lambda i: (0,) * a.ndim, pipeline_mode=pl.Buffered(1))
    return pl.pallas_call(
        functools.partial(_ffn_kernel, fc=fc, final_norm=final_norm),
        grid=(n // tm,),
        in_specs=[pl.BlockSpec((tm, d), lambda i: (i, 0)),
                  pl.BlockSpec((1, d), lambda i: (0, 0)),
                  resident(w1), resident(w3), resident(w2),
                  pl.BlockSpec((1, d), lambda i: (0, 0))],
        out_specs=pl.BlockSpec((tm, d), lambda i: (i, 0)),
        out_shape=jax.ShapeDtypeStruct((n, d), F32),
        compiler_params=_params("parallel"),
        name="dense_ffn",
    )(x2, g, w1, w3, w2, final_g)


def _expert_ffn_kernel(tile_ref, exp_ref, lo_ref, hi_ref, xs_ref, w1_ref, w3_ref, w2_ref, o_ref, h_ref, acc_ref):
    it = pl.program_id(0)
    c = pl.program_id(1)
    rows = o_ref.shape[0]
    lo, hi = lo_ref[it], hi_ref[it]

    @pl.when(hi > lo)
    def _():
        @pl.when(c == 0)
        def _():
            h_ref[...] = _unpack_bf16_pairs(xs_ref[...]).astype(BF16)

        y = _swiglu_chunk(h_ref[...], w1_ref[0], w3_ref[0], w2_ref[0])

        @pl.when(c == 0)
        def _():
            acc_ref[...] = y

        @pl.when(c > 0)
        def _():
            acc_ref[...] += y

        @pl.when(c == pl.num_programs(1) - 1)
        def _():
            first_row = lo - tile_ref[it] * rows
            packed = _pack_bf16_pairs(acc_ref[...])

            @pl.when(first_row == 0)
            def _():
                o_ref[...] = packed

            @pl.when(first_row > 0)
            def _():
                row = lax.broadcasted_iota(jnp.int32, o_ref.shape, 0)
                o_ref[...] = jnp.where(row >= first_row, packed, o_ref[...])


def _expert_ffn(xs, items, w1, w3, w2, rows, fc):
    n_pairs, half = xs.shape
    d = 2 * half
    d_ff = w1.shape[2]
    item_tile, item_expert, item_lo, item_hi = items
    grid_spec = pltpu.PrefetchScalarGridSpec(
        num_scalar_prefetch=4,
        grid=(item_tile.shape[0], d_ff // fc),
        in_specs=[pl.BlockSpec((rows, half), lambda i, c, t, e, lo, hi: (t[i], 0)),
                  pl.BlockSpec((1, d, fc), lambda i, c, t, e, lo, hi: (e[i], 0, c)),
                  pl.BlockSpec((1, d, fc), lambda i, c, t, e, lo, hi: (e[i], 0, c)),
                  pl.BlockSpec((1, fc, d), lambda i, c, t, e, lo, hi: (e[i], c, 0))],
        out_specs=pl.BlockSpec((rows, half), lambda i, c, t, e, lo, hi: (t[i], 0)),
        scratch_shapes=[pltpu.VMEM((rows, d), BF16), pltpu.VMEM((rows, d), F32)],
    )
    return pl.pallas_call(
        _expert_ffn_kernel,
        grid_spec=grid_spec,
        out_shape=jax.ShapeDtypeStruct((n_pairs, half), jnp.uint32),
        compiler_params=_params("arbitrary", "arbitrary"),
        name="expert_ffn",
    )(item_tile, item_expert, item_lo, item_hi, xs, w1, w3, w2)


def _work_items(counts, n_pairs, rows):
    n_exp = counts.shape[0]
    n_tiles = n_pairs // rows
    ends = jnp.cumsum(counts)
    starts = ends - counts
    lo = jnp.sort(jnp.concatenate([jnp.arange(n_tiles, dtype=jnp.int32) * rows, starts[1:]]))
    hi = jnp.concatenate([lo[1:], jnp.full((1,), n_pairs, jnp.int32)])
    tile = jnp.minimum(lo // rows, n_tiles - 1)
    expert = jnp.minimum(jnp.searchsorted(ends, lo, side="right").astype(jnp.int32), n_exp - 1)
    return (tile, expert, lo, hi), starts


def _combine_kernel(x_ref, y1_ref, y2_ref, w_ref, fg_ref, o_ref, *, final_norm):
    w = w_ref[...]
    out = x_ref[...] + (w[:, 0:1] * _unpack_bf16_pairs(y1_ref[...]) + w[:, 1:2] * _unpack_bf16_pairs(y2_ref[...]))
    if final_norm:
        out = _rms(out, fg_ref[...], RMS_EPS)
    o_ref[...] = out


def _combine(x2, y_pairs, wts, final_g, tm, final_norm):
    n, d = x2.shape
    nt = n // tm
    return pl.pallas_call(
        functools.partial(_combine_kernel, final_norm=final_norm),
        grid=(nt,),
        in_specs=[pl.BlockSpec((tm, d), lambda i: (i, 0)),
                  pl.BlockSpec((tm, d // 2), lambda i: (i, 0)),
                  pl.BlockSpec((tm, d // 2), lambda i: (i + nt, 0)),
                  pl.BlockSpec((tm, wts.shape[1]), lambda i: (i, 0)),
                  pl.BlockSpec((1, d), lambda i: (0, 0))],
        out_specs=pl.BlockSpec((tm, d), lambda i: (i, 0)),
        out_shape=jax.ShapeDtypeStruct((n, d), F32),
        compiler_params=_params("parallel"),
        name="moe_combine",
    )(x2, y_pairs, y_pairs, wts, final_g)


def _moe(x2, g_ffn, router, w1, w3, w2, final_g, tm, final_norm):
    n, d = x2.shape
    n_exp = router.shape[1]
    router_pad = jnp.pad(router, ((0, 0), (0, 128 - n_exp)))
    idx, wts, h_packed, seen = _router(x2, g_ffn, router_pad, n_exp, tm)

    rows = _pick_tile(2 * n, 512)
    counts = seen[0, :n_exp].astype(jnp.int32)
    items, starts = _work_items(counts, 2 * n, rows)
    pos = jnp.concatenate([starts[idx[:, 0]] + idx[:, 2], starts[idx[:, 1]] + idx[:, 3]])

    xs = _sc_scatter_pairs(h_packed, pos)
    ys = _expert_ffn(xs, items, w1, w3, w2, rows, _pick_tile(w1.shape[2], 1792))
    y_pairs = _sc_gather(ys, pos)
    return _combine(x2, y_pairs, wts, final_g, tm, final_norm)


def _block_diag(w):
    g, c, _ = w.shape
    eye = jnp.eye(g, dtype=w.dtype)
    return (eye[:, None, :, None] * w[:, :, None, :]).reshape(g * c, g * c)


def _pick_tile(n, target):
    t = min(n, target)
    while n % t:
        t //= 2
    return t


def kernel(x, bias_table, mix_norm_g, w_in, conv_w, pool_w, pool_scale, diff_lambda, diff_subln_g,
           branch_proj, w_out, ffn_norm_g, dense_w1, dense_w3, dense_w2, moe_router, moe_w1, moe_w3,
           moe_w2, final_norm_g):
    batch, seq, d = x.shape
    depth = w_in.shape[0]
    n_mix = N_MIX_SLICES * BRANCH_WIDTH
    n = batch * seq
    assert seq % ATT_BLOCK == 0 and d % 128 == 0
    tm = _pick_tile(seq, 512)

    x2 = x.reshape(n, d)
    tiles = _bias_tiles(bias_table)
    row = lambda v: v.reshape(1, -1)
    final_g = row(final_norm_g)

    for i in range(depth):
        last_layer = i == depth - 1
        g_mix = row(mix_norm_g[i])
        w_mix = w_in[i, :, :n_mix].astype(BF16)
        w_gate = w_in[i, :, n_mix:].astype(BF16)
        u = _inproj(x2, g_mix, w_mix, tm)

        lam_init = 0.8 - 0.6 * math.exp(-0.3 * i)
        y_a = _moba(u, tiles, batch, seq)
        y_d = _diff(u, tiles, diff_lambda[i], row(jnp.tile(diff_subln_g[i], N_HEADS)), batch, seq, lam_init)
        y_bc = _convpool(u, conv_w[i], _block_diag(pool_w[i]).astype(BF16), row(pool_scale[i]), seq, tm)
        x2 = _merge(x2, g_mix, w_gate, y_a, y_bc, y_d,
                    branch_proj[i].reshape(-1, d).astype(BF16), w_out[i].astype(BF16), tm)

        g_ffn = row(ffn_norm_g[i])
        j = i // 2
        if i % 2 == 0:
            x2 = _ffn(x2, g_ffn, dense_w1[j].astype(BF16), dense_w3[j].astype(BF16),
                      dense_w2[j].astype(BF16), final_g, tm, min(512, dense_w1.shape[2]), final_norm=last_layer)
        else:
            x2 = _moe(x2, g_ffn, moe_router[j], moe_w1[j].astype(BF16), moe_w3[j].astype(BF16),
                      moe_w2[j].astype(BF16), final_g, tm, final_norm=last_layer)

    return x2.reshape(batch, seq, d)
```

```python
import functools
import math

import numpy as np
import jax
import jax.numpy as jnp
from jax import lax
from jax.experimental import pallas as pl
from jax.experimental.pallas import tpu as pltpu
from jax.experimental.pallas import tpu_sc as plsc

F32 = jnp.float32
BF16 = jnp.bfloat16

BRANCH_WIDTH = 256
N_MIX_SLICES = 10
HEAD_WIDTH = 64
N_HEADS = 4
DIFF_QK_DIM = 32
ATT_BLOCK = 256
MOBA_TOPK = 3
CONV_WIDTH = 3
POOL_WINDOWS = (2, 4, 8, 16)
POOL_GROUP = 64
HALO = 16
REL_BUCKETS = 32
REL_MAX_DIST = 128
TOP_K_EXPERTS = 2
RMS_EPS = 1e-6
SUBLN_EPS = 1e-5
NEG_INF = -1e30
LOG2_E = math.log2(math.e)
V7X_VMEM_BYTES = 64 * 1024 * 1024
VMEM_LIMIT = V7X_VMEM_BYTES - 8 * 1024 * 1024

_TRANS_B = (((1,), (1,)), ((), ()))


def _params(*sem):
    return pltpu.CompilerParams(dimension_semantics=sem, vmem_limit_bytes=VMEM_LIMIT)


def _rms(x, g, eps):
    r = lax.rsqrt(jnp.mean(x * x, axis=-1, keepdims=True) + eps)
    return x * r * g


def _inproj_kernel(x_ref, g_ref, w_ref, o_ref):
    h = _rms(x_ref[...], g_ref[...], RMS_EPS).astype(BF16)
    o_ref[...] = jnp.dot(h, w_ref[...], preferred_element_type=F32)


def _inproj(x2, g, w, tm):
    n, d = x2.shape
    wn = w.shape[1]
    return pl.pallas_call(
        _inproj_kernel,
        grid=(n // tm,),
        in_specs=[pl.BlockSpec((tm, d), lambda i: (i, 0)),
                  pl.BlockSpec((1, d), lambda i: (0, 0)),
                  pl.BlockSpec((d, wn), lambda i: (0, 0))],
        out_specs=pl.BlockSpec((tm, wn), lambda i: (i, 0)),
        out_shape=jax.ShapeDtypeStruct((n, wn), F32),
        compiler_params=_params("parallel"),
        name="inproj",
    )(x2, g, w)


def _rel_bucket_np(dist):
    n = np.maximum(dist, 0)
    max_exact = REL_BUCKETS // 2
    nf = np.maximum(n, max_exact).astype(np.float32)
    large = max_exact + (np.log(nf / np.float32(max_exact)) / np.float32(math.log(REL_MAX_DIST / max_exact))
                         * np.float32(REL_BUCKETS - max_exact)).astype(np.int32)
    large = np.minimum(large, REL_BUCKETS - 1)
    return np.where(n < max_exact, n, large).astype(np.int32)


def _bucket_tiles():
    i = np.arange(ATT_BLOCK)[:, None]
    j = np.arange(ATT_BLOCK)[None, :]
    return np.stack([_rel_bucket_np(i - j), _rel_bucket_np(ATT_BLOCK + i - j)])


def _bias_tiles_kernel(tab_ref, bkt_ref, o_ref):
    h = pl.program_id(0)
    bkt = bkt_ref[...]
    acc = jnp.zeros(bkt.shape, F32)
    for b in range(REL_BUCKETS):
        acc = jnp.where(bkt == b, tab_ref[b, h], acc)
    o_ref[0] = (acc - tab_ref[REL_BUCKETS - 1, h]) * LOG2_E


def _bias_tiles(bias_table):
    n_heads = bias_table.shape[1]
    bkt = jnp.asarray(_bucket_tiles())
    return pl.pallas_call(
        _bias_tiles_kernel,
        grid=(n_heads,),
        in_specs=[pl.BlockSpec(memory_space=pltpu.SMEM),
                  pl.BlockSpec((2, ATT_BLOCK, ATT_BLOCK), lambda h: (0, 0, 0))],
        out_specs=pl.BlockSpec((1, 2, ATT_BLOCK, ATT_BLOCK), lambda h: (h, 0, 0, 0)),
        out_shape=jax.ShapeDtypeStruct((n_heads, 2, ATT_BLOCK, ATT_BLOCK), F32),
        compiler_params=_params("arbitrary"),
        name="bias_tiles",
    )(bias_table, bkt)


LANE_TILE = 128
SUBLANES = 8


def _head_lanes(lane, h):
    return (lane >= h * HEAD_WIDTH) & (lane < (h + 1) * HEAD_WIDTH)


def _head_to_low_lanes(x, h):
    assert 2 * HEAD_WIDTH == LANE_TILE
    tile = x[:, (h // 2) * LANE_TILE:(h // 2 + 1) * LANE_TILE]
    if h % 2:
        tile = pltpu.roll(tile, HEAD_WIDTH, axis=1)
    return jnp.where(lax.broadcasted_iota(jnp.int32, tile.shape, 1) < HEAD_WIDTH, tile, 0.0)


def _fold(op, s):
    out = s[:, :LANE_TILE]
    for c in range(LANE_TILE, s.shape[1], LANE_TILE):
        out = op(out, s[:, c:c + LANE_TILE])
    return out


def _stage_values(v_ref, vm_ref):
    t = ATT_BLOCK
    lane = lax.broadcasted_iota(jnp.int32, (t, BRANCH_WIDTH), 1)
    for j in range(vm_ref.shape[0]):
        vj = v_ref[j * t:(j + 1) * t, :]
        for h in range(N_HEADS):
            vm_ref[j, h * t:(h + 1) * t, :] = jnp.where(_head_lanes(lane, h), vj, 0.0).astype(BF16)


def _softmax_attend(qb, q16, head_of, key_slot, groups, kb_ref, vm_ref, tiles_ref, s_ref, m_ref, l_ref, acc_ref):
    t = ATT_BLOCK
    chains = range(len(q16))
    causal = (lax.broadcasted_iota(jnp.int32, (t, t), 0) >= lax.broadcasted_iota(jnp.int32, (t, t), 1))

    slots = sorted({key_slot(i) for i in chains})

    def keys(j):
        return {slot: kb_ref[slot, pl.ds(pl.multiple_of(j * t, t), t), :] for slot in slots}

    def score(i, kj):
        return lax.dot_general(q16[i], kj[key_slot(i)], _TRANS_B, preferred_element_type=F32)

    def put(i, j, s, first):
        s_ref[i, j] = s
        m_ref[i] = _fold(jnp.maximum, s) if first else jnp.maximum(m_ref[i], _fold(jnp.maximum, s))

    def in_pairs(count, fn):
        def pair(p, carry):
            fn([2 * p, 2 * p + 1])
            return carry

        lax.fori_loop(0, count >> 1, pair, 0)

        @pl.when((count & 1) == 1)
        def _():
            fn([count - 1])

    k_own = keys(qb)
    for i in chains:
        put(i, qb, jnp.where(causal, score(i, k_own) + tiles_ref[head_of(i), 0], NEG_INF), True)

    @pl.when(qb >= 1)
    def _():
        k_prev = keys(qb - 1)
        for i in chains:
            put(i, qb - 1, score(i, k_prev) + tiles_ref[head_of(i), 1], False)

    def far(js):
        ks = [keys(j) for j in js]
        for i in chains:
            ss = [score(i, kj) for kj in ks]
            for j, s in zip(js, ss):
                s_ref[i, j] = s
            m_ref[i] = functools.reduce(jnp.maximum, [m_ref[i]] + [_fold(jnp.maximum, s) for s in ss])

    in_pairs(jnp.maximum(qb - 1, 0), far)

    for i in chains:
        m_ref[i] = jnp.broadcast_to(jnp.max(m_ref[i], axis=1, keepdims=True), (t, LANE_TILE))

    def accumulate(js, first=False):
        for g, members in enumerate(groups):
            parts = [[] for _ in js]
            for i in members:
                sums = []
                for a, j in enumerate(js):
                    s = s_ref[i, j]
                    p = [jnp.exp2(s[:, c:c + LANE_TILE] - m_ref[i]) for c in range(0, t, LANE_TILE)]
                    sums += p
                    parts[a] += [x.astype(BF16) for x in p]
                row_sum = functools.reduce(jnp.add, sums)
                l_ref[i] = row_sum if first else l_ref[i] + row_sum
            lhs = jnp.concatenate([x for tile_parts in parts for x in tile_parts], axis=1)
            rhs = jnp.concatenate([vm_ref[j] for j in js], axis=0) if len(js) > 1 else vm_ref[js[0]]
            pv = jnp.dot(lhs, rhs, preferred_element_type=F32)
            acc_ref[g] = pv if first else acc_ref[g] + pv

    accumulate([qb], first=True)
    in_pairs(qb, accumulate)


def _per_head_lanes(lane, cols):
    out = jnp.broadcast_to(cols[0], lane.shape)
    for h in range(1, N_HEADS):
        out = jnp.where(_head_lanes(lane, h), cols[h], out)
    return out


def _attention_scratch(seq, n_chains, n_groups, key_slots, key_width):
    t = ATT_BLOCK
    n_blk = seq // t
    return [pltpu.VMEM((key_slots, seq, key_width), BF16),
            pltpu.VMEM((n_blk, N_HEADS * t, BRANCH_WIDTH), BF16),
            pltpu.VMEM((n_chains, n_blk, t, t), F32),
            pltpu.VMEM((n_chains, t, LANE_TILE), F32),
            pltpu.VMEM((n_chains, t, LANE_TILE), F32),
            pltpu.VMEM((n_groups, t, BRANCH_WIDTH), F32)]


def _moba_kernel(q_ref, k_ref, v_ref, tiles_ref, o_ref, kb_ref, vm_ref, s_ref, m_ref, l_ref, acc_ref, kmean_ref):
    t = ATT_BLOCK
    n_blk = k_ref.shape[0] // t
    qb = pl.program_id(1)

    heads = range(N_HEADS)
    blk_rows = kmean_ref.shape[0] // N_HEADS
    assert n_blk <= blk_rows <= LANE_TILE

    assert HEAD_WIDTH + n_blk <= LANE_TILE
    low = lax.broadcasted_iota(jnp.int32, (t, LANE_TILE), 1)

    @pl.when(qb == 0)
    def _():
        _stage_values(v_ref, vm_ref)
        ch = lax.broadcasted_iota(jnp.int32, (1, BRANCH_WIDTH), 1)
        kmean_ref[...] = jnp.zeros(kmean_ref.shape, F32)
        for j in range(n_blk):
            kj = k_ref[j * t:(j + 1) * t, :]
            mean_j = jnp.mean(kj, axis=0, keepdims=True)
            for h in heads:
                kb_ref[h, j * t:(j + 1) * t, :] = jnp.where(low == HEAD_WIDTH + j, 1.0,
                                                            _head_to_low_lanes(kj, h)).astype(BF16)
                kmean_ref[h * blk_rows + j:h * blk_rows + j + 1, :] = jnp.where(_head_lanes(ch, h), mean_j, 0.0)

    lane = lax.broadcasted_iota(jnp.int32, (t, BRANCH_WIDTH), 1)
    q = q_ref[...]
    gate = lax.dot_general(kmean_ref[...], q * (HEAD_WIDTH ** -0.5), _TRANS_B, precision=lax.Precision.HIGHEST,
                           preferred_element_type=F32)
    blk = lax.broadcasted_iota(jnp.int32, (blk_rows, t), 0).astype(F32)
    drops = []
    for h in heads:
        g = gate[h * blk_rows:(h + 1) * blk_rows, :]
        avail = blk < qb.astype(F32)
        for _ in range(MOBA_TOPK):
            best = jnp.max(jnp.where(avail, g, -jnp.inf), axis=0, keepdims=True)
            first = jnp.min(jnp.where(avail & (g == best), blk, float(blk_rows)), axis=0, keepdims=True)
            avail = avail & (blk != first)
        drops += [jnp.zeros((HEAD_WIDTH, t), F32), jnp.where(avail, NEG_INF, 0.0),
                  jnp.zeros((LANE_TILE - HEAD_WIDTH - blk_rows, t), F32)]
    eye = (lax.broadcasted_iota(jnp.int32, (t, t), 0) == lax.broadcasted_iota(jnp.int32, (t, t), 1)).astype(BF16)
    drop = lax.dot_general(eye, jnp.concatenate(drops, axis=0).astype(BF16), _TRANS_B,
                           preferred_element_type=F32).astype(BF16)
    qf = q * (HEAD_WIDTH ** -0.5 * LOG2_E)
    q_aug = [jnp.where(low < HEAD_WIDTH, _head_to_low_lanes(qf, h).astype(BF16),
                       drop[:, h * LANE_TILE:(h + 1) * LANE_TILE]) for h in heads]

    _softmax_attend(qb, q_aug, lambda i: i, lambda i: i, [list(heads)], kb_ref, vm_ref, tiles_ref,
                    s_ref, m_ref, l_ref, acc_ref)
    row_sums = _per_head_lanes(lane, [jnp.sum(l_ref[h], axis=1, keepdims=True) for h in heads])
    o_ref[...] = (acc_ref[0] / row_sums).astype(o_ref.dtype)


def _moba(u, tiles, batch, seq):
    t = ATT_BLOCK
    nq = seq // t
    return pl.pallas_call(
        _moba_kernel,
        grid=(batch, nq),
        in_specs=[pl.BlockSpec((t, BRANCH_WIDTH), lambda b, q: (b * nq + q, 0)),
                  pl.BlockSpec((seq, BRANCH_WIDTH), lambda b, q: (b, 1)),
                  pl.BlockSpec((seq, BRANCH_WIDTH), lambda b, q: (b, 2)),
                  pl.BlockSpec((N_HEADS, 2, t, t), lambda b, q: (0, 0, 0, 0))],
        out_specs=pl.BlockSpec((t, BRANCH_WIDTH), lambda b, q: (b * nq + q, 0)),
        out_shape=jax.ShapeDtypeStruct((batch * seq, BRANCH_WIDTH), BF16),
        scratch_shapes=(_attention_scratch(seq, N_HEADS, 1, N_HEADS, LANE_TILE)
                        + [pltpu.VMEM((N_HEADS * SUBLANES * pl.cdiv(nq, SUBLANES), BRANCH_WIDTH), F32)]),
        compiler_params=_params("arbitrary", "arbitrary"),
        name="moba",
    )(u, u, u, tiles)


def _diff_kernel(lam_ref, g_ref, q_ref, k_ref, v_ref, tiles_ref, o_ref, kb_ref, vm_ref, s_ref, m_ref, l_ref,
                 acc_ref, *, lam_init):
    t = ATT_BLOCK
    qb = pl.program_id(1)

    @pl.when(qb == 0)
    def _():
        _stage_values(v_ref, vm_ref)
        kb_ref[0] = k_ref[...].astype(BF16)

    lp = lam_ref[...]
    lam = (jnp.exp(jnp.sum(lp[0:1] * lp[1:2], axis=1, keepdims=True))
           - jnp.exp(jnp.sum(lp[2:3] * lp[3:4], axis=1, keepdims=True)) + lam_init)

    lane = lax.broadcasted_iota(jnp.int32, (t, BRANCH_WIDTH), 1)
    qf = q_ref[...] * (DIFF_QK_DIM ** -0.5 * LOG2_E)
    q16 = []
    for h in range(N_HEADS):
        for c in range(2):
            lo = h * HEAD_WIDTH + c * DIFF_QK_DIM
            q16.append(jnp.where((lane >= lo) & (lane < lo + DIFF_QK_DIM), qf, 0.0).astype(BF16))
    groups = [[2 * h + c for h in range(N_HEADS)] for c in range(2)]
    _softmax_attend(qb, q16, lambda i: i // 2, lambda i: 0, groups, kb_ref, vm_ref, tiles_ref,
                    s_ref, m_ref, l_ref, acc_ref)

    row_sums = [_per_head_lanes(lane, [jnp.sum(l_ref[i], axis=1, keepdims=True) for i in members])
                for members in groups]
    o = acc_ref[0] / row_sums[0] - lam * (acc_ref[1] / row_sums[1])
    sq = o * o
    mean_sq = _per_head_lanes(lane, [jnp.sum(jnp.where(_head_lanes(lane, h), sq, 0.0), axis=1, keepdims=True)
                                     for h in range(N_HEADS)]) * (1.0 / HEAD_WIDTH)
    o_ref[...] = (o * lax.rsqrt(mean_sq + SUBLN_EPS) * g_ref[...] * (1.0 - lam_init)).astype(o_ref.dtype)


def _diff(u, tiles, lam_params, subln_g4, batch, seq, lam_init):
    t = ATT_BLOCK
    nq = seq // t
    return pl.pallas_call(
        functools.partial(_diff_kernel, lam_init=lam_init),
        grid=(batch, nq),
        in_specs=[pl.BlockSpec((4, DIFF_QK_DIM), lambda b, q: (0, 0)),
                  pl.BlockSpec((1, BRANCH_WIDTH), lambda b, q: (0, 0)),
                  pl.BlockSpec((t, BRANCH_WIDTH), lambda b, q: (b * nq + q, 7)),
                  pl.BlockSpec((seq, BRANCH_WIDTH), lambda b, q: (b, 8)),
                  pl.BlockSpec((seq, BRANCH_WIDTH), lambda b, q: (b, 9)),
                  pl.BlockSpec((N_HEADS, 2, t, t), lambda b, q: (1, 0, 0, 0))],
        out_specs=pl.BlockSpec((t, BRANCH_WIDTH), lambda b, q: (b * nq + q, 0)),
        out_shape=jax.ShapeDtypeStruct((batch * seq, BRANCH_WIDTH), BF16),
        scratch_shapes=_attention_scratch(seq, 2 * N_HEADS, 2, 1, BRANCH_WIDTH),
        compiler_params=_params("arbitrary", "arbitrary"),
        name="diff_attn",
    )(lam_params, subln_g4, u, u, u, tiles)


def _conv_pool_branches(xb_ref, bb_ref, cb_ref, pc_ref, xh_ref, ch_ref, ph_ref, cw_ref, pw_ref, ps_ref,
                        u_ref, s_ref, seq):
    tm = xb_ref.shape[0]
    pos0 = (pl.program_id(0) * tm) % seq
    has_history = pos0 > 0
    pos = pos0 + lax.broadcasted_iota(jnp.int32, (tm, 1), 0)
    lane = lax.broadcasted_iota(jnp.int32, (tm, BRANCH_WIDTH), 1)

    u_ref[0:HALO, :] = jnp.where(has_history, ch_ref[...] * xh_ref[...], 0.0)
    u_ref[HALO:, :] = cb_ref[...] * xb_ref[...]
    conv = cw_ref[CONV_WIDTH - 1:CONV_WIDTH, :] * u_ref[HALO:, :]
    for i in range(CONV_WIDTH - 1):
        shift = CONV_WIDTH - 1 - i
        conv = conv + cw_ref[i:i + 1, :] * u_ref[HALO - shift:HALO - shift + tm, :]
    y_b = (bb_ref[...] * conv).astype(BF16)

    s_ref[0:HALO, :] = jnp.where(has_history, ph_ref[...], 0.0)
    s_ref[HALO:, :] = pc_ref[...]
    pooled = jnp.zeros((tm, BRANCH_WIDTH), F32)
    done = 0
    for g, w in enumerate(POOL_WINDOWS):
        half = w // 2
        cur = s_ref[done + half:, :] + s_ref[done:HALO + tm - half, :]
        done += half
        s_ref[done:, :] = cur
        cnt = jnp.minimum(pos + 1, w).astype(F32)
        mean_w = s_ref[HALO:, :] / cnt
        pooled = jnp.where((lane >= g * POOL_GROUP) & (lane < (g + 1) * POOL_GROUP), mean_w, pooled)
    pooled = pooled - pc_ref[...]
    y_c = jnp.dot(pooled.astype(BF16), pw_ref[...], preferred_element_type=F32) * ps_ref[...]
    return y_b, y_c.astype(BF16)


def _merge_kernel(x_ref, g_ref, wg_ref, ya_ref, yd_ref, xb_ref, bb_ref, cb_ref, pc_ref, xh_ref, ch_ref, ph_ref,
                  cw_ref, pw_ref, ps_ref, bp_ref, wo_ref, o_ref, u_ref, s_ref, *, seq):
    d = x_ref.shape[1]
    x = x_ref[...]
    h = _rms(x, g_ref[...], RMS_EPS).astype(BF16)
    def gated(b, y):
        gate = jax.nn.sigmoid(jnp.dot(h, wg_ref[:, b * d:(b + 1) * d], preferred_element_type=F32))
        return gate * jnp.dot(y, bp_ref[b * BRANCH_WIDTH:(b + 1) * BRANCH_WIDTH, :], preferred_element_type=F32)

    merged = gated(0, ya_ref[...]) + gated(3, yd_ref[...])
    y_b, y_c = _conv_pool_branches(xb_ref, bb_ref, cb_ref, pc_ref, xh_ref, ch_ref, ph_ref, cw_ref, pw_ref, ps_ref,
                                   u_ref, s_ref, seq)
    merged = merged + gated(1, y_b) + gated(2, y_c)
    o_ref[...] = x + jnp.dot(merged.astype(BF16), wo_ref[...], preferred_element_type=F32)


def _merge(x2, g, w_gate, y_a, y_d, u, conv_w, pool_w_bd, pool_scale, bp, w_out, seq, tm):
    n, d = x2.shape
    full = lambda a: pl.BlockSpec(a.shape, lambda i: (0,) * a.ndim)
    resident = lambda a: pl.BlockSpec(a.shape, lambda i: (0,) * a.ndim, pipeline_mode=pl.Buffered(1))
    rows = lambda a: pl.BlockSpec((tm, a.shape[1]), lambda i: (i, 0))
    u_rows = lambda c: pl.BlockSpec((tm, BRANCH_WIDTH), lambda i, c=c: (i, c))
    u_halo = lambda c: pl.BlockSpec((HALO, BRANCH_WIDTH),
                                    lambda i, c=c: (jnp.maximum(i * (tm // HALO) - 1, 0), c))
    return pl.pallas_call(
        functools.partial(_merge_kernel, seq=seq),
        grid=(n // tm,),
        in_specs=[rows(x2), full(g), resident(w_gate), rows(y_a), rows(y_d),
                  u_rows(3), u_rows(4), u_rows(5), u_rows(6), u_halo(3), u_halo(5), u_halo(6),
                  full(conv_w), full(pool_w_bd), full(pool_scale), resident(bp), resident(w_out)],
        out_specs=pl.BlockSpec((tm, d), lambda i: (i, 0)),
        out_shape=jax.ShapeDtypeStruct((n, d), F32),
        scratch_shapes=[pltpu.VMEM((HALO + tm, BRANCH_WIDTH), F32),
                        pltpu.VMEM((HALO + tm, BRANCH_WIDTH), F32)],
        compiler_params=_params("parallel"),
        name="merge",
    )(x2, g, w_gate, y_a, y_d, u, u, u, u, u, u, u, conv_w, pool_w_bd, pool_scale, bp, w_out)


ROUTE_COLS = 8


def _pack_bf16_pairs(h):
    c = h.shape[1] // 2
    bits = lax.bitcast_convert_type(h.astype(BF16).astype(F32), jnp.uint32)
    return (bits[:, :c] >> 16) | (bits[:, c:] & jnp.uint32(0xFFFF0000))


def _unpack_bf16_pairs(w):
    lo = lax.bitcast_convert_type(w << 16, F32)
    hi = lax.bitcast_convert_type(w & jnp.uint32(0xFFFF0000), F32)
    return jnp.concatenate([lo, hi], axis=1)


def _router_kernel(x_ref, g_ref, r_ref, tri_ref, idx_ref, wts_ref, hpk_ref, cnt_ref, seen_ref, *, n_exp):
    @pl.when(pl.program_id(0) == 0)
    def _():
        seen_ref[...] = jnp.zeros(seen_ref.shape, F32)

    h = _rms(x_ref[...], g_ref[...], RMS_EPS)
    logits = jnp.dot(h, r_ref[...], precision=lax.Precision.HIGHEST, preferred_element_type=F32)
    lane = lax.broadcasted_iota(jnp.int32, logits.shape, 1).astype(F32)
    logits = jnp.where(lane < n_exp, logits, -jnp.inf)
    big = float(logits.shape[1])
    m1 = jnp.max(logits, axis=1, keepdims=True)
    i1 = jnp.min(jnp.where(logits == m1, lane, big), axis=1, keepdims=True)
    rest = jnp.where(lane == i1, -jnp.inf, logits)
    m2 = jnp.max(rest, axis=1, keepdims=True)
    i2 = jnp.min(jnp.where(rest == m2, lane, big), axis=1, keepdims=True)
    e2 = jnp.exp(m2 - m1)
    w1 = 1.0 / (1.0 + e2)
    w2 = e2 / (1.0 + e2)

    chosen = (lane == i1) | (lane == i2)
    before = seen_ref[...] + jnp.dot(tri_ref[...], chosen.astype(BF16), preferred_element_type=F32)
    r1 = jnp.sum(jnp.where(lane == i1, before, 0.0), axis=1, keepdims=True)
    r2 = jnp.sum(jnp.where(lane == i2, before, 0.0), axis=1, keepdims=True)
    seen_ref[...] += jnp.sum(chosen.astype(F32), axis=0, keepdims=True)
    cnt_ref[...] = seen_ref[...]

    pick = lambda col, a, b, c, d: jnp.where(col == 0, a, jnp.where(col == 1, b, jnp.where(col == 2, c, d)))
    record = pick(lane, i1, i2, r1, jnp.where(lane == 3, r2, 0.0))
    idx_ref[...] = record.T[0:idx_ref.shape[0], :].astype(jnp.int32)
    wts_ref[...] = pick(lax.broadcasted_iota(jnp.int32, wts_ref.shape, 1), w1, w2, 0.0, 0.0)
    hpk_ref[...] = _pack_bf16_pairs(h)


def _router(x2, g, router_pad, n_exp, tm):
    n, d = x2.shape
    tri = jnp.asarray(np.tril(np.ones((tm, tm), np.float32), -1), BF16)
    full = lambda a: pl.BlockSpec(a.shape, lambda i: (0,) * a.ndim)
    rows = lambda w: pl.BlockSpec((tm, w), lambda i: (i, 0))
    return pl.pallas_call(
        functools.partial(_router_kernel, n_exp=n_exp),
        grid=(n // tm,),
        in_specs=[rows(d), full(g), full(router_pad), full(tri)],
        out_specs=[pl.BlockSpec((ROUTE_COLS, tm), lambda i: (0, i)), rows(ROUTE_COLS), rows(d // 2),
                   pl.BlockSpec((1, router_pad.shape[1]), lambda i: (0, 0))],
        out_shape=[jax.ShapeDtypeStruct((ROUTE_COLS, n), jnp.int32),
                   jax.ShapeDtypeStruct((n, ROUTE_COLS), F32),
                   jax.ShapeDtypeStruct((n, d // 2), jnp.uint32),
                   jax.ShapeDtypeStruct((1, router_pad.shape[1]), F32)],
        scratch_shapes=[pltpu.VMEM((1, router_pad.shape[1]), F32)],
        compiler_params=_params("arbitrary"),
        name="router",
    )(x2, g, router_pad, tri)


V7X_SC_CORES = 2
V7X_SC_SUBCORES = 16
SC_GATHER_BYTES = 256 * 1024
SC_MAX_INDEX_VECTOR = 128


def _sc_gather(table, idx):
    n_rows, width = idx.shape[0], table.shape[1]
    workers = V7X_SC_CORES * V7X_SC_SUBCORES
    chunk = min(SC_MAX_INDEX_VECTOR, SC_GATHER_BYTES // (width * table.dtype.itemsize))
    assert table.dtype.itemsize == 4 and n_rows % (workers * chunk) == 0 and chunk % 8 == 0
    per_worker = n_rows // workers
    mesh = plsc.VectorSubcoreMesh(core_axis_name="c", subcore_axis_name="s",
                                  num_cores=V7X_SC_CORES, num_subcores=V7X_SC_SUBCORES)

    def body(table_hbm, idx_hbm, out_hbm, idx_v, rows_v, sem):
        base = (lax.axis_index("s") * V7X_SC_CORES + lax.axis_index("c")) * per_worker

        @pl.loop(0, per_worker // chunk)
        def _(i):
            off = base + i * chunk
            pltpu.sync_copy(idx_hbm.at[pl.ds(off, chunk)], idx_v)
            pltpu.async_copy(table_hbm.at[idx_v], rows_v, sem).wait()
            pltpu.sync_copy(rows_v, out_hbm.at[pl.ds(off, chunk)])

    return pl.kernel(
        body,
        out_type=jax.ShapeDtypeStruct((n_rows, width), table.dtype),
        mesh=mesh,
        scratch_types=[pltpu.VMEM((chunk,), jnp.int32), pltpu.VMEM((chunk, width), table.dtype),
                       pltpu.SemaphoreType.DMA],
        name="sc_gather",
    )(table, idx)


def _sc_scatter_pairs(table, pos):
    n, width = table.shape
    workers = V7X_SC_CORES * V7X_SC_SUBCORES
    chunk = min(SC_MAX_INDEX_VECTOR, SC_GATHER_BYTES // (width * table.dtype.itemsize))
    assert table.dtype.itemsize == 4 and n % (workers * chunk) == 0 and chunk % 8 == 0
    per_worker = n // workers
    mesh = plsc.VectorSubcoreMesh(core_axis_name="c", subcore_axis_name="s",
                                  num_cores=V7X_SC_CORES, num_subcores=V7X_SC_SUBCORES)

    def body(table_hbm, pos_hbm, out_hbm, idx_a, idx_b, rows_v, sem):
        base = (lax.axis_index("s") * V7X_SC_CORES + lax.axis_index("c")) * per_worker

        @pl.loop(0, per_worker // chunk)
        def _(i):
            off = base + i * chunk
            pltpu.sync_copy(table_hbm.at[pl.ds(off, chunk)], rows_v)
            pltpu.sync_copy(pos_hbm.at[pl.ds(off, chunk)], idx_a)
            pltpu.sync_copy(pos_hbm.at[pl.ds(n + off, chunk)], idx_b)
            pltpu.async_copy(rows_v, out_hbm.at[idx_a], sem).wait()
            pltpu.async_copy(rows_v, out_hbm.at[idx_b], sem).wait()

    return pl.kernel(
        body,
        out_type=jax.ShapeDtypeStruct((2 * n, width), table.dtype),
        mesh=mesh,
        scratch_types=[pltpu.VMEM((chunk,), jnp.int32), pltpu.VMEM((chunk,), jnp.int32),
                       pltpu.VMEM((chunk, width), table.dtype), pltpu.SemaphoreType.DMA],
        name="sc_scatter_pairs",
    )(table, pos)


def _swiglu_chunk(h, w1, w3, w2):
    a = jnp.dot(h, w1, preferred_element_type=F32)
    b = jnp.dot(h, w3, preferred_element_type=F32)
    t = (a * jax.nn.sigmoid(a) * b).astype(BF16)
    return jnp.dot(t, w2, preferred_element_type=F32)


def _ffn_kernel(x_ref, g_ref, w1_ref, w3_ref, w2_ref, fg_ref, o_ref, *, fc, final_norm):
    d_ff = w1_ref.shape[1]
    x = x_ref[...]
    h = _rms(x, g_ref[...], RMS_EPS).astype(BF16)
    acc = None
    for c0 in range(0, d_ff, fc):
        c1 = min(c0 + fc, d_ff)
        y = _swiglu_chunk(h, w1_ref[:, c0:c1], w3_ref[:, c0:c1], w2_ref[c0:c1, :])
        acc = y if acc is None else acc + y
    out = x + acc
    if final_norm:
        out = _rms(out, fg_ref[...], RMS_EPS)
    o_ref[...] = out


def _ffn(x2, g, w1, w3, w2, final_g, tm, fc, final_norm):
    n, d = x2.shape
    resident = lambda a: pl.BlockSpec(a.shape, lambda i: (0,) * a.ndim, pipeline_mode=pl.Buffered(1))
    return pl.pallas_call(
        functools.partial(_ffn_kernel, fc=fc, final_norm=final_norm),
        grid=(n // tm,),
        in_specs=[pl.BlockSpec((tm, d), lambda i: (i, 0)),
                  pl.BlockSpec((1, d), lambda i: (0, 0)),
                  resident(w1), resident(w3), resident(w2),
                  pl.BlockSpec((1, d), lambda i: (0, 0))],
        out_specs=pl.BlockSpec((tm, d), lambda i: (i, 0)),
        out_shape=jax.ShapeDtypeStruct((n, d), F32),
        compiler_params=_params("parallel"),
        name="dense_ffn",
    )(x2, g, w1, w3, w2, final_g)


EXPERT_SUBCHUNK = 512


def _expert_ffn_kernel(tile_ref, exp_ref, lo_ref, hi_ref, xs_ref, w1_ref, w3_ref, w2_ref, o_ref, h_ref, acc_ref):
    it = pl.program_id(0)
    c = pl.program_id(1)
    rows = o_ref.shape[0]
    lo, hi = lo_ref[it], hi_ref[it]

    @pl.when(hi > lo)
    def _():
        @pl.when(c == 0)
        def _():
            h_ref[...] = _unpack_bf16_pairs(xs_ref[...]).astype(BF16)

        h = h_ref[...]
        fc = w1_ref.shape[2]
        y = None
        for c0 in range(0, fc, EXPERT_SUBCHUNK):
            c1 = min(c0 + EXPERT_SUBCHUNK, fc)
            part = _swiglu_chunk(h, w1_ref[0, :, c0:c1], w3_ref[0, :, c0:c1], w2_ref[0, c0:c1, :])
            y = part if y is None else y + part

        @pl.when(c == 0)
        def _():
            acc_ref[...] = y

        @pl.when(c > 0)
        def _():
            acc_ref[...] += y

        @pl.when(c == pl.num_programs(1) - 1)
        def _():
            first_row = lo - tile_ref[it] * rows
            packed = _pack_bf16_pairs(acc_ref[...])

            @pl.when(first_row == 0)
            def _():
                o_ref[...] = packed

            @pl.when(first_row > 0)
            def _():
                row = lax.broadcasted_iota(jnp.int32, o_ref.shape, 0)
                o_ref[...] = jnp.where(row >= first_row, packed, o_ref[...])


def _expert_ffn(xs, items, w1, w3, w2, rows, fc):
    n_pairs, half = xs.shape
    d = 2 * half
    d_ff = w1.shape[2]
    item_tile, item_expert, item_lo, item_hi = items
    grid_spec = pltpu.PrefetchScalarGridSpec(
        num_scalar_prefetch=4,
        grid=(item_tile.shape[0], d_ff // fc),
        in_specs=[pl.BlockSpec((rows, half), lambda i, c, t, e, lo, hi: (t[i], 0)),
                  pl.BlockSpec((1, d, fc), lambda i, c, t, e, lo, hi: (e[i], 0, c)),
                  pl.BlockSpec((1, d, fc), lambda i, c, t, e, lo, hi: (e[i], 0, c)),
                  pl.BlockSpec((1, fc, d), lambda i, c, t, e, lo, hi: (e[i], c, 0))],
        out_specs=pl.BlockSpec((rows, half), lambda i, c, t, e, lo, hi: (t[i], 0)),
        scratch_shapes=[pltpu.VMEM((rows, d), BF16), pltpu.VMEM((rows, d), F32)],
    )
    return pl.pallas_call(
        _expert_ffn_kernel,
        grid_spec=grid_spec,
        out_shape=jax.ShapeDtypeStruct((n_pairs, half), jnp.uint32),
        compiler_params=_params("arbitrary", "arbitrary"),
        name="expert_ffn",
    )(item_tile, item_expert, item_lo, item_hi, xs, w1, w3, w2)


def _work_items(counts, n_pairs, rows):
    n_exp = counts.shape[0]
    n_tiles = n_pairs // rows
    ends = jnp.cumsum(counts)
    starts = ends - counts
    lo = jnp.sort(jnp.concatenate([jnp.arange(n_tiles, dtype=jnp.int32) * rows, starts[1:]]))
    hi = jnp.concatenate([lo[1:], jnp.full((1,), n_pairs, jnp.int32)])
    tile = jnp.minimum(lo // rows, n_tiles - 1)
    expert = jnp.minimum(jnp.searchsorted(ends, lo, side="right").astype(jnp.int32), n_exp - 1)
    return (tile, expert, lo, hi), starts


def _combine_kernel(x_ref, y1_ref, y2_ref, w_ref, fg_ref, o_ref, *, final_norm):
    w = w_ref[...]
    out = x_ref[...] + (w[:, 0:1] * _unpack_bf16_pairs(y1_ref[...]) + w[:, 1:2] * _unpack_bf16_pairs(y2_ref[...]))
    if final_norm:
        out = _rms(out, fg_ref[...], RMS_EPS)
    o_ref[...] = out


def _combine(x2, y_pairs, wts, final_g, tm, final_norm):
    n, d = x2.shape
    nt = n // tm
    return pl.pallas_call(
        functools.partial(_combine_kernel, final_norm=final_norm),
        grid=(nt,),
        in_specs=[pl.BlockSpec((tm, d), lambda i: (i, 0)),
                  pl.BlockSpec((tm, d // 2), lambda i: (i, 0)),
                  pl.BlockSpec((tm, d // 2), lambda i: (i + nt, 0)),
                  pl.BlockSpec((tm, wts.shape[1]), lambda i: (i, 0)),
                  pl.BlockSpec((1, d), lambda i: (0, 0))],
        out_specs=pl.BlockSpec((tm, d), lambda i: (i, 0)),
        out_shape=jax.ShapeDtypeStruct((n, d), F32),
        compiler_params=_params("parallel"),
        name="moe_combine",
    )(x2, y_pairs, y_pairs, wts, final_g)


def _moe(x2, g_ffn, router, w1, w3, w2, final_g, tm, final_norm):
    n, d = x2.shape
    n_exp = router.shape[1]
    router_pad = jnp.pad(router, ((0, 0), (0, 128 - n_exp)))
    idx, wts, h_packed, seen = _router(x2, g_ffn, router_pad, n_exp, tm)

    rows = _pick_tile(2 * n, 512)
    counts = seen[0, :n_exp].astype(jnp.int32)
    items, starts = _work_items(counts, 2 * n, rows)
    pos = jnp.concatenate([starts[idx[0]] + idx[2], starts[idx[1]] + idx[3]])

    xs = _sc_scatter_pairs(h_packed, pos)
    ys = _expert_ffn(xs, items, w1, w3, w2, rows, _pick_tile(w1.shape[2], 1792))
    y_pairs = _sc_gather(ys, pos)
    return _combine(x2, y_pairs, wts, final_g, tm, final_norm)


def _block_diag(w):
    g, c, _ = w.shape
    eye = jnp.eye(g, dtype=w.dtype)
    return (eye[:, None, :, None] * w[:, :, None, :]).reshape(g * c, g * c)


def _pick_tile(n, target):
    t = min(n, target)
    while n % t:
        t //= 2
    return t


def kernel(x, bias_table, mix_norm_g, w_in, conv_w, pool_w, pool_scale, diff_lambda, diff_subln_g,
           branch_proj, w_out, ffn_norm_g, dense_w1, dense_w3, dense_w2, moe_router, moe_w1, moe_w3,
           moe_w2, final_norm_g):
    batch, seq, d = x.shape
    depth = w_in.shape[0]
    n_mix = N_MIX_SLICES * BRANCH_WIDTH
    n = batch * seq
    assert seq % ATT_BLOCK == 0 and d % 128 == 0
    tm = _pick_tile(seq, 512)

    x2 = x.reshape(n, d)
    tiles = _bias_tiles(bias_table)
    row = lambda v: v.reshape(1, -1)
    final_g = row(final_norm_g)

    for i in range(depth):
        last_layer = i == depth - 1
        g_mix = row(mix_norm_g[i])
        w_mix = w_in[i, :, :n_mix].astype(BF16)
        w_gate = w_in[i, :, n_mix:].astype(BF16)
        u = _inproj(x2, g_mix, w_mix, tm)

        lam_init = 0.8 - 0.6 * math.exp(-0.3 * i)
        y_a = _moba(u, tiles, batch, seq)
        y_d = _diff(u, tiles, diff_lambda[i], row(jnp.tile(diff_subln_g[i], N_HEADS)), batch, seq, lam_init)
        x2 = _merge(x2, g_mix, w_gate, y_a, y_d, u, conv_w[i], _block_diag(pool_w[i]).astype(BF16),
                    row(pool_scale[i]), branch_proj[i].reshape(-1, d).astype(BF16), w_out[i].astype(BF16),
                    seq, tm)

        g_ffn = row(ffn_norm_g[i])
        j = i // 2
        if i % 2 == 0:
            x2 = _ffn(x2, g_ffn, dense_w1[j].astype(BF16), dense_w3[j].astype(BF16),
                      dense_w2[j].astype(BF16), final_g, tm, min(512, dense_w1.shape[2]), final_norm=last_layer)
        else:
            x2 = _moe(x2, g_ffn, moe_router[j], moe_w1[j].astype(BF16), moe_w3[j].astype(BF16),
                      moe_w2[j].astype(BF16), final_g, tm, final_norm=last_layer)

    return x2.reshape(batch, seq, d)
```

```python
import functools
import math

import numpy as np
import jax
import jax.numpy as jnp
from jax import lax
from jax.experimental import pallas as pl
from jax.experimental.pallas import tpu as pltpu
from jax.experimental.pallas import tpu_sc as plsc

F32 = jnp.float32
BF16 = jnp.bfloat16

BRANCH_WIDTH = 256
N_MIX_SLICES = 10
HEAD_WIDTH = 64
N_HEADS = 4
DIFF_QK_DIM = 32
ATT_BLOCK = 256
MOBA_TOPK = 3
CONV_WIDTH = 3
POOL_WINDOWS = (2, 4, 8, 16)
POOL_GROUP = 64
HALO = 16
REL_BUCKETS = 32
REL_MAX_DIST = 128
TOP_K_EXPERTS = 2
RMS_EPS = 1e-6
SUBLN_EPS = 1e-5
NEG_INF = -1e30
LOG2_E = math.log2(math.e)
V7X_VMEM_BYTES = 64 * 1024 * 1024
VMEM_LIMIT = V7X_VMEM_BYTES - 8 * 1024 * 1024

_TRANS_B = (((1,), (1,)), ((), ()))


def _params(*sem):
    return pltpu.CompilerParams(dimension_semantics=sem, vmem_limit_bytes=VMEM_LIMIT)


def _rms(x, g, eps):
    r = lax.rsqrt(jnp.mean(x * x, axis=-1, keepdims=True) + eps)
    return x * r * g


def _inproj_kernel(x_ref, g_ref, w_ref, o_ref):
    h = _rms(x_ref[...], g_ref[...], RMS_EPS).astype(BF16)
    o_ref[...] = jnp.dot(h, w_ref[...], preferred_element_type=F32)


def _inproj(x2, g, w, tm):
    n, d = x2.shape
    wn = w.shape[1]
    return pl.pallas_call(
        _inproj_kernel,
        grid=(n // tm,),
        in_specs=[pl.BlockSpec((tm, d), lambda i: (i, 0)),
                  pl.BlockSpec((1, d), lambda i: (0, 0)),
                  pl.BlockSpec((d, wn), lambda i: (0, 0))],
        out_specs=pl.BlockSpec((tm, wn), lambda i: (i, 0)),
        out_shape=jax.ShapeDtypeStruct((n, wn), F32),
        compiler_params=_params("parallel"),
        name="inproj",
    )(x2, g, w)


def _rel_bucket_np(dist):
    n = np.maximum(dist, 0)
    max_exact = REL_BUCKETS // 2
    nf = np.maximum(n, max_exact).astype(np.float32)
    large = max_exact + (np.log(nf / np.float32(max_exact)) / np.float32(math.log(REL_MAX_DIST / max_exact))
                         * np.float32(REL_BUCKETS - max_exact)).astype(np.int32)
    large = np.minimum(large, REL_BUCKETS - 1)
    return np.where(n < max_exact, n, large).astype(np.int32)


def _bucket_tiles():
    i = np.arange(ATT_BLOCK)[:, None]
    j = np.arange(ATT_BLOCK)[None, :]
    return np.stack([_rel_bucket_np(i - j), _rel_bucket_np(ATT_BLOCK + i - j)])


def _bias_tiles_kernel(tab_ref, bkt_ref, o_ref):
    h = pl.program_id(0)
    bkt = bkt_ref[...]
    acc = jnp.zeros(bkt.shape, F32)
    for b in range(REL_BUCKETS):
        acc = jnp.where(bkt == b, tab_ref[b, h], acc)
    o_ref[0] = (acc - tab_ref[REL_BUCKETS - 1, h]) * LOG2_E


def _bias_tiles(bias_table):
    n_heads = bias_table.shape[1]
    bkt = jnp.asarray(_bucket_tiles())
    return pl.pallas_call(
        _bias_tiles_kernel,
        grid=(n_heads,),
        in_specs=[pl.BlockSpec(memory_space=pltpu.SMEM),
                  pl.BlockSpec((2, ATT_BLOCK, ATT_BLOCK), lambda h: (0, 0, 0))],
        out_specs=pl.BlockSpec((1, 2, ATT_BLOCK, ATT_BLOCK), lambda h: (h, 0, 0, 0)),
        out_shape=jax.ShapeDtypeStruct((n_heads, 2, ATT_BLOCK, ATT_BLOCK), F32),
        compiler_params=_params("arbitrary"),
        name="bias_tiles",
    )(bias_table, bkt)


LANE_TILE = 128
SUBLANES = 8


def _head_lanes(lane, h):
    return (lane >= h * HEAD_WIDTH) & (lane < (h + 1) * HEAD_WIDTH)


def _head_to_low_lanes(x, h):
    assert 2 * HEAD_WIDTH == LANE_TILE
    tile = x[:, (h // 2) * LANE_TILE:(h // 2 + 1) * LANE_TILE]
    if h % 2:
        tile = pltpu.roll(tile, HEAD_WIDTH, axis=1)
    return jnp.where(lax.broadcasted_iota(jnp.int32, tile.shape, 1) < HEAD_WIDTH, tile, 0.0)


def _fold(op, s):
    out = s[:, :LANE_TILE]
    for c in range(LANE_TILE, s.shape[1], LANE_TILE):
        out = op(out, s[:, c:c + LANE_TILE])
    return out


def _stage_values(v_ref, vm_ref):
    t = ATT_BLOCK
    lane = lax.broadcasted_iota(jnp.int32, (t, BRANCH_WIDTH), 1)
    for j in range(vm_ref.shape[0]):
        vj = v_ref[j * t:(j + 1) * t, :]
        for h in range(N_HEADS):
            vm_ref[j, h * t:(h + 1) * t, :] = jnp.where(_head_lanes(lane, h), vj, 0.0).astype(BF16)


def _softmax_attend(qb, q16, head_of, key_slot, groups, kb_ref, vm_ref, tiles_ref, s_ref, m_ref, l_ref, acc_ref):
    t = ATT_BLOCK
    chains = range(len(q16))
    causal = (lax.broadcasted_iota(jnp.int32, (t, t), 0) >= lax.broadcasted_iota(jnp.int32, (t, t), 1))

    slots = sorted({key_slot(i) for i in chains})

    def keys(j):
        return {slot: kb_ref[slot, pl.ds(pl.multiple_of(j * t, t), t), :] for slot in slots}

    def score(i, kj):
        return lax.dot_general(q16[i], kj[key_slot(i)], _TRANS_B, preferred_element_type=F32)

    def put(i, j, s, first):
        s_ref[i, j] = s
        m_ref[i] = _fold(jnp.maximum, s) if first else jnp.maximum(m_ref[i], _fold(jnp.maximum, s))

    def in_pairs(count, fn):
        def pair(p, carry):
            fn([2 * p, 2 * p + 1])
            return carry

        lax.fori_loop(0, count >> 1, pair, 0)

        @pl.when((count & 1) == 1)
        def _():
            fn([count - 1])

    k_own = keys(qb)
    for i in chains:
        put(i, qb, jnp.where(causal, score(i, k_own) + tiles_ref[head_of(i), 0], NEG_INF), True)

    @pl.when(qb >= 1)
    def _():
        k_prev = keys(qb - 1)
        for i in chains:
            put(i, qb - 1, score(i, k_prev) + tiles_ref[head_of(i), 1], False)

    def far(js):
        ks = [keys(j) for j in js]
        for i in chains:
            ss = [score(i, kj) for kj in ks]
            for j, s in zip(js, ss):
                s_ref[i, j] = s
            m_ref[i] = functools.reduce(jnp.maximum, [m_ref[i]] + [_fold(jnp.maximum, s) for s in ss])

    in_pairs(jnp.maximum(qb - 1, 0), far)

    for i in chains:
        m_ref[i] = jnp.broadcast_to(jnp.max(m_ref[i], axis=1, keepdims=True), (t, LANE_TILE))

    def accumulate(js, first=False):
        for g, members in enumerate(groups):
            parts = [[] for _ in js]
            for i in members:
                sums = []
                for a, j in enumerate(js):
                    s = s_ref[i, j]
                    p = [jnp.exp2(s[:, c:c + LANE_TILE] - m_ref[i]) for c in range(0, t, LANE_TILE)]
                    sums += p
                    parts[a] += [x.astype(BF16) for x in p]
                row_sum = functools.reduce(jnp.add, sums)
                l_ref[i] = row_sum if first else l_ref[i] + row_sum
            lhs = jnp.concatenate([x for tile_parts in parts for x in tile_parts], axis=1)
            rhs = jnp.concatenate([vm_ref[j] for j in js], axis=0) if len(js) > 1 else vm_ref[js[0]]
            pv = jnp.dot(lhs, rhs, preferred_element_type=F32)
            acc_ref[g] = pv if first else acc_ref[g] + pv

    accumulate([qb], first=True)
    in_pairs(qb, accumulate)


def _per_head_lanes(lane, cols):
    out = jnp.broadcast_to(cols[0], lane.shape)
    for h in range(1, N_HEADS):
        out = jnp.where(_head_lanes(lane, h), cols[h], out)
    return out


def _attention_scratch(seq, n_chains, n_groups, key_slots, key_width):
    t = ATT_BLOCK
    n_blk = seq // t
    return [pltpu.VMEM((key_slots, seq, key_width), BF16),
            pltpu.VMEM((n_blk, N_HEADS * t, BRANCH_WIDTH), BF16),
            pltpu.VMEM((n_chains, n_blk, t, t), F32),
            pltpu.VMEM((n_chains, t, LANE_TILE), F32),
            pltpu.VMEM((n_chains, t, LANE_TILE), F32),
            pltpu.VMEM((n_groups, t, BRANCH_WIDTH), F32)]


def _moba_kernel(q_ref, k_ref, v_ref, tiles_ref, o_ref, kb_ref, vm_ref, s_ref, m_ref, l_ref, acc_ref, kmean_ref):
    t = ATT_BLOCK
    n_blk = k_ref.shape[0] // t
    qb = pl.program_id(1)

    heads = range(N_HEADS)
    blk_rows = kmean_ref.shape[0] // N_HEADS
    assert n_blk <= blk_rows <= LANE_TILE

    assert HEAD_WIDTH + n_blk <= LANE_TILE
    low = lax.broadcasted_iota(jnp.int32, (t, LANE_TILE), 1)

    @pl.when(qb == 0)
    def _():
        _stage_values(v_ref, vm_ref)
        ch = lax.broadcasted_iota(jnp.int32, (1, BRANCH_WIDTH), 1)
        kmean_ref[...] = jnp.zeros(kmean_ref.shape, F32)
        for j in range(n_blk):
            kj = k_ref[j * t:(j + 1) * t, :]
            mean_j = jnp.mean(kj, axis=0, keepdims=True)
            for h in heads:
                kb_ref[h, j * t:(j + 1) * t, :] = jnp.where(low == HEAD_WIDTH + j, 1.0,
                                                            _head_to_low_lanes(kj, h)).astype(BF16)
                kmean_ref[h * blk_rows + j:h * blk_rows + j + 1, :] = jnp.where(_head_lanes(ch, h), mean_j, 0.0)

    lane = lax.broadcasted_iota(jnp.int32, (t, BRANCH_WIDTH), 1)
    q = q_ref[...]
    gate = lax.dot_general(kmean_ref[...], q * (HEAD_WIDTH ** -0.5), _TRANS_B, precision=lax.Precision.HIGHEST,
                           preferred_element_type=F32)
    blk = lax.broadcasted_iota(jnp.int32, (blk_rows, t), 0).astype(F32)
    drops = []
    for h in heads:
        g = gate[h * blk_rows:(h + 1) * blk_rows, :]
        avail = blk < qb.astype(F32)
        for _ in range(MOBA_TOPK):
            best = jnp.max(jnp.where(avail, g, -jnp.inf), axis=0, keepdims=True)
            first = jnp.min(jnp.where(avail & (g == best), blk, float(blk_rows)), axis=0, keepdims=True)
            avail = avail & (blk != first)
        drops += [jnp.zeros((HEAD_WIDTH, t), F32), jnp.where(avail, NEG_INF, 0.0),
                  jnp.zeros((LANE_TILE - HEAD_WIDTH - blk_rows, t), F32)]
    eye = (lax.broadcasted_iota(jnp.int32, (t, t), 0) == lax.broadcasted_iota(jnp.int32, (t, t), 1)).astype(BF16)
    drop = lax.dot_general(eye, jnp.concatenate(drops, axis=0).astype(BF16), _TRANS_B,
                           preferred_element_type=F32).astype(BF16)
    qf = q * (HEAD_WIDTH ** -0.5 * LOG2_E)
    q_aug = [jnp.where(low < HEAD_WIDTH, _head_to_low_lanes(qf, h).astype(BF16),
                       drop[:, h * LANE_TILE:(h + 1) * LANE_TILE]) for h in heads]

    _softmax_attend(qb, q_aug, lambda i: i, lambda i: i, [list(heads)], kb_ref, vm_ref, tiles_ref,
                    s_ref, m_ref, l_ref, acc_ref)
    row_sums = _per_head_lanes(lane, [jnp.sum(l_ref[h], axis=1, keepdims=True) for h in heads])
    o_ref[...] = (acc_ref[0] / row_sums).astype(o_ref.dtype)


def _moba(u, tiles, batch, seq):
    t = ATT_BLOCK
    nq = seq // t
    return pl.pallas_call(
        _moba_kernel,
        grid=(batch, nq),
        in_specs=[pl.BlockSpec((t, BRANCH_WIDTH), lambda b, q: (b * nq + q, 0)),
                  pl.BlockSpec((seq, BRANCH_WIDTH), lambda b, q: (b, 1)),
                  pl.BlockSpec((seq, BRANCH_WIDTH), lambda b, q: (b, 2)),
                  pl.BlockSpec((N_HEADS, 2, t, t), lambda b, q: (0, 0, 0, 0))],
        out_specs=pl.BlockSpec((t, BRANCH_WIDTH), lambda b, q: (b * nq + q, 0)),
        out_shape=jax.ShapeDtypeStruct((batch * seq, BRANCH_WIDTH), BF16),
        scratch_shapes=(_attention_scratch(seq, N_HEADS, 1, N_HEADS, LANE_TILE)
                        + [pltpu.VMEM((N_HEADS * SUBLANES * pl.cdiv(nq, SUBLANES), BRANCH_WIDTH), F32)]),
        compiler_params=_params("arbitrary", "arbitrary"),
        name="moba",
    )(u, u, u, tiles)


def _diff_kernel(lam_ref, g_ref, q_ref, k_ref, v_ref, tiles_ref, o_ref, kb_ref, vm_ref, s_ref, m_ref, l_ref,
                 acc_ref, *, lam_init):
    t = ATT_BLOCK
    qb = pl.program_id(1)

    @pl.when(qb == 0)
    def _():
        _stage_values(v_ref, vm_ref)
        kb_ref[0] = k_ref[...].astype(BF16)

    lp = lam_ref[...]
    lam = (jnp.exp(jnp.sum(lp[0:1] * lp[1:2], axis=1, keepdims=True))
           - jnp.exp(jnp.sum(lp[2:3] * lp[3:4], axis=1, keepdims=True)) + lam_init)

    lane = lax.broadcasted_iota(jnp.int32, (t, BRANCH_WIDTH), 1)
    qf = q_ref[...] * (DIFF_QK_DIM ** -0.5 * LOG2_E)
    q16 = []
    for h in range(N_HEADS):
        for c in range(2):
            lo = h * HEAD_WIDTH + c * DIFF_QK_DIM
            q16.append(jnp.where((lane >= lo) & (lane < lo + DIFF_QK_DIM), qf, 0.0).astype(BF16))
    groups = [[2 * h + c for h in range(N_HEADS)] for c in range(2)]
    _softmax_attend(qb, q16, lambda i: i // 2, lambda i: 0, groups, kb_ref, vm_ref, tiles_ref,
                    s_ref, m_ref, l_ref, acc_ref)

    row_sums = [_per_head_lanes(lane, [jnp.sum(l_ref[i], axis=1, keepdims=True) for i in members])
                for members in groups]
    o = acc_ref[0] / row_sums[0] - lam * (acc_ref[1] / row_sums[1])
    sq = o * o
    mean_sq = _per_head_lanes(lane, [jnp.sum(jnp.where(_head_lanes(lane, h), sq, 0.0), axis=1, keepdims=True)
                                     for h in range(N_HEADS)]) * (1.0 / HEAD_WIDTH)
    o_ref[...] = (o * lax.rsqrt(mean_sq + SUBLN_EPS) * g_ref[...] * (1.0 - lam_init)).astype(o_ref.dtype)


def _diff(u, tiles, lam_params, subln_g4, batch, seq, lam_init):
    t = ATT_BLOCK
    nq = seq // t
    return pl.pallas_call(
        functools.partial(_diff_kernel, lam_init=lam_init),
        grid=(batch, nq),
        in_specs=[pl.BlockSpec((4, DIFF_QK_DIM), lambda b, q: (0, 0)),
                  pl.BlockSpec((1, BRANCH_WIDTH), lambda b, q: (0, 0)),
                  pl.BlockSpec((t, BRANCH_WIDTH), lambda b, q: (b * nq + q, 7)),
                  pl.BlockSpec((seq, BRANCH_WIDTH), lambda b, q: (b, 8)),
                  pl.BlockSpec((seq, BRANCH_WIDTH), lambda b, q: (b, 9)),
                  pl.BlockSpec((N_HEADS, 2, t, t), lambda b, q: (1, 0, 0, 0))],
        out_specs=pl.BlockSpec((t, BRANCH_WIDTH), lambda b, q: (b * nq + q, 0)),
        out_shape=jax.ShapeDtypeStruct((batch * seq, BRANCH_WIDTH), BF16),
        scratch_shapes=_attention_scratch(seq, 2 * N_HEADS, 2, 1, BRANCH_WIDTH),
        compiler_params=_params("arbitrary", "arbitrary"),
        name="diff_attn",
    )(lam_params, subln_g4, u, u, u, tiles)


def _conv_pool_branches(xb_ref, bb_ref, cb_ref, pc_ref, xh_ref, ch_ref, ph_ref, cw_ref, pw_ref, ps_ref,
                        u_ref, s_ref, seq):
    tm = xb_ref.shape[0]
    pos0 = (pl.program_id(0) * tm) % seq
    has_history = pos0 > 0
    pos = pos0 + lax.broadcasted_iota(jnp.int32, (tm, 1), 0)
    lane = lax.broadcasted_iota(jnp.int32, (tm, BRANCH_WIDTH), 1)

    u_ref[0:HALO, :] = jnp.where(has_history, ch_ref[...] * xh_ref[...], 0.0)
    u_ref[HALO:, :] = cb_ref[...] * xb_ref[...]
    conv = cw_ref[CONV_WIDTH - 1:CONV_WIDTH, :] * u_ref[HALO:, :]
    for i in range(CONV_WIDTH - 1):
        shift = CONV_WIDTH - 1 - i
        conv = conv + cw_ref[i:i + 1, :] * u_ref[HALO - shift:HALO - shift + tm, :]
    y_b = (bb_ref[...] * conv).astype(BF16)

    s_ref[0:HALO, :] = jnp.where(has_history, ph_ref[...], 0.0)
    s_ref[HALO:, :] = pc_ref[...]
    pooled = jnp.zeros((tm, BRANCH_WIDTH), F32)
    done = 0
    for g, w in enumerate(POOL_WINDOWS):
        half = w // 2
        cur = s_ref[done + half:, :] + s_ref[done:HALO + tm - half, :]
        done += half
        s_ref[done:, :] = cur
        cnt = jnp.minimum(pos + 1, w).astype(F32)
        mean_w = s_ref[HALO:, :] / cnt
        pooled = jnp.where((lane >= g * POOL_GROUP) & (lane < (g + 1) * POOL_GROUP), mean_w, pooled)
    pooled = pooled - pc_ref[...]
    y_c = jnp.dot(pooled.astype(BF16), pw_ref[...], preferred_element_type=F32) * ps_ref[...]
    return y_b, y_c.astype(BF16)


def _merge_kernel(x_ref, g_ref, wg_ref, ya_ref, yd_ref, xb_ref, bb_ref, cb_ref, pc_ref, xh_ref, ch_ref, ph_ref,
                  cw_ref, pw_ref, ps_ref, bp_ref, wo_ref, o_ref, u_ref, s_ref, *, seq):
    d = x_ref.shape[1]
    x = x_ref[...]
    h = _rms(x, g_ref[...], RMS_EPS).astype(BF16)
    def gated(b, y):
        gate = jax.nn.sigmoid(jnp.dot(h, wg_ref[:, b * d:(b + 1) * d], preferred_element_type=F32))
        return gate * jnp.dot(y, bp_ref[b * BRANCH_WIDTH:(b + 1) * BRANCH_WIDTH, :], preferred_element_type=F32)

    merged = gated(0, ya_ref[...]) + gated(3, yd_ref[...])
    y_b, y_c = _conv_pool_branches(xb_ref, bb_ref, cb_ref, pc_ref, xh_ref, ch_ref, ph_ref, cw_ref, pw_ref, ps_ref,
                                   u_ref, s_ref, seq)
    merged = merged + gated(1, y_b) + gated(2, y_c)
    o_ref[...] = x + jnp.dot(merged.astype(BF16), wo_ref[...], preferred_element_type=F32)


def _merge(x2, g, w_gate, y_a, y_d, u, conv_w, pool_w_bd, pool_scale, bp, w_out, seq, tm):
    n, d = x2.shape
    full = lambda a: pl.BlockSpec(a.shape, lambda i: (0,) * a.ndim)
    resident = lambda a: pl.BlockSpec(a.shape, lambda i: (0,) * a.ndim, pipeline_mode=pl.Buffered(1))
    rows = lambda a: pl.BlockSpec((tm, a.shape[1]), lambda i: (i, 0))
    u_rows = lambda c: pl.BlockSpec((tm, BRANCH_WIDTH), lambda i, c=c: (i, c))
    u_halo = lambda c: pl.BlockSpec((HALO, BRANCH_WIDTH),
                                    lambda i, c=c: (jnp.maximum(i * (tm // HALO) - 1, 0), c))
    return pl.pallas_call(
        functools.partial(_merge_kernel, seq=seq),
        grid=(n // tm,),
        in_specs=[rows(x2), full(g), resident(w_gate), rows(y_a), rows(y_d),
                  u_rows(3), u_rows(4), u_rows(5), u_rows(6), u_halo(3), u_halo(5), u_halo(6),
                  full(conv_w), full(pool_w_bd), full(pool_scale), resident(bp), resident(w_out)],
        out_specs=pl.BlockSpec((tm, d), lambda i: (i, 0)),
        out_shape=jax.ShapeDtypeStruct((n, d), F32),
        scratch_shapes=[pltpu.VMEM((HALO + tm, BRANCH_WIDTH), F32),
                        pltpu.VMEM((HALO + tm, BRANCH_WIDTH), F32)],
        compiler_params=_params("parallel"),
        name="merge",
    )(x2, g, w_gate, y_a, y_d, u, u, u, u, u, u, u, conv_w, pool_w_bd, pool_scale, bp, w_out)


ROUTE_COLS = 8


def _pack_bf16_pairs(h):
    c = h.shape[1] // 2
    bits = lax.bitcast_convert_type(h.astype(BF16).astype(F32), jnp.uint32)
    return (bits[:, :c] >> 16) | (bits[:, c:] & jnp.uint32(0xFFFF0000))


def _unpack_bf16_pairs(w):
    lo = lax.bitcast_convert_type(w << 16, F32)
    hi = lax.bitcast_convert_type(w & jnp.uint32(0xFFFF0000), F32)
    return jnp.concatenate([lo, hi], axis=1)


def _router_kernel(x_ref, g_ref, r_ref, tri_ref, idx_ref, wts_ref, hpk_ref, cnt_ref, seen_ref, *, n_exp):
    @pl.when(pl.program_id(0) == 0)
    def _():
        seen_ref[...] = jnp.zeros(seen_ref.shape, F32)

    h = _rms(x_ref[...], g_ref[...], RMS_EPS)
    h_hi = h.astype(BF16)
    h_lo = (h - h_hi.astype(F32)).astype(BF16)
    r = r_ref[...]
    r_hi = r.astype(BF16)
    r_lo = (r - r_hi.astype(F32)).astype(BF16)
    logits = (jnp.dot(h_hi, r_hi, preferred_element_type=F32) + jnp.dot(h_hi, r_lo, preferred_element_type=F32)
              + jnp.dot(h_lo, r_hi, preferred_element_type=F32))
    lane = lax.broadcasted_iota(jnp.int32, logits.shape, 1).astype(F32)
    logits = jnp.where(lane < n_exp, logits, -jnp.inf)
    big = float(logits.shape[1])
    m1 = jnp.max(logits, axis=1, keepdims=True)
    i1 = jnp.min(jnp.where(logits == m1, lane, big), axis=1, keepdims=True)
    rest = jnp.where(lane == i1, -jnp.inf, logits)
    m2 = jnp.max(rest, axis=1, keepdims=True)
    i2 = jnp.min(jnp.where(rest == m2, lane, big), axis=1, keepdims=True)
    e2 = jnp.exp(m2 - m1)
    w1 = 1.0 / (1.0 + e2)
    w2 = e2 / (1.0 + e2)

    chosen = (lane == i1) | (lane == i2)
    before = seen_ref[...] + jnp.dot(tri_ref[...], chosen.astype(BF16), preferred_element_type=F32)
    r1 = jnp.sum(jnp.where(lane == i1, before, 0.0), axis=1, keepdims=True)
    r2 = jnp.sum(jnp.where(lane == i2, before, 0.0), axis=1, keepdims=True)
    seen_ref[...] += jnp.sum(chosen.astype(F32), axis=0, keepdims=True)
    cnt_ref[...] = seen_ref[...]

    pick = lambda col, a, b, c, d: jnp.where(col == 0, a, jnp.where(col == 1, b, jnp.where(col == 2, c, d)))
    record = pick(lane, i1, i2, r1, jnp.where(lane == 3, r2, 0.0))
    idx_ref[...] = record.T[0:idx_ref.shape[0], :].astype(jnp.int32)
    wts_ref[...] = pick(lax.broadcasted_iota(jnp.int32, wts_ref.shape, 1), w1, w2, 0.0, 0.0)
    hpk_ref[...] = _pack_bf16_pairs(h)


def _router(x2, g, router_pad, n_exp, tm):
    n, d = x2.shape
    tri = jnp.asarray(np.tril(np.ones((tm, tm), np.float32), -1), BF16)
    full = lambda a: pl.BlockSpec(a.shape, lambda i: (0,) * a.ndim)
    rows = lambda w: pl.BlockSpec((tm, w), lambda i: (i, 0))
    return pl.pallas_call(
        functools.partial(_router_kernel, n_exp=n_exp),
        grid=(n // tm,),
        in_specs=[rows(d), full(g), full(router_pad), full(tri)],
        out_specs=[pl.BlockSpec((ROUTE_COLS, tm), lambda i: (0, i)), rows(ROUTE_COLS), rows(d // 2),
                   pl.BlockSpec((1, router_pad.shape[1]), lambda i: (0, 0))],
        out_shape=[jax.ShapeDtypeStruct((ROUTE_COLS, n), jnp.int32),
                   jax.ShapeDtypeStruct((n, ROUTE_COLS), F32),
                   jax.ShapeDtypeStruct((n, d // 2), jnp.uint32),
                   jax.ShapeDtypeStruct((1, router_pad.shape[1]), F32)],
        scratch_shapes=[pltpu.VMEM((1, router_pad.shape[1]), F32)],
        compiler_params=_params("arbitrary"),
        name="router",
    )(x2, g, router_pad, tri)


V7X_SC_CORES = 2
V7X_SC_SUBCORES = 16
SC_GATHER_BYTES = 256 * 1024
SC_MAX_INDEX_VECTOR = 128


def _sc_gather(table, idx):
    n_rows, width = idx.shape[0], table.shape[1]
    workers = V7X_SC_CORES * V7X_SC_SUBCORES
    chunk = min(SC_MAX_INDEX_VECTOR, SC_GATHER_BYTES // (width * table.dtype.itemsize))
    assert table.dtype.itemsize == 4 and n_rows % (workers * chunk) == 0 and chunk % 8 == 0
    per_worker = n_rows // workers
    mesh = plsc.VectorSubcoreMesh(core_axis_name="c", subcore_axis_name="s",
                                  num_cores=V7X_SC_CORES, num_subcores=V7X_SC_SUBCORES)

    def body(table_hbm, idx_hbm, out_hbm, idx_v, rows_v, sem):
        base = (lax.axis_index("s") * V7X_SC_CORES + lax.axis_index("c")) * per_worker

        @pl.loop(0, per_worker // chunk)
        def _(i):
            off = base + i * chunk
            pltpu.sync_copy(idx_hbm.at[pl.ds(off, chunk)], idx_v)
            pltpu.async_copy(table_hbm.at[idx_v], rows_v, sem).wait()
            pltpu.sync_copy(rows_v, out_hbm.at[pl.ds(off, chunk)])

    return pl.kernel(
        body,
        out_type=jax.ShapeDtypeStruct((n_rows, width), table.dtype),
        mesh=mesh,
        scratch_types=[pltpu.VMEM((chunk,), jnp.int32), pltpu.VMEM((chunk, width), table.dtype),
                       pltpu.SemaphoreType.DMA],
        name="sc_gather",
    )(table, idx)


def _sc_scatter_pairs(table, pos):
    n, width = table.shape
    workers = V7X_SC_CORES * V7X_SC_SUBCORES
    chunk = min(SC_MAX_INDEX_VECTOR, SC_GATHER_BYTES // (width * table.dtype.itemsize))
    assert table.dtype.itemsize == 4 and n % (workers * chunk) == 0 and chunk % 8 == 0
    per_worker = n // workers
    mesh = plsc.VectorSubcoreMesh(core_axis_name="c", subcore_axis_name="s",
                                  num_cores=V7X_SC_CORES, num_subcores=V7X_SC_SUBCORES)

    def body(table_hbm, pos_hbm, out_hbm, idx_a, idx_b, rows_v, sem):
        base = (lax.axis_index("s") * V7X_SC_CORES + lax.axis_index("c")) * per_worker

        @pl.loop(0, per_worker // chunk)
        def _(i):
            off = base + i * chunk
            pltpu.sync_copy(table_hbm.at[pl.ds(off, chunk)], rows_v)
            pltpu.sync_copy(pos_hbm.at[pl.ds(off, chunk)], idx_a)
            pltpu.sync_copy(pos_hbm.at[pl.ds(n + off, chunk)], idx_b)
            pltpu.async_copy(rows_v, out_hbm.at[idx_a], sem).wait()
            pltpu.async_copy(rows_v, out_hbm.at[idx_b], sem).wait()

    return pl.kernel(
        body,
        out_type=jax.ShapeDtypeStruct((2 * n, width), table.dtype),
        mesh=mesh,
        scratch_types=[pltpu.VMEM((chunk,), jnp.int32), pltpu.VMEM((chunk,), jnp.int32),
                       pltpu.VMEM((chunk, width), table.dtype), pltpu.SemaphoreType.DMA],
        name="sc_scatter_pairs",
    )(table, pos)


def _swiglu_chunk(h, w1, w3, w2):
    a = jnp.dot(h, w1, preferred_element_type=F32)
    b = jnp.dot(h, w3, preferred_element_type=F32)
    t = (a * jax.nn.sigmoid(a) * b).astype(BF16)
    return jnp.dot(t, w2, preferred_element_type=F32)


def _ffn_kernel(x_ref, g_ref, w1_ref, w3_ref, w2_ref, fg_ref, o_ref, *, fc, final_norm):
    d_ff = w1_ref.shape[1]
    x = x_ref[...]
    h = _rms(x, g_ref[...], RMS_EPS).astype(BF16)
    acc = None
    for c0 in range(0, d_ff, fc):
        c1 = min(c0 + fc, d_ff)
        y = _swiglu_chunk(h, w1_ref[:, c0:c1], w3_ref[:, c0:c1], w2_ref[c0:c1, :])
        acc = y if acc is None else acc + y
    out = x + acc
    if final_norm:
        out = _rms(out, fg_ref[...], RMS_EPS)
    o_ref[...] = out


def _ffn(x2, g, w1, w3, w2, final_g, tm, fc, final_norm):
    n, d = x2.shape
    resident = lambda a: pl.BlockSpec(a.shape, lambda i: (0,) * a.ndim, pipeline_mode=pl.Buffered(1))
    return pl.pallas_call(
        functools.partial(_ffn_kernel, fc=fc, final_norm=final_norm),
        grid=(n // tm,),
        in_specs=[pl.BlockSpec((tm, d), lambda i: (i, 0)),
                  pl.BlockSpec((1, d), lambda i: (0, 0)),
                  resident(w1), resident(w3), resident(w2),
                  pl.BlockSpec((1, d), lambda i: (0, 0))],
        out_specs=pl.BlockSpec((tm, d), lambda i: (i, 0)),
        out_shape=jax.ShapeDtypeStruct((n, d), F32),
        compiler_params=_params("parallel"),
        name="dense_ffn",
    )(x2, g, w1, w3, w2, final_g)


EXPERT_SUBCHUNK = 512


def _expert_ffn_kernel(tile_ref, exp_ref, lo_ref, hi_ref, xs_ref, w1_ref, w3_ref, w2_ref, o_ref):
    it = pl.program_id(0)
    rows = o_ref.shape[0]
    lo, hi = lo_ref[it], hi_ref[it]

    @pl.when(hi > lo)
    def _():
        h = _unpack_bf16_pairs(xs_ref[...]).astype(BF16)
        d_ff = w1_ref.shape[2]
        y = None
        for c0 in range(0, d_ff, EXPERT_SUBCHUNK):
            c1 = min(c0 + EXPERT_SUBCHUNK, d_ff)
            part = _swiglu_chunk(h, w1_ref[0, :, c0:c1], w3_ref[0, :, c0:c1], w2_ref[0, c0:c1, :])
            y = part if y is None else y + part
        packed = _pack_bf16_pairs(y)
        first_row = lo - tile_ref[it] * rows

        @pl.when(first_row == 0)
        def _():
            o_ref[...] = packed

        @pl.when(first_row > 0)
        def _():
            row = lax.broadcasted_iota(jnp.int32, o_ref.shape, 0)
            o_ref[...] = jnp.where(row >= first_row, packed, o_ref[...])


def _expert_ffn(xs, items, w1, w3, w2, rows):
    n_pairs, half = xs.shape
    item_tile, item_expert, item_lo, item_hi = items
    expert_weights = lambda w: pl.BlockSpec((1,) + w.shape[1:], lambda i, t, e, lo, hi: (e[i], 0, 0),
                                            pipeline_mode=pl.Buffered(1))
    grid_spec = pltpu.PrefetchScalarGridSpec(
        num_scalar_prefetch=4,
        grid=(item_tile.shape[0],),
        in_specs=[pl.BlockSpec((rows, half), lambda i, t, e, lo, hi: (t[i], 0)),
                  expert_weights(w1), expert_weights(w3), expert_weights(w2)],
        out_specs=pl.BlockSpec((rows, half), lambda i, t, e, lo, hi: (t[i], 0)),
    )
    return pl.pallas_call(
        _expert_ffn_kernel,
        grid_spec=grid_spec,
        out_shape=jax.ShapeDtypeStruct((n_pairs, half), jnp.uint32),
        compiler_params=_params("arbitrary"),
        name="expert_ffn",
    )(item_tile, item_expert, item_lo, item_hi, xs, w1, w3, w2)


def _work_items(counts, n_pairs, rows):
    n_exp = counts.shape[0]
    n_tiles = n_pairs // rows
    ends = jnp.cumsum(counts)
    starts = ends - counts
    expert_cuts = starts[1:]
    expert_at = jnp.arange(n_exp - 1, dtype=jnp.int32) + jnp.minimum(expert_cuts // rows + 1, n_tiles)
    k = jnp.arange(n_tiles + n_exp - 1, dtype=jnp.int32)[:, None]
    experts_before = jnp.sum(expert_at[None, :] < k, axis=1, dtype=jnp.int32)
    is_expert_cut = expert_at[None, :] == k
    lo = jnp.where(jnp.any(is_expert_cut, axis=1), jnp.sum(jnp.where(is_expert_cut, expert_cuts[None, :], 0), axis=1),
                   (k[:, 0] - experts_before) * rows).astype(jnp.int32)
    hi = jnp.concatenate([lo[1:], jnp.full((1,), n_pairs, jnp.int32)])
    tile = jnp.minimum(lo // rows, n_tiles - 1)
    expert = jnp.minimum(jnp.searchsorted(ends, lo, side="right").astype(jnp.int32), n_exp - 1)
    return (tile, expert, lo, hi), starts


def _combine_kernel(x_ref, y1_ref, y2_ref, w_ref, fg_ref, o_ref, *, final_norm):
    w = w_ref[...]
    out = x_ref[...] + (w[:, 0:1] * _unpack_bf16_pairs(y1_ref[...]) + w[:, 1:2] * _unpack_bf16_pairs(y2_ref[...]))
    if final_norm:
        out = _rms(out, fg_ref[...], RMS_EPS)
    o_ref[...] = out


def _combine(x2, y_pairs, wts, final_g, tm, final_norm):
    n, d = x2.shape
    nt = n // tm
    return pl.pallas_call(
        functools.partial(_combine_kernel, final_norm=final_norm),
        grid=(nt,),
        in_specs=[pl.BlockSpec((tm, d), lambda i: (i, 0)),
                  pl.BlockSpec((tm, d // 2), lambda i: (i, 0)),
                  pl.BlockSpec((tm, d // 2), lambda i: (i + nt, 0)),
                  pl.BlockSpec((tm, wts.shape[1]), lambda i: (i, 0)),
                  pl.BlockSpec((1, d), lambda i: (0, 0))],
        out_specs=pl.BlockSpec((tm, d), lambda i: (i, 0)),
        out_shape=jax.ShapeDtypeStruct((n, d), F32),
        compiler_params=_params("parallel"),
        name="moe_combine",
    )(x2, y_pairs, y_pairs, wts, final_g)


def _moe(x2, g_ffn, router, w1, w3, w2, final_g, tm, final_norm):
    n, d = x2.shape
    n_exp = router.shape[1]
    router_pad = jnp.pad(router, ((0, 0), (0, 128 - n_exp)))
    idx, wts, h_packed, seen = _router(x2, g_ffn, router_pad, n_exp, tm)

    rows = _pick_tile(2 * n, 512)
    counts = seen[0, :n_exp].astype(jnp.int32)
    items, starts = _work_items(counts, 2 * n, rows)
    pos = jnp.concatenate([starts[idx[0]] + idx[2], starts[idx[1]] + idx[3]])

    xs = _sc_scatter_pairs(h_packed, pos)
    ys = _expert_ffn(xs, items, w1, w3, w2, rows)
    y_pairs = _sc_gather(ys, pos)
    return _combine(x2, y_pairs, wts, final_g, tm, final_norm)


def _block_diag(w):
    g, c, _ = w.shape
    eye = jnp.eye(g, dtype=w.dtype)
    return (eye[:, None, :, None] * w[:, :, None, :]).reshape(g * c, g * c)


def _pick_tile(n, target):
    t = min(n, target)
    while n % t:
        t //= 2
    return t


def kernel(x, bias_table, mix_norm_g, w_in, conv_w, pool_w, pool_scale, diff_lambda, diff_subln_g,
           branch_proj, w_out, ffn_norm_g, dense_w1, dense_w3, dense_w2, moe_router, moe_w1, moe_w3,
           moe_w2, final_norm_g):
    batch, seq, d = x.shape
    depth = w_in.shape[0]
    n_mix = N_MIX_SLICES * BRANCH_WIDTH
    n = batch * seq
    assert seq % ATT_BLOCK == 0 and d % 128 == 0
    tm = _pick_tile(seq, 512)

    x2 = x.reshape(n, d)
    tiles = _bias_tiles(bias_table)
    row = lambda v: v.reshape(1, -1)
    final_g = row(final_norm_g)

    for i in range(depth):
        last_layer = i == depth - 1
        g_mix = row(mix_norm_g[i])
        w_mix = w_in[i, :, :n_mix].astype(BF16)
        w_gate = w_in[i, :, n_mix:].astype(BF16)
        u = _inproj(x2, g_mix, w_mix, tm)

        lam_init = 0.8 - 0.6 * math.exp(-0.3 * i)
        y_a = _moba(u, tiles, batch, seq)
        y_d = _diff(u, tiles, diff_lambda[i], row(jnp.tile(diff_subln_g[i], N_HEADS)), batch, seq, lam_init)
        x2 = _merge(x2, g_mix, w_gate, y_a, y_d, u, conv_w[i], _block_diag(pool_w[i]).astype(BF16),
                    row(pool_scale[i]), branch_proj[i].reshape(-1, d).astype(BF16), w_out[i].astype(BF16),
                    seq, tm)

        g_ffn = row(ffn_norm_g[i])
        j = i // 2
        if i % 2 == 0:
            x2 = _ffn(x2, g_ffn, dense_w1[j].astype(BF16), dense_w3[j].astype(BF16),
                      dense_w2[j].astype(BF16), final_g, tm, min(512, dense_w1.shape[2]), final_norm=last_layer)
        else:
            x2 = _moe(x2, g_ffn, moe_router[j], moe_w1[j].astype(BF16), moe_w3[j].astype(BF16),
                      moe_w2[j].astype(BF16), final_g, tm, final_norm=last_layer)

    return x2.reshape(batch, seq, d)
```

```python
import functools
import math

import numpy as np
import jax
import jax.numpy as jnp
from jax import lax
from jax.experimental import pallas as pl
from jax.experimental.pallas import tpu as pltpu
from jax.experimental.pallas import tpu_sc as plsc

F32 = jnp.float32
BF16 = jnp.bfloat16

BRANCH_WIDTH = 256
N_MIX_SLICES = 10
HEAD_WIDTH = 64
N_HEADS = 4
DIFF_QK_DIM = 32
ATT_BLOCK = 256
MOBA_TOPK = 3
CONV_WIDTH = 3
POOL_WINDOWS = (2, 4, 8, 16)
POOL_GROUP = 64
HALO = 16
REL_BUCKETS = 32
REL_MAX_DIST = 128
TOP_K_EXPERTS = 2
RMS_EPS = 1e-6
SUBLN_EPS = 1e-5
NEG_INF = -1e30
LOG2_E = math.log2(math.e)
V7X_VMEM_BYTES = 64 * 1024 * 1024
VMEM_LIMIT = V7X_VMEM_BYTES - 8 * 1024 * 1024

_TRANS_B = (((1,), (1,)), ((), ()))


def _params(*sem):
    return pltpu.CompilerParams(dimension_semantics=sem, vmem_limit_bytes=VMEM_LIMIT)


def _rms(x, g, eps):
    r = lax.rsqrt(jnp.mean(x * x, axis=-1, keepdims=True) + eps)
    return x * r * g


def _inproj_kernel(x_ref, g_ref, w_ref, o_ref):
    h = _rms(x_ref[...], g_ref[...], RMS_EPS).astype(BF16)
    o_ref[...] = jnp.dot(h, w_ref[...], preferred_element_type=F32)


def _inproj(x2, g, w, tm):
    n, d = x2.shape
    wn = w.shape[1]
    return pl.pallas_call(
        _inproj_kernel,
        grid=(n // tm,),
        in_specs=[pl.BlockSpec((tm, d), lambda i: (i, 0)),
                  pl.BlockSpec((1, d), lambda i: (0, 0)),
                  pl.BlockSpec((d, wn), lambda i: (0, 0))],
        out_specs=pl.BlockSpec((tm, wn), lambda i: (i, 0)),
        out_shape=jax.ShapeDtypeStruct((n, wn), F32),
        compiler_params=_params("parallel"),
        name="inproj",
    )(x2, g, w)


def _rel_bucket_np(dist):
    n = np.maximum(dist, 0)
    max_exact = REL_BUCKETS // 2
    nf = np.maximum(n, max_exact).astype(np.float32)
    large = max_exact + (np.log(nf / np.float32(max_exact)) / np.float32(math.log(REL_MAX_DIST / max_exact))
                         * np.float32(REL_BUCKETS - max_exact)).astype(np.int32)
    large = np.minimum(large, REL_BUCKETS - 1)
    return np.where(n < max_exact, n, large).astype(np.int32)


def _bucket_tiles():
    i = np.arange(ATT_BLOCK)[:, None]
    j = np.arange(ATT_BLOCK)[None, :]
    return np.stack([_rel_bucket_np(i - j), _rel_bucket_np(ATT_BLOCK + i - j)])


def _bias_tiles_kernel(tab_ref, bkt_ref, o_ref):
    h = pl.program_id(0)
    bkt = bkt_ref[...]
    acc = jnp.zeros(bkt.shape, F32)
    for b in range(REL_BUCKETS):
        acc = jnp.where(bkt == b, tab_ref[b, h], acc)
    o_ref[0] = (acc - tab_ref[REL_BUCKETS - 1, h]) * LOG2_E


def _bias_tiles(bias_table):
    n_heads = bias_table.shape[1]
    bkt = jnp.asarray(_bucket_tiles())
    return pl.pallas_call(
        _bias_tiles_kernel,
        grid=(n_heads,),
        in_specs=[pl.BlockSpec(memory_space=pltpu.SMEM),
                  pl.BlockSpec((2, ATT_BLOCK, ATT_BLOCK), lambda h: (0, 0, 0))],
        out_specs=pl.BlockSpec((1, 2, ATT_BLOCK, ATT_BLOCK), lambda h: (h, 0, 0, 0)),
        out_shape=jax.ShapeDtypeStruct((n_heads, 2, ATT_BLOCK, ATT_BLOCK), F32),
        compiler_params=_params("arbitrary"),
        name="bias_tiles",
    )(bias_table, bkt)


LANE_TILE = 128
SUBLANES = 8


def _head_lanes(lane, h):
    return (lane >= h * HEAD_WIDTH) & (lane < (h + 1) * HEAD_WIDTH)


def _head_to_low_lanes(x, h):
    assert 2 * HEAD_WIDTH == LANE_TILE
    tile = x[:, (h // 2) * LANE_TILE:(h // 2 + 1) * LANE_TILE]
    if h % 2:
        tile = pltpu.roll(tile, HEAD_WIDTH, axis=1)
    return jnp.where(lax.broadcasted_iota(jnp.int32, tile.shape, 1) < HEAD_WIDTH, tile, 0.0)


def _fold(op, s):
    out = s[:, :LANE_TILE]
    for c in range(LANE_TILE, s.shape[1], LANE_TILE):
        out = op(out, s[:, c:c + LANE_TILE])
    return out


def _stage_values(v_ref, vm_ref):
    t = ATT_BLOCK
    lane = lax.broadcasted_iota(jnp.int32, (t, BRANCH_WIDTH), 1)
    for j in range(vm_ref.shape[0]):
        vj = v_ref[j * t:(j + 1) * t, :]
        for h in range(N_HEADS):
            vm_ref[j, h * t:(h + 1) * t, :] = jnp.where(_head_lanes(lane, h), vj, 0.0).astype(BF16)


def _softmax_attend(qb, q16, head_of, key_slot, groups, kb_ref, vm_ref, tiles_ref, s_ref, m_ref, l_ref, acc_ref):
    t = ATT_BLOCK
    chains = range(len(q16))
    causal = (lax.broadcasted_iota(jnp.int32, (t, t), 0) >= lax.broadcasted_iota(jnp.int32, (t, t), 1))

    slots = sorted({key_slot(i) for i in chains})

    def keys(j):
        return {slot: kb_ref[slot, pl.ds(pl.multiple_of(j * t, t), t), :] for slot in slots}

    def score(i, kj):
        return lax.dot_general(q16[i], kj[key_slot(i)], _TRANS_B, preferred_element_type=F32)

    def in_pairs(count, fn):
        def pair(p, carry):
            fn([2 * p, 2 * p + 1])
            return carry

        lax.fori_loop(0, count >> 1, pair, 0)

        @pl.when((count & 1) == 1)
        def _():
            fn([count - 1])

    def near(with_prev):
        k_own = keys(qb)
        k_prev = keys(qb - 1) if with_prev else None
        for i in chains:
            s = jnp.where(causal, score(i, k_own) + tiles_ref[head_of(i), 0], NEG_INF)
            s_ref[i, qb] = s
            row_max = _fold(jnp.maximum, s)
            if with_prev:
                s = score(i, k_prev) + tiles_ref[head_of(i), 1]
                s_ref[i, qb - 1] = s
                row_max = jnp.maximum(row_max, _fold(jnp.maximum, s))
            m_ref[i] = row_max

    pl.when(qb == 0)(functools.partial(near, False))
    pl.when(qb >= 1)(functools.partial(near, True))

    def far(js):
        ks = [keys(j) for j in js]
        for i in chains:
            ss = [score(i, kj) for kj in ks]
            for j, s in zip(js, ss):
                s_ref[i, j] = s
            m_ref[i] = functools.reduce(jnp.maximum, [m_ref[i]] + [_fold(jnp.maximum, s) for s in ss])

    in_pairs(jnp.maximum(qb - 1, 0), far)

    for i in chains:
        m_ref[i] = jnp.broadcast_to(jnp.max(m_ref[i], axis=1, keepdims=True), (t, LANE_TILE))

    def accumulate(js, first=False):
        for g, members in enumerate(groups):
            parts = [[] for _ in js]
            for i in members:
                sums = []
                for a, j in enumerate(js):
                    s = s_ref[i, j]
                    p = [jnp.exp2(s[:, c:c + LANE_TILE] - m_ref[i]) for c in range(0, t, LANE_TILE)]
                    sums += p
                    parts[a] += [x.astype(BF16) for x in p]
                row_sum = functools.reduce(jnp.add, sums)
                l_ref[i] = row_sum if first else l_ref[i] + row_sum
            lhs = jnp.concatenate([x for tile_parts in parts for x in tile_parts], axis=1)
            rhs = jnp.concatenate([vm_ref[j] for j in js], axis=0) if len(js) > 1 else vm_ref[js[0]]
            pv = jnp.dot(lhs, rhs, preferred_element_type=F32)
            acc_ref[g] = pv if first else acc_ref[g] + pv

    pl.when(qb == 0)(functools.partial(accumulate, [qb], first=True))
    pl.when(qb >= 1)(functools.partial(accumulate, [qb - 1, qb], first=True))
    in_pairs(jnp.maximum(qb - 1, 0), accumulate)


def _per_head_lanes(lane, cols):
    out = jnp.broadcast_to(cols[0], lane.shape)
    for h in range(1, N_HEADS):
        out = jnp.where(_head_lanes(lane, h), cols[h], out)
    return out


def _attention_scratch(seq, n_chains, n_groups, key_slots, key_width):
    t = ATT_BLOCK
    n_blk = seq // t
    return [pltpu.VMEM((key_slots, seq, key_width), BF16),
            pltpu.VMEM((n_blk, N_HEADS * t, BRANCH_WIDTH), BF16),
            pltpu.VMEM((n_chains, n_blk, t, t), F32),
            pltpu.VMEM((n_chains, t, LANE_TILE), F32),
            pltpu.VMEM((n_chains, t, LANE_TILE), F32),
            pltpu.VMEM((n_groups, t, BRANCH_WIDTH), F32)]


def _moba_kernel(q_ref, k_ref, v_ref, tiles_ref, o_ref, kb_ref, vm_ref, s_ref, m_ref, l_ref, acc_ref, kmean_ref):
    t = ATT_BLOCK
    n_blk = k_ref.shape[0] // t
    qb = pl.program_id(1)

    heads = range(N_HEADS)
    blk_rows = kmean_ref.shape[0] // N_HEADS
    assert n_blk <= blk_rows <= LANE_TILE

    assert HEAD_WIDTH + n_blk <= LANE_TILE
    low = lax.broadcasted_iota(jnp.int32, (t, LANE_TILE), 1)

    @pl.when(qb == 0)
    def _():
        _stage_values(v_ref, vm_ref)
        ch = lax.broadcasted_iota(jnp.int32, (1, BRANCH_WIDTH), 1)
        kmean_ref[...] = jnp.zeros(kmean_ref.shape, F32)
        for j in range(n_blk):
            kj = k_ref[j * t:(j + 1) * t, :]
            mean_j = jnp.mean(kj, axis=0, keepdims=True)
            for h in heads:
                kb_ref[h, j * t:(j + 1) * t, :] = jnp.where(low == HEAD_WIDTH + j, 1.0,
                                                            _head_to_low_lanes(kj, h)).astype(BF16)
                kmean_ref[h * blk_rows + j:h * blk_rows + j + 1, :] = jnp.where(_head_lanes(ch, h), mean_j, 0.0)

    lane = lax.broadcasted_iota(jnp.int32, (t, BRANCH_WIDTH), 1)
    q = q_ref[...]
    gate = lax.dot_general(kmean_ref[...], q * (HEAD_WIDTH ** -0.5), _TRANS_B, precision=lax.Precision.HIGHEST,
                           preferred_element_type=F32)
    blk = lax.broadcasted_iota(jnp.int32, (blk_rows, t), 0).astype(F32)
    drops = []
    for h in heads:
        g = gate[h * blk_rows:(h + 1) * blk_rows, :]
        avail = blk < qb.astype(F32)
        for _ in range(MOBA_TOPK):
            best = jnp.max(jnp.where(avail, g, -jnp.inf), axis=0, keepdims=True)
            first = jnp.min(jnp.where(avail & (g == best), blk, float(blk_rows)), axis=0, keepdims=True)
            avail = avail & (blk != first)
        drops += [jnp.zeros((HEAD_WIDTH, t), F32), jnp.where(avail, NEG_INF, 0.0),
                  jnp.zeros((LANE_TILE - HEAD_WIDTH - blk_rows, t), F32)]
    eye = (lax.broadcasted_iota(jnp.int32, (t, t), 0) == lax.broadcasted_iota(jnp.int32, (t, t), 1)).astype(BF16)
    drop = lax.dot_general(eye, jnp.concatenate(drops, axis=0).astype(BF16), _TRANS_B,
                           preferred_element_type=F32).astype(BF16)
    qf = q * (HEAD_WIDTH ** -0.5 * LOG2_E)
    q_aug = [jnp.where(low < HEAD_WIDTH, _head_to_low_lanes(qf, h).astype(BF16),
                       drop[:, h * LANE_TILE:(h + 1) * LANE_TILE]) for h in heads]

    _softmax_attend(qb, q_aug, lambda i: i, lambda i: i, [list(heads)], kb_ref, vm_ref, tiles_ref,
                    s_ref, m_ref, l_ref, acc_ref)
    row_sums = _per_head_lanes(lane, [jnp.sum(l_ref[h], axis=1, keepdims=True) for h in heads])
    o_ref[...] = (acc_ref[0] / row_sums).astype(o_ref.dtype)


def _moba(u, tiles, batch, seq):
    t = ATT_BLOCK
    nq = seq // t
    return pl.pallas_call(
        _moba_kernel,
        grid=(batch, nq),
        in_specs=[pl.BlockSpec((t, BRANCH_WIDTH), lambda b, q: (b * nq + q, 0)),
                  pl.BlockSpec((seq, BRANCH_WIDTH), lambda b, q: (b, 1)),
                  pl.BlockSpec((seq, BRANCH_WIDTH), lambda b, q: (b, 2)),
                  pl.BlockSpec((N_HEADS, 2, t, t), lambda b, q: (0, 0, 0, 0))],
        out_specs=pl.BlockSpec((t, BRANCH_WIDTH), lambda b, q: (b * nq + q, 0)),
        out_shape=jax.ShapeDtypeStruct((batch * seq, BRANCH_WIDTH), BF16),
        scratch_shapes=(_attention_scratch(seq, N_HEADS, 1, N_HEADS, LANE_TILE)
                        + [pltpu.VMEM((N_HEADS * SUBLANES * pl.cdiv(nq, SUBLANES), BRANCH_WIDTH), F32)]),
        compiler_params=_params("arbitrary", "arbitrary"),
        name="moba",
    )(u, u, u, tiles)


def _diff_kernel(lam_ref, g_ref, q_ref, k_ref, v_ref, tiles_ref, o_ref, kb_ref, vm_ref, s_ref, m_ref, l_ref,
                 acc_ref, *, lam_init):
    t = ATT_BLOCK
    qb = pl.program_id(1)

    @pl.when(qb == 0)
    def _():
        _stage_values(v_ref, vm_ref)
        kb_ref[0] = k_ref[...].astype(BF16)

    lp = lam_ref[...]
    lam = (jnp.exp(jnp.sum(lp[0:1] * lp[1:2], axis=1, keepdims=True))
           - jnp.exp(jnp.sum(lp[2:3] * lp[3:4], axis=1, keepdims=True)) + lam_init)

    lane = lax.broadcasted_iota(jnp.int32, (t, BRANCH_WIDTH), 1)
    qf = q_ref[...] * (DIFF_QK_DIM ** -0.5 * LOG2_E)
    q16 = []
    for h in range(N_HEADS):
        for c in range(2):
            lo = h * HEAD_WIDTH + c * DIFF_QK_DIM
            q16.append(jnp.where((lane >= lo) & (lane < lo + DIFF_QK_DIM), qf, 0.0).astype(BF16))
    groups = [[2 * h + c for h in range(N_HEADS)] for c in range(2)]
    _softmax_attend(qb, q16, lambda i: i // 2, lambda i: 0, groups, kb_ref, vm_ref, tiles_ref,
                    s_ref, m_ref, l_ref, acc_ref)

    row_sums = [_per_head_lanes(lane, [jnp.sum(l_ref[i], axis=1, keepdims=True) for i in members])
                for members in groups]
    o = acc_ref[0] / row_sums[0] - lam * (acc_ref[1] / row_sums[1])
    sq = o * o
    mean_sq = _per_head_lanes(lane, [jnp.sum(jnp.where(_head_lanes(lane, h), sq, 0.0), axis=1, keepdims=True)
                                     for h in range(N_HEADS)]) * (1.0 / HEAD_WIDTH)
    o_ref[...] = (o * lax.rsqrt(mean_sq + SUBLN_EPS) * g_ref[...] * (1.0 - lam_init)).astype(o_ref.dtype)


def _diff(u, tiles, lam_params, subln_g4, batch, seq, lam_init):
    t = ATT_BLOCK
    nq = seq // t
    return pl.pallas_call(
        functools.partial(_diff_kernel, lam_init=lam_init),
        grid=(batch, nq),
        in_specs=[pl.BlockSpec((4, DIFF_QK_DIM), lambda b, q: (0, 0)),
                  pl.BlockSpec((1, BRANCH_WIDTH), lambda b, q: (0, 0)),
                  pl.BlockSpec((t, BRANCH_WIDTH), lambda b, q: (b * nq + q, 7)),
                  pl.BlockSpec((seq, BRANCH_WIDTH), lambda b, q: (b, 8)),
                  pl.BlockSpec((seq, BRANCH_WIDTH), lambda b, q: (b, 9)),
                  pl.BlockSpec((N_HEADS, 2, t, t), lambda b, q: (1, 0, 0, 0))],
        out_specs=pl.BlockSpec((t, BRANCH_WIDTH), lambda b, q: (b * nq + q, 0)),
        out_shape=jax.ShapeDtypeStruct((batch * seq, BRANCH_WIDTH), BF16),
        scratch_shapes=_attention_scratch(seq, 2 * N_HEADS, 2, 1, BRANCH_WIDTH),
        compiler_params=_params("arbitrary", "arbitrary"),
        name="diff_attn",
    )(lam_params, subln_g4, u, u, u, tiles)


def _conv_pool_branches(xb_ref, bb_ref, cb_ref, pc_ref, xh_ref, ch_ref, ph_ref, cw_ref, pw_ref, ps_ref,
                        u_ref, s_ref, seq):
    tm = xb_ref.shape[0]
    pos0 = (pl.program_id(0) * tm) % seq
    has_history = pos0 > 0
    pos = pos0 + lax.broadcasted_iota(jnp.int32, (tm, 1), 0)
    lane = lax.broadcasted_iota(jnp.int32, (tm, BRANCH_WIDTH), 1)

    u_ref[0:HALO, :] = jnp.where(has_history, ch_ref[...] * xh_ref[...], 0.0)
    u_ref[HALO:, :] = cb_ref[...] * xb_ref[...]
    conv = cw_ref[CONV_WIDTH - 1:CONV_WIDTH, :] * u_ref[HALO:, :]
    for i in range(CONV_WIDTH - 1):
        shift = CONV_WIDTH - 1 - i
        conv = conv + cw_ref[i:i + 1, :] * u_ref[HALO - shift:HALO - shift + tm, :]
    y_b = (bb_ref[...] * conv).astype(BF16)

    s_ref[0:HALO, :] = jnp.where(has_history, ph_ref[...], 0.0)
    s_ref[HALO:, :] = pc_ref[...]
    pooled = jnp.zeros((tm, BRANCH_WIDTH), F32)
    done = 0
    for g, w in enumerate(POOL_WINDOWS):
        half = w // 2
        cur = s_ref[done + half:, :] + s_ref[done:HALO + tm - half, :]
        done += half
        s_ref[done:, :] = cur
        cnt = jnp.minimum(pos + 1, w).astype(F32)
        mean_w = s_ref[HALO:, :] / cnt
        pooled = jnp.where((lane >= g * POOL_GROUP) & (lane < (g + 1) * POOL_GROUP), mean_w, pooled)
    pooled = pooled - pc_ref[...]
    y_c = jnp.dot(pooled.astype(BF16), pw_ref[...], preferred_element_type=F32) * ps_ref[...]
    return y_b, y_c.astype(BF16)


def _merge_kernel(x_ref, g_ref, wg_ref, ya_ref, yd_ref, xb_ref, bb_ref, cb_ref, pc_ref, xh_ref, ch_ref, ph_ref,
                  cw_ref, pw_ref, ps_ref, bp_ref, wo_ref, o_ref, u_ref, s_ref, *, seq):
    d = x_ref.shape[1]
    x = x_ref[...]
    h = _rms(x, g_ref[...], RMS_EPS).astype(BF16)
    def gated(b, y):
        gate = jax.nn.sigmoid(jnp.dot(h, wg_ref[:, b * d:(b + 1) * d], preferred_element_type=F32))
        return gate * jnp.dot(y, bp_ref[b * BRANCH_WIDTH:(b + 1) * BRANCH_WIDTH, :], preferred_element_type=F32)

    merged = gated(0, ya_ref[...]) + gated(3, yd_ref[...])
    y_b, y_c = _conv_pool_branches(xb_ref, bb_ref, cb_ref, pc_ref, xh_ref, ch_ref, ph_ref, cw_ref, pw_ref, ps_ref,
                                   u_ref, s_ref, seq)
    merged = merged + gated(1, y_b) + gated(2, y_c)
    o_ref[...] = x + jnp.dot(merged.astype(BF16), wo_ref[...], preferred_element_type=F32)


def _merge(x2, g, w_gate, y_a, y_d, u, conv_w, pool_w_bd, pool_scale, bp, w_out, seq, tm):
    n, d = x2.shape
    full = lambda a: pl.BlockSpec(a.shape, lambda i: (0,) * a.ndim)
    resident = lambda a: pl.BlockSpec(a.shape, lambda i: (0,) * a.ndim, pipeline_mode=pl.Buffered(1))
    rows = lambda a: pl.BlockSpec((tm, a.shape[1]), lambda i: (i, 0))
    u_rows = lambda c: pl.BlockSpec((tm, BRANCH_WIDTH), lambda i, c=c: (i, c))
    u_halo = lambda c: pl.BlockSpec((HALO, BRANCH_WIDTH),
                                    lambda i, c=c: (jnp.maximum(i * (tm // HALO) - 1, 0), c))
    return pl.pallas_call(
        functools.partial(_merge_kernel, seq=seq),
        grid=(n // tm,),
        in_specs=[rows(x2), full(g), resident(w_gate), rows(y_a), rows(y_d),
                  u_rows(3), u_rows(4), u_rows(5), u_rows(6), u_halo(3), u_halo(5), u_halo(6),
                  full(conv_w), full(pool_w_bd), full(pool_scale), resident(bp), resident(w_out)],
        out_specs=pl.BlockSpec((tm, d), lambda i: (i, 0)),
        out_shape=jax.ShapeDtypeStruct((n, d), F32),
        scratch_shapes=[pltpu.VMEM((HALO + tm, BRANCH_WIDTH), F32),
                        pltpu.VMEM((HALO + tm, BRANCH_WIDTH), F32)],
        compiler_params=_params("parallel"),
        name="merge",
    )(x2, g, w_gate, y_a, y_d, u, u, u, u, u, u, u, conv_w, pool_w_bd, pool_scale, bp, w_out)


ROUTE_COLS = 8


def _pack_bf16_pairs(h):
    c = h.shape[1] // 2
    bits = lax.bitcast_convert_type(h.astype(BF16).astype(F32), jnp.uint32)
    return (bits[:, :c] >> 16) | (bits[:, c:] & jnp.uint32(0xFFFF0000))


def _unpack_bf16_pairs(w):
    lo = lax.bitcast_convert_type(w << 16, F32)
    hi = lax.bitcast_convert_type(w & jnp.uint32(0xFFFF0000), F32)
    return jnp.concatenate([lo, hi], axis=1)


def _router_kernel(x_ref, g_ref, r_ref, tri_ref, idx_ref, wts_ref, hpk_ref, cnt_ref, seen_ref, *, n_exp):
    @pl.when(pl.program_id(0) == 0)
    def _():
        seen_ref[...] = jnp.zeros(seen_ref.shape, F32)

    h = _rms(x_ref[...], g_ref[...], RMS_EPS)
    h_hi = h.astype(BF16)
    h_lo = (h - h_hi.astype(F32)).astype(BF16)
    r = r_ref[...]
    r_hi = r.astype(BF16)
    r_lo = (r - r_hi.astype(F32)).astype(BF16)
    logits = (jnp.dot(h_hi, r_hi, preferred_element_type=F32) + jnp.dot(h_hi, r_lo, preferred_element_type=F32)
              + jnp.dot(h_lo, r_hi, preferred_element_type=F32))
    lane = lax.broadcasted_iota(jnp.int32, logits.shape, 1).astype(F32)
    logits = jnp.where(lane < n_exp, logits, -jnp.inf)
    big = float(logits.shape[1])
    m1 = jnp.max(logits, axis=1, keepdims=True)
    i1 = jnp.min(jnp.where(logits == m1, lane, big), axis=1, keepdims=True)
    rest = jnp.where(lane == i1, -jnp.inf, logits)
    m2 = jnp.max(rest, axis=1, keepdims=True)
    i2 = jnp.min(jnp.where(rest == m2, lane, big), axis=1, keepdims=True)
    e2 = jnp.exp(m2 - m1)
    w1 = 1.0 / (1.0 + e2)
    w2 = e2 / (1.0 + e2)

    chosen = (lane == i1) | (lane == i2)
    before = seen_ref[...] + jnp.dot(tri_ref[...], chosen.astype(BF16), preferred_element_type=F32)
    r1 = jnp.sum(jnp.where(lane == i1, before, 0.0), axis=1, keepdims=True)
    r2 = jnp.sum(jnp.where(lane == i2, before, 0.0), axis=1, keepdims=True)
    seen_ref[...] += jnp.sum(chosen.astype(F32), axis=0, keepdims=True)
    cnt_ref[...] = seen_ref[...]

    pick = lambda col, a, b, c, d: jnp.where(col == 0, a, jnp.where(col == 1, b, jnp.where(col == 2, c, d)))
    record = pick(lane, i1, i2, r1, jnp.where(lane == 3, r2, 0.0))
    idx_ref[...] = record.T[0:idx_ref.shape[0], :].astype(jnp.int32)
    wts_ref[...] = pick(lax.broadcasted_iota(jnp.int32, wts_ref.shape, 1), w1, w2, 0.0, 0.0)
    hpk_ref[...] = _pack_bf16_pairs(h)


def _router(x2, g, router_pad, n_exp, tm):
    n, d = x2.shape
    tri = jnp.asarray(np.tril(np.ones((tm, tm), np.float32), -1), BF16)
    full = lambda a: pl.BlockSpec(a.shape, lambda i: (0,) * a.ndim)
    rows = lambda w: pl.BlockSpec((tm, w), lambda i: (i, 0))
    return pl.pallas_call(
        functools.partial(_router_kernel, n_exp=n_exp),
        grid=(n // tm,),
        in_specs=[rows(d), full(g), full(router_pad), full(tri)],
        out_specs=[pl.BlockSpec((ROUTE_COLS, tm), lambda i: (0, i)), rows(ROUTE_COLS), rows(d // 2),
                   pl.BlockSpec((1, router_pad.shape[1]), lambda i: (0, 0))],
        out_shape=[jax.ShapeDtypeStruct((ROUTE_COLS, n), jnp.int32),
                   jax.ShapeDtypeStruct((n, ROUTE_COLS), F32),
                   jax.ShapeDtypeStruct((n, d // 2), jnp.uint32),
                   jax.ShapeDtypeStruct((1, router_pad.shape[1]), F32)],
        scratch_shapes=[pltpu.VMEM((1, router_pad.shape[1]), F32)],
        compiler_params=_params("arbitrary"),
        name="router",
    )(x2, g, router_pad, tri)


V7X_SC_CORES = 2
V7X_SC_SUBCORES = 16
SC_GATHER_BYTES = 256 * 1024
SC_MAX_INDEX_VECTOR = 128


def _sc_gather(table, idx):
    n_rows, width = idx.shape[0], table.shape[1]
    workers = V7X_SC_CORES * V7X_SC_SUBCORES
    chunk = min(SC_MAX_INDEX_VECTOR, SC_GATHER_BYTES // (width * table.dtype.itemsize))
    assert table.dtype.itemsize == 4 and n_rows % (workers * chunk) == 0 and chunk % 8 == 0
    per_worker = n_rows // workers
    mesh = plsc.VectorSubcoreMesh(core_axis_name="c", subcore_axis_name="s",
                                  num_cores=V7X_SC_CORES, num_subcores=V7X_SC_SUBCORES)

    def body(table_hbm, idx_hbm, out_hbm, idx_v, rows_v, sem):
        base = (lax.axis_index("s") * V7X_SC_CORES + lax.axis_index("c")) * per_worker

        @pl.loop(0, per_worker // chunk)
        def _(i):
            off = base + i * chunk
            pltpu.sync_copy(idx_hbm.at[pl.ds(off, chunk)], idx_v)
            pltpu.async_copy(table_hbm.at[idx_v], rows_v, sem).wait()
            pltpu.sync_copy(rows_v, out_hbm.at[pl.ds(off, chunk)])

    return pl.kernel(
        body,
        out_type=jax.ShapeDtypeStruct((n_rows, width), table.dtype),
        mesh=mesh,
        scratch_types=[pltpu.VMEM((chunk,), jnp.int32), pltpu.VMEM((chunk, width), table.dtype),
                       pltpu.SemaphoreType.DMA],
        name="sc_gather",
    )(table, idx)


def _sc_scatter_pairs(table, pos):
    n, width = table.shape
    workers = V7X_SC_CORES * V7X_SC_SUBCORES
    chunk = min(SC_MAX_INDEX_VECTOR, SC_GATHER_BYTES // (width * table.dtype.itemsize))
    assert table.dtype.itemsize == 4 and n % (workers * chunk) == 0 and chunk % 8 == 0
    per_worker = n // workers
    mesh = plsc.VectorSubcoreMesh(core_axis_name="c", subcore_axis_name="s",
                                  num_cores=V7X_SC_CORES, num_subcores=V7X_SC_SUBCORES)

    def body(table_hbm, pos_hbm, out_hbm, idx_a, idx_b, rows_v, sem):
        base = (lax.axis_index("s") * V7X_SC_CORES + lax.axis_index("c")) * per_worker

        @pl.loop(0, per_worker // chunk)
        def _(i):
            off = base + i * chunk
            pltpu.sync_copy(table_hbm.at[pl.ds(off, chunk)], rows_v)
            pltpu.sync_copy(pos_hbm.at[pl.ds(off, chunk)], idx_a)
            pltpu.sync_copy(pos_hbm.at[pl.ds(n + off, chunk)], idx_b)
            pltpu.async_copy(rows_v, out_hbm.at[idx_a], sem).wait()
            pltpu.async_copy(rows_v, out_hbm.at[idx_b], sem).wait()

    return pl.kernel(
        body,
        out_type=jax.ShapeDtypeStruct((2 * n, width), table.dtype),
        mesh=mesh,
        scratch_types=[pltpu.VMEM((chunk,), jnp.int32), pltpu.VMEM((chunk,), jnp.int32),
                       pltpu.VMEM((chunk, width), table.dtype), pltpu.SemaphoreType.DMA],
        name="sc_scatter_pairs",
    )(table, pos)


def _swiglu_chunk(h, w1, w3, w2):
    a = jnp.dot(h, w1, preferred_element_type=F32)
    b = jnp.dot(h, w3, preferred_element_type=F32)
    t = (a * jax.nn.sigmoid(a) * b).astype(BF16)
    return jnp.dot(t, w2, preferred_element_type=F32)


def _ffn_kernel(x_ref, g_ref, w1_ref, w3_ref, w2_ref, fg_ref, o_ref, *, fc, final_norm):
    d_ff = w1_ref.shape[1]
    x = x_ref[...]
    h = _rms(x, g_ref[...], RMS_EPS).astype(BF16)
    acc = None
    for c0 in range(0, d_ff, fc):
        c1 = min(c0 + fc, d_ff)
        y = _swiglu_chunk(h, w1_ref[:, c0:c1], w3_ref[:, c0:c1], w2_ref[c0:c1, :])
        acc = y if acc is None else acc + y
    out = x + acc
    if final_norm:
        out = _rms(out, fg_ref[...], RMS_EPS)
    o_ref[...] = out


def _ffn(x2, g, w1, w3, w2, final_g, tm, fc, final_norm):
    n, d = x2.shape
    resident = lambda a: pl.BlockSpec(a.shape, lambda i: (0,) * a.ndim, pipeline_mode=pl.Buffered(1))
    return pl.pallas_call(
        functools.partial(_ffn_kernel, fc=fc, final_norm=final_norm),
        grid=(n // tm,),
        in_specs=[pl.BlockSpec((tm, d), lambda i: (i, 0)),
                  pl.BlockSpec((1, d), lambda i: (0, 0)),
                  resident(w1), resident(w3), resident(w2),
                  pl.BlockSpec((1, d), lambda i: (0, 0))],
        out_specs=pl.BlockSpec((tm, d), lambda i: (i, 0)),
        out_shape=jax.ShapeDtypeStruct((n, d), F32),
        compiler_params=_params("parallel"),
        name="dense_ffn",
    )(x2, g, w1, w3, w2, final_g)


EXPERT_SUBCHUNK = 512


def _expert_ffn_kernel(tile_ref, exp_ref, lo_ref, hi_ref, xs_ref, w1_ref, w3_ref, w2_ref, o_ref):
    it = pl.program_id(0)
    rows = o_ref.shape[0]
    lo, hi = lo_ref[it], hi_ref[it]

    @pl.when(hi > lo)
    def _():
        h = _unpack_bf16_pairs(xs_ref[...]).astype(BF16)
        d_ff = w1_ref.shape[2]
        y = None
        for c0 in range(0, d_ff, EXPERT_SUBCHUNK):
            c1 = min(c0 + EXPERT_SUBCHUNK, d_ff)
            part = _swiglu_chunk(h, w1_ref[0, :, c0:c1], w3_ref[0, :, c0:c1], w2_ref[0, c0:c1, :])
            y = part if y is None else y + part
        packed = _pack_bf16_pairs(y)
        first_row = lo - tile_ref[it] * rows

        @pl.when(first_row == 0)
        def _():
            o_ref[...] = packed

        @pl.when(first_row > 0)
        def _():
            row = lax.broadcasted_iota(jnp.int32, o_ref.shape, 0)
            o_ref[...] = jnp.where(row >= first_row, packed, o_ref[...])


def _expert_ffn(xs, items, w1, w3, w2, rows):
    n_pairs, half = xs.shape
    item_tile, item_expert, item_lo, item_hi = items
    expert_weights = lambda w: pl.BlockSpec((1,) + w.shape[1:], lambda i, t, e, lo, hi: (e[i], 0, 0),
                                            pipeline_mode=pl.Buffered(1))
    grid_spec = pltpu.PrefetchScalarGridSpec(
        num_scalar_prefetch=4,
        grid=(item_tile.shape[0],),
        in_specs=[pl.BlockSpec((rows, half), lambda i, t, e, lo, hi: (t[i], 0)),
                  expert_weights(w1), expert_weights(w3), expert_weights(w2)],
        out_specs=pl.BlockSpec((rows, half), lambda i, t, e, lo, hi: (t[i], 0)),
    )
    return pl.pallas_call(
        _expert_ffn_kernel,
        grid_spec=grid_spec,
        out_shape=jax.ShapeDtypeStruct((n_pairs, half), jnp.uint32),
        compiler_params=_params("arbitrary"),
        name="expert_ffn",
    )(item_tile, item_expert, item_lo, item_hi, xs, w1, w3, w2)


def _work_items(counts, n_pairs, rows):
    n_exp = counts.shape[0]
    n_tiles = n_pairs // rows
    ends = jnp.cumsum(counts)
    starts = ends - counts
    expert_cuts = starts[1:]
    expert_at = jnp.arange(n_exp - 1, dtype=jnp.int32) + jnp.minimum(expert_cuts // rows + 1, n_tiles)
    k = jnp.arange(n_tiles + n_exp - 1, dtype=jnp.int32)[:, None]
    experts_before = jnp.sum(expert_at[None, :] < k, axis=1, dtype=jnp.int32)
    is_expert_cut = expert_at[None, :] == k
    lo = jnp.where(jnp.any(is_expert_cut, axis=1), jnp.sum(jnp.where(is_expert_cut, expert_cuts[None, :], 0), axis=1),
                   (k[:, 0] - experts_before) * rows).astype(jnp.int32)
    hi = jnp.concatenate([lo[1:], jnp.full((1,), n_pairs, jnp.int32)])
    tile = jnp.minimum(lo // rows, n_tiles - 1)
    expert = jnp.minimum(jnp.sum(ends[None, :] <= lo[:, None], axis=1, dtype=jnp.int32), n_exp - 1)
    return (tile, expert, lo, hi), starts


COMBINE_PARTS = 2


def _combine_kernel(x_ref, y1_ref, y2_ref, w_ref, fg_ref, *rest, final_norm):
    o_ref = rest[-1]
    w = w_ref[...]
    out = x_ref[...] + (w[:, 0:1] * _unpack_bf16_pairs(y1_ref[...]) + w[:, 1:2] * _unpack_bf16_pairs(y2_ref[...]))
    if final_norm:
        out = _rms(out, fg_ref[...], RMS_EPS)
    o_ref[...] = out


def _combine_part(x2, y_pairs, wts, final_g, tm, final_norm, first_tile, previous):
    n, d = x2.shape
    nt = y_pairs.shape[0] // 2 // tm
    operands = [x2, y_pairs, y_pairs, wts, final_g]
    in_specs = [pl.BlockSpec((tm, d), lambda i: (i + first_tile, 0)),
                pl.BlockSpec((tm, d // 2), lambda i: (i, 0)),
                pl.BlockSpec((tm, d // 2), lambda i: (i + nt, 0)),
                pl.BlockSpec((tm, wts.shape[1]), lambda i: (i + first_tile, 0)),
                pl.BlockSpec((1, d), lambda i: (0, 0))]
    aliases = {}
    if previous is not None:
        operands.append(previous)
        in_specs.append(pl.BlockSpec(memory_space=pl.ANY))
        aliases = {len(operands) - 1: 0}
    return pl.pallas_call(
        functools.partial(_combine_kernel, final_norm=final_norm),
        grid=(nt,),
        in_specs=in_specs,
        out_specs=pl.BlockSpec((tm, d), lambda i: (i + first_tile, 0)),
        out_shape=jax.ShapeDtypeStruct((n, d), F32),
        input_output_aliases=aliases,
        compiler_params=_params("parallel"),
        name="moe_combine",
    )(*operands)


def _moe(x2, g_ffn, router, w1, w3, w2, final_g, tm, final_norm):
    n, d = x2.shape
    n_exp = router.shape[1]
    router_pad = jnp.pad(router, ((0, 0), (0, 128 - n_exp)))
    idx, wts, h_packed, seen = _router(x2, g_ffn, router_pad, n_exp, tm)

    rows = _pick_tile(2 * n, 512)
    counts = seen[0, :n_exp].astype(jnp.int32)
    items, starts = _work_items(counts, 2 * n, rows)
    pos = jnp.concatenate([starts[idx[0]] + idx[2], starts[idx[1]] + idx[3]])

    xs = _sc_scatter_pairs(h_packed, pos)
    ys = _expert_ffn(xs, items, w1, w3, w2, rows)
    out = None
    part = n // COMBINE_PARTS
    assert part % tm == 0
    for p in range(COMBINE_PARTS):
        part_pos = jnp.concatenate([pos[p * part:(p + 1) * part], pos[n + p * part:n + (p + 1) * part]])
        out = _combine_part(x2, _sc_gather(ys, part_pos), wts, final_g, tm, final_norm, p * (part // tm), out)
    return out


def _block_diag(w):
    g, c, _ = w.shape
    eye = jnp.eye(g, dtype=w.dtype)
    return (eye[:, None, :, None] * w[:, :, None, :]).reshape(g * c, g * c)


def _pick_tile(n, target):
    t = min(n, target)
    while n % t:
        t //= 2
    return t


def kernel(x, bias_table, mix_norm_g, w_in, conv_w, pool_w, pool_scale, diff_lambda, diff_subln_g,
           branch_proj, w_out, ffn_norm_g, dense_w1, dense_w3, dense_w2, moe_router, moe_w1, moe_w3,
           moe_w2, final_norm_g):
    batch, seq, d = x.shape
    depth = w_in.shape[0]
    n_mix = N_MIX_SLICES * BRANCH_WIDTH
    n = batch * seq
    assert seq % ATT_BLOCK == 0 and d % 128 == 0
    tm = _pick_tile(seq, 512)

    x2 = x.reshape(n, d)
    tiles = _bias_tiles(bias_table)
    row = lambda v: v.reshape(1, -1)
    final_g = row(final_norm_g)

    for i in range(depth):
        last_layer = i == depth - 1
        g_mix = row(mix_norm_g[i])
        w_mix = w_in[i, :, :n_mix].astype(BF16)
        w_gate = w_in[i, :, n_mix:].astype(BF16)
        u = _inproj(x2, g_mix, w_mix, tm)

        lam_init = 0.8 - 0.6 * math.exp(-0.3 * i)
        y_a = _moba(u, tiles, batch, seq)
        y_d = _diff(u, tiles, diff_lambda[i], row(jnp.tile(diff_subln_g[i], N_HEADS)), batch, seq, lam_init)
        x2 = _merge(x2, g_mix, w_gate, y_a, y_d, u, conv_w[i], _block_diag(pool_w[i]).astype(BF16),
                    row(pool_scale[i]), branch_proj[i].reshape(-1, d).astype(BF16), w_out[i].astype(BF16),
                    seq, tm)

        g_ffn = row(ffn_norm_g[i])
        j = i // 2
        if i % 2 == 0:
            x2 = _ffn(x2, g_ffn, dense_w1[j].astype(BF16), dense_w3[j].astype(BF16),
                      dense_w2[j].astype(BF16), final_g, tm, min(512, dense_w1.shape[2]), final_norm=last_layer)
        else:
            x2 = _moe(x2, g_ffn, moe_router[j], moe_w1[j].astype(BF16), moe_w3[j].astype(BF16),
                      moe_w2[j].astype(BF16), final_g, tm, final_norm=last_layer)

    return x2.reshape(batch, seq, d)
```

```python
import functools
import math

import numpy as np
import jax
import jax.numpy as jnp
from jax import lax
from jax.experimental import pallas as pl
from jax.experimental.pallas import tpu as pltpu
from jax.experimental.pallas import tpu_sc as plsc

F32 = jnp.float32
BF16 = jnp.bfloat16

BRANCH_WIDTH = 256
N_MIX_SLICES = 10
HEAD_WIDTH = 64
N_HEADS = 4
DIFF_QK_DIM = 32
ATT_BLOCK = 256
MOBA_TOPK = 3
CONV_WIDTH = 3
POOL_WINDOWS = (2, 4, 8, 16)
POOL_GROUP = 64
HALO = 16
REL_BUCKETS = 32
REL_MAX_DIST = 128
TOP_K_EXPERTS = 2
RMS_EPS = 1e-6
SUBLN_EPS = 1e-5
NEG_INF = -1e30
LOG2_E = math.log2(math.e)
V7X_VMEM_BYTES = 64 * 1024 * 1024
VMEM_LIMIT = V7X_VMEM_BYTES - 8 * 1024 * 1024

_TRANS_B = (((1,), (1,)), ((), ()))


def _params(*sem):
    return pltpu.CompilerParams(dimension_semantics=sem, vmem_limit_bytes=VMEM_LIMIT)


def _rms(x, g, eps):
    r = lax.rsqrt(jnp.mean(x * x, axis=-1, keepdims=True) + eps)
    return x * r * g


def _inproj_kernel(x_ref, g_ref, w_ref, o_ref):
    h = _rms(x_ref[...], g_ref[...], RMS_EPS).astype(BF16)
    o_ref[...] = jnp.dot(h, w_ref[...], preferred_element_type=F32)


def _inproj(x2, g, w, tm):
    n, d = x2.shape
    wn = w.shape[1]
    return pl.pallas_call(
        _inproj_kernel,
        grid=(n // tm,),
        in_specs=[pl.BlockSpec((tm, d), lambda i: (i, 0)),
                  pl.BlockSpec((1, d), lambda i: (0, 0)),
                  pl.BlockSpec((d, wn), lambda i: (0, 0))],
        out_specs=pl.BlockSpec((tm, wn), lambda i: (i, 0)),
        out_shape=jax.ShapeDtypeStruct((n, wn), F32),
        compiler_params=_params("parallel"),
        name="inproj",
    )(x2, g, w)


def _rel_bucket_np(dist):
    n = np.maximum(dist, 0)
    max_exact = REL_BUCKETS // 2
    nf = np.maximum(n, max_exact).astype(np.float32)
    large = max_exact + (np.log(nf / np.float32(max_exact)) / np.float32(math.log(REL_MAX_DIST / max_exact))
                         * np.float32(REL_BUCKETS - max_exact)).astype(np.int32)
    large = np.minimum(large, REL_BUCKETS - 1)
    return np.where(n < max_exact, n, large).astype(np.int32)


def _bucket_tiles():
    i = np.arange(ATT_BLOCK)[:, None]
    j = np.arange(ATT_BLOCK)[None, :]
    return np.stack([_rel_bucket_np(i - j), _rel_bucket_np(ATT_BLOCK + i - j)])


def _bias_tiles_kernel(tab_ref, bkt_ref, o_ref):
    h = pl.program_id(0)
    bkt = bkt_ref[...]
    acc = jnp.zeros(bkt.shape, F32)
    for b in range(REL_BUCKETS):
        acc = jnp.where(bkt == b, tab_ref[b, h], acc)
    o_ref[0] = (acc - tab_ref[REL_BUCKETS - 1, h]) * LOG2_E


def _bias_tiles(bias_table):
    n_heads = bias_table.shape[1]
    bkt = jnp.asarray(_bucket_tiles())
    return pl.pallas_call(
        _bias_tiles_kernel,
        grid=(n_heads,),
        in_specs=[pl.BlockSpec(memory_space=pltpu.SMEM),
                  pl.BlockSpec((2, ATT_BLOCK, ATT_BLOCK), lambda h: (0, 0, 0))],
        out_specs=pl.BlockSpec((1, 2, ATT_BLOCK, ATT_BLOCK), lambda h: (h, 0, 0, 0)),
        out_shape=jax.ShapeDtypeStruct((n_heads, 2, ATT_BLOCK, ATT_BLOCK), F32),
        compiler_params=_params("arbitrary"),
        name="bias_tiles",
    )(bias_table, bkt)


LANE_TILE = 128
SUBLANES = 8


def _head_lanes(lane, h):
    return (lane >= h * HEAD_WIDTH) & (lane < (h + 1) * HEAD_WIDTH)


def _head_to_low_lanes(x, h):
    assert 2 * HEAD_WIDTH == LANE_TILE
    tile = x[:, (h // 2) * LANE_TILE:(h // 2 + 1) * LANE_TILE]
    if h % 2:
        tile = pltpu.roll(tile, HEAD_WIDTH, axis=1)
    return jnp.where(lax.broadcasted_iota(jnp.int32, tile.shape, 1) < HEAD_WIDTH, tile, 0.0)


def _fold(op, s):
    out = s[:, :LANE_TILE]
    for c in range(LANE_TILE, s.shape[1], LANE_TILE):
        out = op(out, s[:, c:c + LANE_TILE])
    return out


def _stage_values(v_ref, vm_ref):
    t = ATT_BLOCK
    lane = lax.broadcasted_iota(jnp.int32, (t, BRANCH_WIDTH), 1)
    for j in range(vm_ref.shape[0]):
        vj = v_ref[j * t:(j + 1) * t, :]
        for h in range(N_HEADS):
            vm_ref[j, h * t:(h + 1) * t, :] = jnp.where(_head_lanes(lane, h), vj, 0.0).astype(BF16)


def _softmax_attend(qb, q16, head_of, key_slot, groups, kb_ref, vm_ref, tiles_ref, s_ref, m_ref, l_ref, acc_ref):
    t = ATT_BLOCK
    chains = range(len(q16))
    causal = (lax.broadcasted_iota(jnp.int32, (t, t), 0) >= lax.broadcasted_iota(jnp.int32, (t, t), 1))

    slots = sorted({key_slot(i) for i in chains})

    def keys(j):
        return {slot: kb_ref[slot, pl.ds(pl.multiple_of(j * t, t), t), :] for slot in slots}

    def score(i, kj):
        return lax.dot_general(q16[i], kj[key_slot(i)], _TRANS_B, preferred_element_type=F32)

    def in_pairs(count, fn):
        def pair(p, carry):
            fn([2 * p, 2 * p + 1])
            return carry

        lax.fori_loop(0, count >> 1, pair, 0)

        @pl.when((count & 1) == 1)
        def _():
            fn([count - 1])

    k_own = keys(qb)
    for i in chains:
        s = jnp.where(causal, score(i, k_own) + tiles_ref[head_of(i), 0], NEG_INF)
        s_ref[i, qb] = s
        m_ref[i] = _fold(jnp.maximum, s)

    @pl.when(qb >= 1)
    def _():
        k_prev = keys(qb - 1)
        for i in chains:
            s = score(i, k_prev) + tiles_ref[head_of(i), 1]
            s_ref[i, qb - 1] = s
            m_ref[i] = jnp.maximum(m_ref[i], _fold(jnp.maximum, s))

    def far(js):
        ks = [keys(j) for j in js]
        for i in chains:
            ss = [score(i, kj) for kj in ks]
            for j, s in zip(js, ss):
                s_ref[i, j] = s
            m_ref[i] = functools.reduce(jnp.maximum, [m_ref[i]] + [_fold(jnp.maximum, s) for s in ss])

    in_pairs(jnp.maximum(qb - 1, 0), far)

    for i in chains:
        m_ref[i] = jnp.broadcast_to(jnp.max(m_ref[i], axis=1, keepdims=True), (t, LANE_TILE))

    def accumulate(js, first=False):
        for g, members in enumerate(groups):
            parts = [[] for _ in js]
            for i in members:
                sums = []
                for a, j in enumerate(js):
                    s = s_ref[i, j]
                    p = [jnp.exp2(s[:, c:c + LANE_TILE] - m_ref[i]) for c in range(0, t, LANE_TILE)]
                    sums += p
                    parts[a] += [x.astype(BF16) for x in p]
                row_sum = functools.reduce(jnp.add, sums)
                l_ref[i] = row_sum if first else l_ref[i] + row_sum
            lhs = jnp.concatenate([x for tile_parts in parts for x in tile_parts], axis=1)
            rhs = jnp.concatenate([vm_ref[j] for j in js], axis=0) if len(js) > 1 else vm_ref[js[0]]
            pv = jnp.dot(lhs, rhs, preferred_element_type=F32)
            acc_ref[g] = pv if first else acc_ref[g] + pv

    accumulate([qb], first=True)
    in_pairs(qb, accumulate)


def _per_head_lanes(lane, cols):
    out = jnp.broadcast_to(cols[0], lane.shape)
    for h in range(1, N_HEADS):
        out = jnp.where(_head_lanes(lane, h), cols[h], out)
    return out


def _attention_scratch(seq, n_chains, n_groups, key_slots, key_width):
    t = ATT_BLOCK
    n_blk = seq // t
    return [pltpu.VMEM((key_slots, seq, key_width), BF16),
            pltpu.VMEM((n_blk, N_HEADS * t, BRANCH_WIDTH), BF16),
            pltpu.VMEM((n_chains, n_blk, t, t), F32),
            pltpu.VMEM((n_chains, t, LANE_TILE), F32),
            pltpu.VMEM((n_chains, t, LANE_TILE), F32),
            pltpu.VMEM((n_groups, t, BRANCH_WIDTH), F32)]


def _moba_kernel(q_ref, k_ref, v_ref, tiles_ref, o_ref, kb_ref, vm_ref, s_ref, m_ref, l_ref, acc_ref, kmean_ref):
    t = ATT_BLOCK
    n_blk = k_ref.shape[0] // t
    qb = pl.program_id(1)

    heads = range(N_HEADS)
    blk_rows = kmean_ref.shape[0] // N_HEADS
    assert n_blk <= blk_rows <= LANE_TILE

    assert HEAD_WIDTH + n_blk <= LANE_TILE
    low = lax.broadcasted_iota(jnp.int32, (t, LANE_TILE), 1)

    @pl.when(qb == 0)
    def _():
        _stage_values(v_ref, vm_ref)
        ch = lax.broadcasted_iota(jnp.int32, (1, BRANCH_WIDTH), 1)
        kmean_ref[...] = jnp.zeros(kmean_ref.shape, F32)
        for j in range(n_blk):
            kj = k_ref[j * t:(j + 1) * t, :]
            mean_j = jnp.mean(kj, axis=0, keepdims=True)
            for h in heads:
                kb_ref[h, j * t:(j + 1) * t, :] = jnp.where(low == HEAD_WIDTH + j, 1.0,
                                                            _head_to_low_lanes(kj, h)).astype(BF16)
                kmean_ref[h * blk_rows + j:h * blk_rows + j + 1, :] = jnp.where(_head_lanes(ch, h), mean_j, 0.0)

    lane = lax.broadcasted_iota(jnp.int32, (t, BRANCH_WIDTH), 1)
    q = q_ref[...]
    gate = lax.dot_general(kmean_ref[...], q * (HEAD_WIDTH ** -0.5), _TRANS_B, precision=lax.Precision.HIGHEST,
                           preferred_element_type=F32)
    blk = lax.broadcasted_iota(jnp.int32, (blk_rows, t), 0).astype(F32)
    drops = []
    for h in heads:
        g = gate[h * blk_rows:(h + 1) * blk_rows, :]
        avail = blk < qb.astype(F32)
        for _ in range(MOBA_TOPK):
            best = jnp.max(jnp.where(avail, g, -jnp.inf), axis=0, keepdims=True)
            first = jnp.min(jnp.where(avail & (g == best), blk, float(blk_rows)), axis=0, keepdims=True)
            avail = avail & (blk != first)
        drops += [jnp.zeros((HEAD_WIDTH, t), F32), jnp.where(avail, NEG_INF, 0.0),
                  jnp.zeros((LANE_TILE - HEAD_WIDTH - blk_rows, t), F32)]
    eye = (lax.broadcasted_iota(jnp.int32, (t, t), 0) == lax.broadcasted_iota(jnp.int32, (t, t), 1)).astype(BF16)
    drop = lax.dot_general(eye, jnp.concatenate(drops, axis=0).astype(BF16), _TRANS_B,
                           preferred_element_type=F32).astype(BF16)
    qf = q * (HEAD_WIDTH ** -0.5 * LOG2_E)
    q_aug = [jnp.where(low < HEAD_WIDTH, _head_to_low_lanes(qf, h).astype(BF16),
                       drop[:, h * LANE_TILE:(h + 1) * LANE_TILE]) for h in heads]

    _softmax_attend(qb, q_aug, lambda i: i, lambda i: i, [list(heads)], kb_ref, vm_ref, tiles_ref,
                    s_ref, m_ref, l_ref, acc_ref)
    row_sums = _per_head_lanes(lane, [jnp.sum(l_ref[h], axis=1, keepdims=True) for h in heads])
    o_ref[...] = (acc_ref[0] / row_sums).astype(o_ref.dtype)


def _moba(u, tiles, batch, seq):
    t = ATT_BLOCK
    nq = seq // t
    return pl.pallas_call(
        _moba_kernel,
        grid=(batch, nq),
        in_specs=[pl.BlockSpec((t, BRANCH_WIDTH), lambda b, q: (b * nq + q, 0)),
                  pl.BlockSpec((seq, BRANCH_WIDTH), lambda b, q: (b, 1)),
                  pl.BlockSpec((seq, BRANCH_WIDTH), lambda b, q: (b, 2)),
                  pl.BlockSpec((N_HEADS, 2, t, t), lambda b, q: (0, 0, 0, 0))],
        out_specs=pl.BlockSpec((t, BRANCH_WIDTH), lambda b, q: (b * nq + q, 0)),
        out_shape=jax.ShapeDtypeStruct((batch * seq, BRANCH_WIDTH), BF16),
        scratch_shapes=(_attention_scratch(seq, N_HEADS, 1, N_HEADS, LANE_TILE)
                        + [pltpu.VMEM((N_HEADS * SUBLANES * pl.cdiv(nq, SUBLANES), BRANCH_WIDTH), F32)]),
        compiler_params=_params("arbitrary", "arbitrary"),
        name="moba",
    )(u, u, u, tiles)


def _diff_kernel(lam_ref, g_ref, q_ref, k_ref, v_ref, tiles_ref, o_ref, kb_ref, vm_ref, s_ref, m_ref, l_ref,
                 acc_ref, *, lam_init):
    t = ATT_BLOCK
    qb = pl.program_id(1)

    @pl.when(qb == 0)
    def _():
        _stage_values(v_ref, vm_ref)
        kb_ref[0] = k_ref[...].astype(BF16)

    lp = lam_ref[...]
    lam = (jnp.exp(jnp.sum(lp[0:1] * lp[1:2], axis=1, keepdims=True))
           - jnp.exp(jnp.sum(lp[2:3] * lp[3:4], axis=1, keepdims=True)) + lam_init)

    lane = lax.broadcasted_iota(jnp.int32, (t, BRANCH_WIDTH), 1)
    qf = q_ref[...] * (DIFF_QK_DIM ** -0.5 * LOG2_E)
    q16 = []
    for h in range(N_HEADS):
        for c in range(2):
            lo = h * HEAD_WIDTH + c * DIFF_QK_DIM
            q16.append(jnp.where((lane >= lo) & (lane < lo + DIFF_QK_DIM), qf, 0.0).astype(BF16))
    groups = [[2 * h + c for h in range(N_HEADS)] for c in range(2)]
    _softmax_attend(qb, q16, lambda i: i // 2, lambda i: 0, groups, kb_ref, vm_ref, tiles_ref,
                    s_ref, m_ref, l_ref, acc_ref)

    row_sums = [_per_head_lanes(lane, [jnp.sum(l_ref[i], axis=1, keepdims=True) for i in members])
                for members in groups]
    o = acc_ref[0] / row_sums[0] - lam * (acc_ref[1] / row_sums[1])
    sq = o * o
    mean_sq = _per_head_lanes(lane, [jnp.sum(jnp.where(_head_lanes(lane, h), sq, 0.0), axis=1, keepdims=True)
                                     for h in range(N_HEADS)]) * (1.0 / HEAD_WIDTH)
    o_ref[...] = (o * lax.rsqrt(mean_sq + SUBLN_EPS) * g_ref[...] * (1.0 - lam_init)).astype(o_ref.dtype)


def _diff(u, tiles, lam_params, subln_g4, batch, seq, lam_init):
    t = ATT_BLOCK
    nq = seq // t
    return pl.pallas_call(
        functools.partial(_diff_kernel, lam_init=lam_init),
        grid=(batch, nq),
        in_specs=[pl.BlockSpec((4, DIFF_QK_DIM), lambda b, q: (0, 0)),
                  pl.BlockSpec((1, BRANCH_WIDTH), lambda b, q: (0, 0)),
                  pl.BlockSpec((t, BRANCH_WIDTH), lambda b, q: (b * nq + q, 7)),
                  pl.BlockSpec((seq, BRANCH_WIDTH), lambda b, q: (b, 8)),
                  pl.BlockSpec((seq, BRANCH_WIDTH), lambda b, q: (b, 9)),
                  pl.BlockSpec((N_HEADS, 2, t, t), lambda b, q: (1, 0, 0, 0))],
        out_specs=pl.BlockSpec((t, BRANCH_WIDTH), lambda b, q: (b * nq + q, 0)),
        out_shape=jax.ShapeDtypeStruct((batch * seq, BRANCH_WIDTH), BF16),
        scratch_shapes=_attention_scratch(seq, 2 * N_HEADS, 2, 1, BRANCH_WIDTH),
        compiler_params=_params("arbitrary", "arbitrary"),
        name="diff_attn",
    )(lam_params, subln_g4, u, u, u, tiles)


def _conv_pool_branches(xb_ref, bb_ref, cb_ref, pc_ref, xh_ref, ch_ref, ph_ref, cw_ref, pw_ref, ps_ref,
                        u_ref, s_ref, seq):
    tm = xb_ref.shape[0]
    pos0 = (pl.program_id(0) * tm) % seq
    has_history = pos0 > 0
    pos = pos0 + lax.broadcasted_iota(jnp.int32, (tm, 1), 0)
    lane = lax.broadcasted_iota(jnp.int32, (tm, BRANCH_WIDTH), 1)

    u_ref[0:HALO, :] = jnp.where(has_history, ch_ref[...] * xh_ref[...], 0.0)
    u_ref[HALO:, :] = cb_ref[...] * xb_ref[...]
    conv = cw_ref[CONV_WIDTH - 1:CONV_WIDTH, :] * u_ref[HALO:, :]
    for i in range(CONV_WIDTH - 1):
        shift = CONV_WIDTH - 1 - i
        conv = conv + cw_ref[i:i + 1, :] * u_ref[HALO - shift:HALO - shift + tm, :]
    y_b = (bb_ref[...] * conv).astype(BF16)

    s_ref[0:HALO, :] = jnp.where(has_history, ph_ref[...], 0.0)
    s_ref[HALO:, :] = pc_ref[...]
    pooled = jnp.zeros((tm, BRANCH_WIDTH), F32)
    done = 0
    for g, w in enumerate(POOL_WINDOWS):
        half = w // 2
        cur = s_ref[done + half:, :] + s_ref[done:HALO + tm - half, :]
        done += half
        s_ref[done:, :] = cur
        cnt = jnp.minimum(pos + 1, w).astype(F32)
        mean_w = s_ref[HALO:, :] / cnt
        pooled = jnp.where((lane >= g * POOL_GROUP) & (lane < (g + 1) * POOL_GROUP), mean_w, pooled)
    pooled = pooled - pc_ref[...]
    y_c = jnp.dot(pooled.astype(BF16), pw_ref[...], preferred_element_type=F32) * ps_ref[...]
    return y_b, y_c.astype(BF16)


def _merge_kernel(x_ref, g_ref, wg_ref, ya_ref, yd_ref, xb_ref, bb_ref, cb_ref, pc_ref, xh_ref, ch_ref, ph_ref,
                  cw_ref, pw_ref, ps_ref, bp_ref, wo_ref, o_ref, u_ref, s_ref, *, seq):
    d = x_ref.shape[1]
    x = x_ref[...]
    h = _rms(x, g_ref[...], RMS_EPS).astype(BF16)
    def gated(b, y):
        gate = jax.nn.sigmoid(jnp.dot(h, wg_ref[:, b * d:(b + 1) * d], preferred_element_type=F32))
        return gate * jnp.dot(y, bp_ref[b * BRANCH_WIDTH:(b + 1) * BRANCH_WIDTH, :], preferred_element_type=F32)

    merged = gated(0, ya_ref[...]) + gated(3, yd_ref[...])
    y_b, y_c = _conv_pool_branches(xb_ref, bb_ref, cb_ref, pc_ref, xh_ref, ch_ref, ph_ref, cw_ref, pw_ref, ps_ref,
                                   u_ref, s_ref, seq)
    merged = merged + gated(1, y_b) + gated(2, y_c)
    o_ref[...] = x + jnp.dot(merged.astype(BF16), wo_ref[...], preferred_element_type=F32)


def _merge(x2, g, w_gate, y_a, y_d, u, conv_w, pool_w_bd, pool_scale, bp, w_out, seq, tm):
    n, d = x2.shape
    full = lambda a: pl.BlockSpec(a.shape, lambda i: (0,) * a.ndim)
    resident = lambda a: pl.BlockSpec(a.shape, lambda i: (0,) * a.ndim, pipeline_mode=pl.Buffered(1))
    rows = lambda a: pl.BlockSpec((tm, a.shape[1]), lambda i: (i, 0))
    u_rows = lambda c: pl.BlockSpec((tm, BRANCH_WIDTH), lambda i, c=c: (i, c))
    u_halo = lambda c: pl.BlockSpec((HALO, BRANCH_WIDTH),
                                    lambda i, c=c: (jnp.maximum(i * (tm // HALO) - 1, 0), c))
    return pl.pallas_call(
        functools.partial(_merge_kernel, seq=seq),
        grid=(n // tm,),
        in_specs=[rows(x2), full(g), resident(w_gate), rows(y_a), rows(y_d),
                  u_rows(3), u_rows(4), u_rows(5), u_rows(6), u_halo(3), u_halo(5), u_halo(6),
                  full(conv_w), full(pool_w_bd), full(pool_scale), resident(bp), resident(w_out)],
        out_specs=pl.BlockSpec((tm, d), lambda i: (i, 0)),
        out_shape=jax.ShapeDtypeStruct((n, d), F32),
        scratch_shapes=[pltpu.VMEM((HALO + tm, BRANCH_WIDTH), F32),
                        pltpu.VMEM((HALO + tm, BRANCH_WIDTH), F32)],
        compiler_params=_params("parallel"),
        name="merge",
    )(x2, g, w_gate, y_a, y_d, u, u, u, u, u, u, u, conv_w, pool_w_bd, pool_scale, bp, w_out)


ROUTE_COLS = 8


def _pack_bf16_pairs(h):
    c = h.shape[1] // 2
    bits = lax.bitcast_convert_type(h.astype(BF16).astype(F32), jnp.uint32)
    return (bits[:, :c] >> 16) | (bits[:, c:] & jnp.uint32(0xFFFF0000))


def _unpack_bf16_pairs(w):
    lo = lax.bitcast_convert_type(w << 16, F32)
    hi = lax.bitcast_convert_type(w & jnp.uint32(0xFFFF0000), F32)
    return jnp.concatenate([lo, hi], axis=1)


def _router_kernel(x_ref, g_ref, r_ref, tri_ref, idx_ref, wts_ref, hpk_ref, cnt_ref, seen_ref, *, n_exp):
    @pl.when(pl.program_id(0) == 0)
    def _():
        seen_ref[...] = jnp.zeros(seen_ref.shape, F32)

    h = _rms(x_ref[...], g_ref[...], RMS_EPS)
    h_hi = h.astype(BF16)
    h_lo = (h - h_hi.astype(F32)).astype(BF16)
    r = r_ref[...]
    r_hi = r.astype(BF16)
    r_lo = (r - r_hi.astype(F32)).astype(BF16)
    logits = (jnp.dot(h_hi, r_hi, preferred_element_type=F32) + jnp.dot(h_hi, r_lo, preferred_element_type=F32)
              + jnp.dot(h_lo, r_hi, preferred_element_type=F32))
    lane = lax.broadcasted_iota(jnp.int32, logits.shape, 1).astype(F32)
    logits = jnp.where(lane < n_exp, logits, -jnp.inf)
    big = float(logits.shape[1])
    m1 = jnp.max(logits, axis=1, keepdims=True)
    i1 = jnp.min(jnp.where(logits == m1, lane, big), axis=1, keepdims=True)
    rest = jnp.where(lane == i1, -jnp.inf, logits)
    m2 = jnp.max(rest, axis=1, keepdims=True)
    i2 = jnp.min(jnp.where(rest == m2, lane, big), axis=1, keepdims=True)
    e2 = jnp.exp(m2 - m1)
    w1 = 1.0 / (1.0 + e2)
    w2 = e2 / (1.0 + e2)

    chosen = (lane == i1) | (lane == i2)
    before = seen_ref[...] + jnp.dot(tri_ref[...], chosen.astype(BF16), preferred_element_type=F32)
    r1 = jnp.sum(jnp.where(lane == i1, before, 0.0), axis=1, keepdims=True)
    r2 = jnp.sum(jnp.where(lane == i2, before, 0.0), axis=1, keepdims=True)
    seen_ref[...] += jnp.sum(chosen.astype(F32), axis=0, keepdims=True)
    cnt_ref[...] = seen_ref[...]

    pick = lambda col, a, b, c, d: jnp.where(col == 0, a, jnp.where(col == 1, b, jnp.where(col == 2, c, d)))
    record = pick(lane, i1, i2, r1, jnp.where(lane == 3, r2, 0.0))
    idx_ref[...] = record.T[0:idx_ref.shape[0], :].astype(jnp.int32)
    wts_ref[...] = pick(lax.broadcasted_iota(jnp.int32, wts_ref.shape, 1), w1, w2, 0.0, 0.0)
    hpk_ref[...] = _pack_bf16_pairs(h)


def _router(x2, g, router_pad, n_exp, tm):
    n, d = x2.shape
    tri = jnp.asarray(np.tril(np.ones((tm, tm), np.float32), -1), BF16)
    full = lambda a: pl.BlockSpec(a.shape, lambda i: (0,) * a.ndim)
    rows = lambda w: pl.BlockSpec((tm, w), lambda i: (i, 0))
    return pl.pallas_call(
        functools.partial(_router_kernel, n_exp=n_exp),
        grid=(n // tm,),
        in_specs=[rows(d), full(g), full(router_pad), full(tri)],
        out_specs=[pl.BlockSpec((ROUTE_COLS, tm), lambda i: (0, i)), rows(ROUTE_COLS), rows(d // 2),
                   pl.BlockSpec((1, router_pad.shape[1]), lambda i: (0, 0))],
        out_shape=[jax.ShapeDtypeStruct((ROUTE_COLS, n), jnp.int32),
                   jax.ShapeDtypeStruct((n, ROUTE_COLS), F32),
                   jax.ShapeDtypeStruct((n, d // 2), jnp.uint32),
                   jax.ShapeDtypeStruct((1, router_pad.shape[1]), F32)],
        scratch_shapes=[pltpu.VMEM((1, router_pad.shape[1]), F32)],
        compiler_params=_params("arbitrary"),
        name="router",
    )(x2, g, router_pad, tri)


V7X_SC_CORES = 2
V7X_SC_SUBCORES = 16
SC_GATHER_BYTES = 256 * 1024
SC_MAX_INDEX_VECTOR = 128


def _sc_gather(table, idx):
    n_rows, width = idx.shape[0], table.shape[1]
    workers = V7X_SC_CORES * V7X_SC_SUBCORES
    chunk = min(SC_MAX_INDEX_VECTOR, SC_GATHER_BYTES // (width * table.dtype.itemsize))
    assert table.dtype.itemsize == 4 and n_rows % (workers * chunk) == 0 and chunk % 8 == 0
    per_worker = n_rows // workers
    mesh = plsc.VectorSubcoreMesh(core_axis_name="c", subcore_axis_name="s",
                                  num_cores=V7X_SC_CORES, num_subcores=V7X_SC_SUBCORES)

    def body(table_hbm, idx_hbm, out_hbm, idx_v, rows_v, sem):
        base = (lax.axis_index("s") * V7X_SC_CORES + lax.axis_index("c")) * per_worker

        @pl.loop(0, per_worker // chunk)
        def _(i):
            off = base + i * chunk
            pltpu.sync_copy(idx_hbm.at[pl.ds(off, chunk)], idx_v)
            pltpu.async_copy(table_hbm.at[idx_v], rows_v, sem).wait()
            pltpu.sync_copy(rows_v, out_hbm.at[pl.ds(off, chunk)])

    return pl.kernel(
        body,
        out_type=jax.ShapeDtypeStruct((n_rows, width), table.dtype),
        mesh=mesh,
        scratch_types=[pltpu.VMEM((chunk,), jnp.int32), pltpu.VMEM((chunk, width), table.dtype),
                       pltpu.SemaphoreType.DMA],
        name="sc_gather",
    )(table, idx)


def _sc_scatter_pairs(table, pos):
    n, width = table.shape
    workers = V7X_SC_CORES * V7X_SC_SUBCORES
    chunk = min(SC_MAX_INDEX_VECTOR, SC_GATHER_BYTES // (width * table.dtype.itemsize))
    assert table.dtype.itemsize == 4 and n % (workers * chunk) == 0 and chunk % 8 == 0
    per_worker = n // workers
    mesh = plsc.VectorSubcoreMesh(core_axis_name="c", subcore_axis_name="s",
                                  num_cores=V7X_SC_CORES, num_subcores=V7X_SC_SUBCORES)

    def body(table_hbm, pos_hbm, out_hbm, idx_a, idx_b, rows_v, sem):
        base = (lax.axis_index("s") * V7X_SC_CORES + lax.axis_index("c")) * per_worker

        @pl.loop(0, per_worker // chunk)
        def _(i):
            off = base + i * chunk
            pltpu.sync_copy(table_hbm.at[pl.ds(off, chunk)], rows_v)
            pltpu.sync_copy(pos_hbm.at[pl.ds(off, chunk)], idx_a)
            pltpu.sync_copy(pos_hbm.at[pl.ds(n + off, chunk)], idx_b)
            pltpu.async_copy(rows_v, out_hbm.at[idx_a], sem).wait()
            pltpu.async_copy(rows_v, out_hbm.at[idx_b], sem).wait()

    return pl.kernel(
        body,
        out_type=jax.ShapeDtypeStruct((2 * n, width), table.dtype),
        mesh=mesh,
        scratch_types=[pltpu.VMEM((chunk,), jnp.int32), pltpu.VMEM((chunk,), jnp.int32),
                       pltpu.VMEM((chunk, width), table.dtype), pltpu.SemaphoreType.DMA],
        name="sc_scatter_pairs",
    )(table, pos)


def _swiglu_chunk(h, w1, w3, w2):
    a = jnp.dot(h, w1, preferred_element_type=F32)
    b = jnp.dot(h, w3, preferred_element_type=F32)
    t = (a * jax.nn.sigmoid(a) * b).astype(BF16)
    return jnp.dot(t, w2, preferred_element_type=F32)


def _ffn_kernel(x_ref, g_ref, w1_ref, w3_ref, w2_ref, fg_ref, o_ref, *, fc, final_norm):
    d_ff = w1_ref.shape[1]
    x = x_ref[...]
    h = _rms(x, g_ref[...], RMS_EPS).astype(BF16)
    acc = None
    for c0 in range(0, d_ff, fc):
        c1 = min(c0 + fc, d_ff)
        y = _swiglu_chunk(h, w1_ref[:, c0:c1], w3_ref[:, c0:c1], w2_ref[c0:c1, :])
        acc = y if acc is None else acc + y
    out = x + acc
    if final_norm:
        out = _rms(out, fg_ref[...], RMS_EPS)
    o_ref[...] = out


def _ffn(x2, g, w1, w3, w2, final_g, tm, fc, final_norm):
    n, d = x2.shape
    resident = lambda a: pl.BlockSpec(a.shape, lambda i: (0,) * a.ndim, pipeline_mode=pl.Buffered(1))
    return pl.pallas_call(
        functools.partial(_ffn_kernel, fc=fc, final_norm=final_norm),
        grid=(n // tm,),
        in_specs=[pl.BlockSpec((tm, d), lambda i: (i, 0)),
                  pl.BlockSpec((1, d), lambda i: (0, 0)),
                  resident(w1), resident(w3), resident(w2),
                  pl.BlockSpec((1, d), lambda i: (0, 0))],
        out_specs=pl.BlockSpec((tm, d), lambda i: (i, 0)),
        out_shape=jax.ShapeDtypeStruct((n, d), F32),
        compiler_params=_params("parallel"),
        name="dense_ffn",
    )(x2, g, w1, w3, w2, final_g)


EXPERT_SUBCHUNK = 512


def _expert_ffn_kernel(tile_ref, exp_ref, lo_ref, hi_ref, xs_ref, w1_ref, w3_ref, w2_ref, o_ref):
    it = pl.program_id(0)
    rows = o_ref.shape[0]
    lo, hi = lo_ref[it], hi_ref[it]

    @pl.when(hi > lo)
    def _():
        h = _unpack_bf16_pairs(xs_ref[...]).astype(BF16)
        d_ff = w1_ref.shape[2]
        y = None
        for c0 in range(0, d_ff, EXPERT_SUBCHUNK):
            c1 = min(c0 + EXPERT_SUBCHUNK, d_ff)
            part = _swiglu_chunk(h, w1_ref[0, :, c0:c1], w3_ref[0, :, c0:c1], w2_ref[0, c0:c1, :])
            y = part if y is None else y + part
        packed = _pack_bf16_pairs(y)
        first_row = lo - tile_ref[it] * rows

        @pl.when(first_row == 0)
        def _():
            o_ref[...] = packed

        @pl.when(first_row > 0)
        def _():
            row = lax.broadcasted_iota(jnp.int32, o_ref.shape, 0)
            o_ref[...] = jnp.where(row >= first_row, packed, o_ref[...])


def _expert_ffn(xs, items, w1, w3, w2, rows):
    n_pairs, half = xs.shape
    item_tile, item_expert, item_lo, item_hi = items
    expert_weights = lambda w: pl.BlockSpec((1,) + w.shape[1:], lambda i, t, e, lo, hi: (e[i], 0, 0))
    grid_spec = pltpu.PrefetchScalarGridSpec(
        num_scalar_prefetch=4,
        grid=(item_tile.shape[0],),
        in_specs=[pl.BlockSpec((rows, half), lambda i, t, e, lo, hi: (t[i], 0)),
                  expert_weights(w1), expert_weights(w3), expert_weights(w2)],
        out_specs=pl.BlockSpec((rows, half), lambda i, t, e, lo, hi: (t[i], 0)),
    )
    return pl.pallas_call(
        _expert_ffn_kernel,
        grid_spec=grid_spec,
        out_shape=jax.ShapeDtypeStruct((n_pairs, half), jnp.uint32),
        compiler_params=_params("arbitrary"),
        name="expert_ffn",
    )(item_tile, item_expert, item_lo, item_hi, xs, w1, w3, w2)


def _work_items(counts, n_pairs, rows):
    n_exp = counts.shape[0]
    n_tiles = n_pairs // rows
    ends = jnp.cumsum(counts)
    starts = ends - counts
    expert_cuts = starts[1:]
    expert_at = jnp.arange(n_exp - 1, dtype=jnp.int32) + jnp.minimum(expert_cuts // rows + 1, n_tiles)
    k = jnp.arange(n_tiles + n_exp - 1, dtype=jnp.int32)[:, None]
    experts_before = jnp.sum(expert_at[None, :] < k, axis=1, dtype=jnp.int32)
    is_expert_cut = expert_at[None, :] == k
    lo = jnp.where(jnp.any(is_expert_cut, axis=1), jnp.sum(jnp.where(is_expert_cut, expert_cuts[None, :], 0), axis=1),
                   (k[:, 0] - experts_before) * rows).astype(jnp.int32)
    hi = jnp.concatenate([lo[1:], jnp.full((1,), n_pairs, jnp.int32)])
    tile = jnp.minimum(lo // rows, n_tiles - 1)
    expert = jnp.minimum(jnp.sum(ends[None, :] <= lo[:, None], axis=1, dtype=jnp.int32), n_exp - 1)
    return (tile, expert, lo, hi), starts


def _combine_kernel(x_ref, y1_ref, y2_ref, w_ref, fg_ref, o_ref, *, final_norm):
    w = w_ref[...]
    out = x_ref[...] + (w[:, 0:1] * _unpack_bf16_pairs(y1_ref[...]) + w[:, 1:2] * _unpack_bf16_pairs(y2_ref[...]))
    if final_norm:
        out = _rms(out, fg_ref[...], RMS_EPS)
    o_ref[...] = out


def _combine(x2, y_pairs, wts, final_g, tm, final_norm):
    n, d = x2.shape
    nt = n // tm
    return pl.pallas_call(
        functools.partial(_combine_kernel, final_norm=final_norm),
        grid=(nt,),
        in_specs=[pl.BlockSpec((tm, d), lambda i: (i, 0)),
                  pl.BlockSpec((tm, d // 2), lambda i: (i, 0)),
                  pl.BlockSpec((tm, d // 2), lambda i: (i + nt, 0)),
                  pl.BlockSpec((tm, wts.shape[1]), lambda i: (i, 0)),
                  pl.BlockSpec((1, d), lambda i: (0, 0))],
        out_specs=pl.BlockSpec((tm, d), lambda i: (i, 0)),
        out_shape=jax.ShapeDtypeStruct((n, d), F32),
        compiler_params=_params("parallel"),
        name="moe_combine",
    )(x2, y_pairs, y_pairs, wts, final_g)


def _moe(x2, g_ffn, router, w1, w3, w2, final_g, tm, final_norm):
    n, d = x2.shape
    n_exp = router.shape[1]
    router_pad = jnp.pad(router, ((0, 0), (0, 128 - n_exp)))
    idx, wts, h_packed, seen = _router(x2, g_ffn, router_pad, n_exp, tm)

    rows = _pick_tile(2 * n, 512)
    counts = seen[0, :n_exp].astype(jnp.int32)
    items, starts = _work_items(counts, 2 * n, rows)
    pos = jnp.concatenate([starts[idx[0]] + idx[2], starts[idx[1]] + idx[3]])

    xs = _sc_scatter_pairs(h_packed, pos)
    ys = _expert_ffn(xs, items, w1, w3, w2, rows)
    y_pairs = _sc_gather(ys, pos)
    return _combine(x2, y_pairs, wts, final_g, tm, final_norm)


def _block_diag(w):
    g, c, _ = w.shape
    eye = jnp.eye(g, dtype=w.dtype)
    return (eye[:, None, :, None] * w[:, :, None, :]).reshape(g * c, g * c)


def _pick_tile(n, target):
    t = min(n, target)
    while n % t:
        t //= 2
    return t


def kernel(x, bias_table, mix_norm_g, w_in, conv_w, pool_w, pool_scale, diff_lambda, diff_subln_g,
           branch_proj, w_out, ffn_norm_g, dense_w1, dense_w3, dense_w2, moe_router, moe_w1, moe_w3,
           moe_w2, final_norm_g):
    batch, seq, d = x.shape
    depth = w_in.shape[0]
    n_mix = N_MIX_SLICES * BRANCH_WIDTH
    n = batch * seq
    assert seq % ATT_BLOCK == 0 and d % 128 == 0
    tm = _pick_tile(seq, 512)

    x2 = x.reshape(n, d)
    tiles = _bias_tiles(bias_table)
    row = lambda v: v.reshape(1, -1)
    final_g = row(final_norm_g)

    for i in range(depth):
        last_layer = i == depth - 1
        g_mix = row(mix_norm_g[i])
        w_mix = w_in[i, :, :n_mix].astype(BF16)
        w_gate = w_in[i, :, n_mix:].astype(BF16)
        u = _inproj(x2, g_mix, w_mix, _pick_tile(n, 2 * tm))

        lam_init = 0.8 - 0.6 * math.exp(-0.3 * i)
        y_a = _moba(u, tiles, batch, seq)
        y_d = _diff(u, tiles, diff_lambda[i], row(jnp.tile(diff_subln_g[i], N_HEADS)), batch, seq, lam_init)
        x2 = _merge(x2, g_mix, w_gate, y_a, y_d, u, conv_w[i], _block_diag(pool_w[i]).astype(BF16),
                    row(pool_scale[i]), branch_proj[i].reshape(-1, d).astype(BF16), w_out[i].astype(BF16),
                    seq, tm)

        g_ffn = row(ffn_norm_g[i])
        j = i // 2
        if i % 2 == 0:
            x2 = _ffn(x2, g_ffn, dense_w1[j].astype(BF16), dense_w3[j].astype(BF16),
                      dense_w2[j].astype(BF16), final_g, tm, min(512, dense_w1.shape[2]), final_norm=last_layer)
        else:
            x2 = _moe(x2, g_ffn, moe_router[j], moe_w1[j].astype(BF16), moe_w3[j].astype(BF16),
                      moe_w2[j].astype(BF16), final_g, tm, final_norm=last_layer)

    return x2.reshape(batch, seq, d)
```

```python
import functools
import math

import numpy as np
import jax
import jax.numpy as jnp
from jax import lax
from jax.experimental import pallas as pl
from jax.experimental.pallas import tpu as pltpu
from jax.experimental.pallas import tpu_sc as plsc

F32 = jnp.float32
BF16 = jnp.bfloat16

BRANCH_WIDTH = 256
N_MIX_SLICES = 10
HEAD_WIDTH = 64
N_HEADS = 4
DIFF_QK_DIM = 32
ATT_BLOCK = 256
MOBA_TOPK = 3
CONV_WIDTH = 3
POOL_WINDOWS = (2, 4, 8, 16)
POOL_GROUP = 64
HALO = 16
REL_BUCKETS = 32
REL_MAX_DIST = 128
TOP_K_EXPERTS = 2
RMS_EPS = 1e-6
SUBLN_EPS = 1e-5
NEG_INF = -1e30
LOG2_E = math.log2(math.e)
V7X_VMEM_BYTES = 64 * 1024 * 1024
VMEM_LIMIT = V7X_VMEM_BYTES - 8 * 1024 * 1024

_TRANS_B = (((1,), (1,)), ((), ()))


def _params(*sem):
    return pltpu.CompilerParams(dimension_semantics=sem, vmem_limit_bytes=VMEM_LIMIT)


def _rms(x, g, eps):
    r = lax.rsqrt(jnp.mean(x * x, axis=-1, keepdims=True) + eps)
    return x * r * g


def _inproj_kernel(x_ref, g_ref, w_ref, o_ref):
    h = _rms(x_ref[...], g_ref[...], RMS_EPS).astype(BF16)
    o_ref[...] = jnp.dot(h, w_ref[...], preferred_element_type=F32)


def _inproj(x2, g, w, tm):
    n, d = x2.shape
    wn = w.shape[1]
    return pl.pallas_call(
        _inproj_kernel,
        grid=(n // tm,),
        in_specs=[pl.BlockSpec((tm, d), lambda i: (i, 0)),
                  pl.BlockSpec((1, d), lambda i: (0, 0)),
                  pl.BlockSpec((d, wn), lambda i: (0, 0))],
        out_specs=pl.BlockSpec((tm, wn), lambda i: (i, 0)),
        out_shape=jax.ShapeDtypeStruct((n, wn), F32),
        compiler_params=_params("parallel"),
        name="inproj",
    )(x2, g, w)


def _rel_bucket_np(dist):
    n = np.maximum(dist, 0)
    max_exact = REL_BUCKETS // 2
    nf = np.maximum(n, max_exact).astype(np.float32)
    large = max_exact + (np.log(nf / np.float32(max_exact)) / np.float32(math.log(REL_MAX_DIST / max_exact))
                         * np.float32(REL_BUCKETS - max_exact)).astype(np.int32)
    large = np.minimum(large, REL_BUCKETS - 1)
    return np.where(n < max_exact, n, large).astype(np.int32)


def _bucket_tiles():
    i = np.arange(ATT_BLOCK)[:, None]
    j = np.arange(ATT_BLOCK)[None, :]
    return np.stack([_rel_bucket_np(i - j), _rel_bucket_np(ATT_BLOCK + i - j)])


def _bias_tiles_kernel(tab_ref, bkt_ref, o_ref):
    h = pl.program_id(0)
    bkt = bkt_ref[...]
    acc = jnp.zeros(bkt.shape, F32)
    for b in range(REL_BUCKETS):
        acc = jnp.where(bkt == b, tab_ref[b, h], acc)
    o_ref[0] = (acc - tab_ref[REL_BUCKETS - 1, h]) * LOG2_E


def _bias_tiles(bias_table):
    n_heads = bias_table.shape[1]
    bkt = jnp.asarray(_bucket_tiles())
    return pl.pallas_call(
        _bias_tiles_kernel,
        grid=(n_heads,),
        in_specs=[pl.BlockSpec(memory_space=pltpu.SMEM),
                  pl.BlockSpec((2, ATT_BLOCK, ATT_BLOCK), lambda h: (0, 0, 0))],
        out_specs=pl.BlockSpec((1, 2, ATT_BLOCK, ATT_BLOCK), lambda h: (h, 0, 0, 0)),
        out_shape=jax.ShapeDtypeStruct((n_heads, 2, ATT_BLOCK, ATT_BLOCK), F32),
        compiler_params=_params("arbitrary"),
        name="bias_tiles",
    )(bias_table, bkt)


LANE_TILE = 128
SUBLANES = 8


def _head_lanes(lane, h):
    return (lane >= h * HEAD_WIDTH) & (lane < (h + 1) * HEAD_WIDTH)


def _head_to_low_lanes(x, h):
    assert 2 * HEAD_WIDTH == LANE_TILE
    tile = x[:, (h // 2) * LANE_TILE:(h // 2 + 1) * LANE_TILE]
    if h % 2:
        tile = pltpu.roll(tile, HEAD_WIDTH, axis=1)
    return jnp.where(lax.broadcasted_iota(jnp.int32, tile.shape, 1) < HEAD_WIDTH, tile, 0.0)


def _fold(op, s):
    out = s[:, :LANE_TILE]
    for c in range(LANE_TILE, s.shape[1], LANE_TILE):
        out = op(out, s[:, c:c + LANE_TILE])
    return out


def _stage_values(v_ref, vm_ref):
    t = ATT_BLOCK
    lane = lax.broadcasted_iota(jnp.int32, (t, BRANCH_WIDTH), 1)
    for j in range(vm_ref.shape[0]):
        vj = v_ref[j * t:(j + 1) * t, :]
        for h in range(N_HEADS):
            vm_ref[j, h * t:(h + 1) * t, :] = jnp.where(_head_lanes(lane, h), vj, 0.0).astype(BF16)


def _softmax_attend(qb, q16, head_of, key_slot, groups, kb_ref, vm_ref, tiles_ref, s_ref, m_ref, l_ref, acc_ref):
    t = ATT_BLOCK
    chains = range(len(q16))
    causal = (lax.broadcasted_iota(jnp.int32, (t, t), 0) >= lax.broadcasted_iota(jnp.int32, (t, t), 1))

    slots = sorted({key_slot(i) for i in chains})

    def keys(j):
        return {slot: kb_ref[slot, pl.ds(pl.multiple_of(j * t, t), t), :] for slot in slots}

    def score(i, kj):
        return lax.dot_general(q16[i], kj[key_slot(i)], _TRANS_B, preferred_element_type=F32)

    def in_pairs(count, fn):
        def pair(p, carry):
            fn([2 * p, 2 * p + 1])
            return carry

        lax.fori_loop(0, count >> 1, pair, 0)

        @pl.when((count & 1) == 1)
        def _():
            fn([count - 1])

    k_own = keys(qb)
    for i in chains:
        s = jnp.where(causal, score(i, k_own) + tiles_ref[head_of(i), 0], NEG_INF)
        s_ref[i, qb] = s
        m_ref[i] = _fold(jnp.maximum, s)

    @pl.when(qb >= 1)
    def _():
        k_prev = keys(qb - 1)
        for i in chains:
            s = score(i, k_prev) + tiles_ref[head_of(i), 1]
            s_ref[i, qb - 1] = s
            m_ref[i] = jnp.maximum(m_ref[i], _fold(jnp.maximum, s))

    def far(js):
        ks = [keys(j) for j in js]
        for i in chains:
            ss = [score(i, kj) for kj in ks]
            for j, s in zip(js, ss):
                s_ref[i, j] = s
            m_ref[i] = functools.reduce(jnp.maximum, [m_ref[i]] + [_fold(jnp.maximum, s) for s in ss])

    in_pairs(jnp.maximum(qb - 1, 0), far)

    for i in chains:
        m_ref[i] = jnp.broadcast_to(jnp.max(m_ref[i], axis=1, keepdims=True), (t, LANE_TILE))

    def accumulate(js, first=False):
        for g, members in enumerate(groups):
            parts = [[] for _ in js]
            for i in members:
                sums = []
                for a, j in enumerate(js):
                    s = s_ref[i, j]
                    p = [jnp.exp2(s[:, c:c + LANE_TILE] - m_ref[i]) for c in range(0, t, LANE_TILE)]
                    sums += p
                    parts[a] += [x.astype(BF16) for x in p]
                row_sum = functools.reduce(jnp.add, sums)
                l_ref[i] = row_sum if first else l_ref[i] + row_sum
            lhs = jnp.concatenate([x for tile_parts in parts for x in tile_parts], axis=1)
            rhs = jnp.concatenate([vm_ref[j] for j in js], axis=0) if len(js) > 1 else vm_ref[js[0]]
            pv = jnp.dot(lhs, rhs, preferred_element_type=F32)
            acc_ref[g] = pv if first else acc_ref[g] + pv

    accumulate([qb], first=True)
    in_pairs(qb, accumulate)


def _per_head_lanes(lane, cols):
    out = jnp.broadcast_to(cols[0], lane.shape)
    for h in range(1, N_HEADS):
        out = jnp.where(_head_lanes(lane, h), cols[h], out)
    return out


def _attention_scratch(seq, n_chains, n_groups, key_slots, key_width):
    t = ATT_BLOCK
    n_blk = seq // t
    return [pltpu.VMEM((key_slots, seq, key_width), BF16),
            pltpu.VMEM((n_blk, N_HEADS * t, BRANCH_WIDTH), BF16),
            pltpu.VMEM((n_chains, n_blk, t, t), F32),
            pltpu.VMEM((n_chains, t, LANE_TILE), F32),
            pltpu.VMEM((n_chains, t, LANE_TILE), F32),
            pltpu.VMEM((n_groups, t, BRANCH_WIDTH), F32)]


def _moba_kernel(q_ref, k_ref, v_ref, tiles_ref, o_ref, kb_ref, vm_ref, s_ref, m_ref, l_ref, acc_ref, kmean_ref):
    t = ATT_BLOCK
    n_blk = k_ref.shape[0] // t
    qb = pl.program_id(1)

    heads = range(N_HEADS)
    blk_rows = kmean_ref.shape[0] // N_HEADS
    assert n_blk <= blk_rows <= LANE_TILE

    assert HEAD_WIDTH + n_blk <= LANE_TILE
    low = lax.broadcasted_iota(jnp.int32, (t, LANE_TILE), 1)

    @pl.when(qb == 0)
    def _():
        _stage_values(v_ref, vm_ref)
        ch = lax.broadcasted_iota(jnp.int32, (1, BRANCH_WIDTH), 1)
        kmean_ref[...] = jnp.zeros(kmean_ref.shape, F32)
        for j in range(n_blk):
            kj = k_ref[j * t:(j + 1) * t, :]
            mean_j = jnp.mean(kj, axis=0, keepdims=True)
            for h in heads:
                kb_ref[h, j * t:(j + 1) * t, :] = jnp.where(low == HEAD_WIDTH + j, 1.0,
                                                            _head_to_low_lanes(kj, h)).astype(BF16)
                kmean_ref[h * blk_rows + j:h * blk_rows + j + 1, :] = jnp.where(_head_lanes(ch, h), mean_j, 0.0)

    lane = lax.broadcasted_iota(jnp.int32, (t, BRANCH_WIDTH), 1)
    q = q_ref[...]
    gate = lax.dot_general(kmean_ref[...], q * (HEAD_WIDTH ** -0.5), _TRANS_B, precision=lax.Precision.HIGHEST,
                           preferred_element_type=F32)
    blk = lax.broadcasted_iota(jnp.int32, (blk_rows, t), 0).astype(F32)
    drops = []
    for h in heads:
        g = gate[h * blk_rows:(h + 1) * blk_rows, :]
        avail = blk < qb.astype(F32)
        for _ in range(MOBA_TOPK):
            best = jnp.max(jnp.where(avail, g, -jnp.inf), axis=0, keepdims=True)
            first = jnp.min(jnp.where(avail & (g == best), blk, float(blk_rows)), axis=0, keepdims=True)
            avail = avail & (blk != first)
        drops += [jnp.zeros((HEAD_WIDTH, t), F32), jnp.where(avail, NEG_INF, 0.0),
                  jnp.zeros((LANE_TILE - HEAD_WIDTH - blk_rows, t), F32)]
    eye = (lax.broadcasted_iota(jnp.int32, (t, t), 0) == lax.broadcasted_iota(jnp.int32, (t, t), 1)).astype(BF16)
    drop = lax.dot_general(eye, jnp.concatenate(drops, axis=0).astype(BF16), _TRANS_B,
                           preferred_element_type=F32).astype(BF16)
    qf = q * (HEAD_WIDTH ** -0.5 * LOG2_E)
    q_aug = [jnp.where(low < HEAD_WIDTH, _head_to_low_lanes(qf, h).astype(BF16),
                       drop[:, h * LANE_TILE:(h + 1) * LANE_TILE]) for h in heads]

    _softmax_attend(qb, q_aug, lambda i: i, lambda i: i, [list(heads)], kb_ref, vm_ref, tiles_ref,
                    s_ref, m_ref, l_ref, acc_ref)
    row_sums = _per_head_lanes(lane, [jnp.sum(l_ref[h], axis=1, keepdims=True) for h in heads])
    o_ref[...] = (acc_ref[0] / row_sums).astype(o_ref.dtype)


def _moba(u, tiles, batch, seq):
    t = ATT_BLOCK
    nq = seq // t
    return pl.pallas_call(
        _moba_kernel,
        grid=(batch, nq),
        in_specs=[pl.BlockSpec((t, BRANCH_WIDTH), lambda b, q: (b * nq + q, 0)),
                  pl.BlockSpec((seq, BRANCH_WIDTH), lambda b, q: (b, 1)),
                  pl.BlockSpec((seq, BRANCH_WIDTH), lambda b, q: (b, 2)),
                  pl.BlockSpec((N_HEADS, 2, t, t), lambda b, q: (0, 0, 0, 0))],
        out_specs=pl.BlockSpec((t, BRANCH_WIDTH), lambda b, q: (b * nq + q, 0)),
        out_shape=jax.ShapeDtypeStruct((batch * seq, BRANCH_WIDTH), BF16),
        scratch_shapes=(_attention_scratch(seq, N_HEADS, 1, N_HEADS, LANE_TILE)
                        + [pltpu.VMEM((N_HEADS * SUBLANES * pl.cdiv(nq, SUBLANES), BRANCH_WIDTH), F32)]),
        compiler_params=_params("arbitrary", "arbitrary"),
        name="moba",
    )(u, u, u, tiles)


def _diff_kernel(lam_ref, g_ref, q_ref, k_ref, v_ref, tiles_ref, o_ref, kb_ref, vm_ref, s_ref, m_ref, l_ref,
                 acc_ref, *, lam_init):
    t = ATT_BLOCK
    qb = pl.program_id(1)

    @pl.when(qb == 0)
    def _():
        _stage_values(v_ref, vm_ref)
        kb_ref[0] = k_ref[...].astype(BF16)

    lp = lam_ref[...]
    lam = (jnp.exp(jnp.sum(lp[0:1] * lp[1:2], axis=1, keepdims=True))
           - jnp.exp(jnp.sum(lp[2:3] * lp[3:4], axis=1, keepdims=True)) + lam_init)

    lane = lax.broadcasted_iota(jnp.int32, (t, BRANCH_WIDTH), 1)
    qf = q_ref[...] * (DIFF_QK_DIM ** -0.5 * LOG2_E)
    q16 = []
    for h in range(N_HEADS):
        for c in range(2):
            lo = h * HEAD_WIDTH + c * DIFF_QK_DIM
            q16.append(jnp.where((lane >= lo) & (lane < lo + DIFF_QK_DIM), qf, 0.0).astype(BF16))
    groups = [[2 * h + c for h in range(N_HEADS)] for c in range(2)]
    _softmax_attend(qb, q16, lambda i: i // 2, lambda i: 0, groups, kb_ref, vm_ref, tiles_ref,
                    s_ref, m_ref, l_ref, acc_ref)

    row_sums = [_per_head_lanes(lane, [jnp.sum(l_ref[i], axis=1, keepdims=True) for i in members])
                for members in groups]
    o = acc_ref[0] / row_sums[0] - lam * (acc_ref[1] / row_sums[1])
    sq = o * o
    mean_sq = _per_head_lanes(lane, [jnp.sum(jnp.where(_head_lanes(lane, h), sq, 0.0), axis=1, keepdims=True)
                                     for h in range(N_HEADS)]) * (1.0 / HEAD_WIDTH)
    o_ref[...] = (o * lax.rsqrt(mean_sq + SUBLN_EPS) * g_ref[...] * (1.0 - lam_init)).astype(o_ref.dtype)


def _diff(u, tiles, lam_params, subln_g4, batch, seq, lam_init):
    t = ATT_BLOCK
    nq = seq // t
    return pl.pallas_call(
        functools.partial(_diff_kernel, lam_init=lam_init),
        grid=(batch, nq),
        in_specs=[pl.BlockSpec((4, DIFF_QK_DIM), lambda b, q: (0, 0)),
                  pl.BlockSpec((1, BRANCH_WIDTH), lambda b, q: (0, 0)),
                  pl.BlockSpec((t, BRANCH_WIDTH), lambda b, q: (b * nq + q, 7)),
                  pl.BlockSpec((seq, BRANCH_WIDTH), lambda b, q: (b, 8)),
                  pl.BlockSpec((seq, BRANCH_WIDTH), lambda b, q: (b, 9)),
                  pl.BlockSpec((N_HEADS, 2, t, t), lambda b, q: (1, 0, 0, 0))],
        out_specs=pl.BlockSpec((t, BRANCH_WIDTH), lambda b, q: (b * nq + q, 0)),
        out_shape=jax.ShapeDtypeStruct((batch * seq, BRANCH_WIDTH), BF16),
        scratch_shapes=_attention_scratch(seq, 2 * N_HEADS, 2, 1, BRANCH_WIDTH),
        compiler_params=_params("arbitrary", "arbitrary"),
        name="diff_attn",
    )(lam_params, subln_g4, u, u, u, tiles)


def _conv_pool_branches(xb_ref, bb_ref, cb_ref, pc_ref, xh_ref, ch_ref, ph_ref, cw_ref, pw_ref, ps_ref,
                        u_ref, s_ref, seq):
    tm = xb_ref.shape[0]
    pos0 = (pl.program_id(0) * tm) % seq
    has_history = pos0 > 0
    pos = pos0 + lax.broadcasted_iota(jnp.int32, (tm, 1), 0)
    lane = lax.broadcasted_iota(jnp.int32, (tm, BRANCH_WIDTH), 1)

    u_ref[0:HALO, :] = jnp.where(has_history, ch_ref[...] * xh_ref[...], 0.0)
    u_ref[HALO:, :] = cb_ref[...] * xb_ref[...]
    conv = cw_ref[CONV_WIDTH - 1:CONV_WIDTH, :] * u_ref[HALO:, :]
    for i in range(CONV_WIDTH - 1):
        shift = CONV_WIDTH - 1 - i
        conv = conv + cw_ref[i:i + 1, :] * u_ref[HALO - shift:HALO - shift + tm, :]
    y_b = (bb_ref[...] * conv).astype(BF16)

    s_ref[0:HALO, :] = jnp.where(has_history, ph_ref[...], 0.0)
    s_ref[HALO:, :] = pc_ref[...]
    pooled = jnp.zeros((tm, BRANCH_WIDTH), F32)
    done = 0
    for g, w in enumerate(POOL_WINDOWS):
        half = w // 2
        cur = s_ref[done + half:, :] + s_ref[done:HALO + tm - half, :]
        done += half
        s_ref[done:, :] = cur
        cnt = jnp.minimum(pos + 1, w).astype(F32)
        mean_w = s_ref[HALO:, :] / cnt
        pooled = jnp.where((lane >= g * POOL_GROUP) & (lane < (g + 1) * POOL_GROUP), mean_w, pooled)
    pooled = pooled - pc_ref[...]
    y_c = jnp.dot(pooled.astype(BF16), pw_ref[...], preferred_element_type=F32) * ps_ref[...]
    return y_b, y_c.astype(BF16)


def _merge_kernel(x_ref, g_ref, wg_ref, ya_ref, yd_ref, xb_ref, bb_ref, cb_ref, pc_ref, xh_ref, ch_ref, ph_ref,
                  cw_ref, pw_ref, ps_ref, bp_ref, wo_ref, o_ref, u_ref, s_ref, *, seq):
    d = x_ref.shape[1]
    x = x_ref[...]
    h = _rms(x, g_ref[...], RMS_EPS).astype(BF16)
    def gated(b, y):
        gate = jax.nn.sigmoid(jnp.dot(h, wg_ref[:, b * d:(b + 1) * d], preferred_element_type=F32))
        return gate * jnp.dot(y, bp_ref[b * BRANCH_WIDTH:(b + 1) * BRANCH_WIDTH, :], preferred_element_type=F32)

    merged = gated(0, ya_ref[...]) + gated(3, yd_ref[...])
    y_b, y_c = _conv_pool_branches(xb_ref, bb_ref, cb_ref, pc_ref, xh_ref, ch_ref, ph_ref, cw_ref, pw_ref, ps_ref,
                                   u_ref, s_ref, seq)
    merged = merged + gated(1, y_b) + gated(2, y_c)
    o_ref[...] = x + jnp.dot(merged.astype(BF16), wo_ref[...], preferred_element_type=F32)


def _merge(x2, g, w_gate, y_a, y_d, u, conv_w, pool_w_bd, pool_scale, bp, w_out, seq, tm):
    n, d = x2.shape
    full = lambda a: pl.BlockSpec(a.shape, lambda i: (0,) * a.ndim)
    resident = lambda a: pl.BlockSpec(a.shape, lambda i: (0,) * a.ndim, pipeline_mode=pl.Buffered(1))
    rows = lambda a: pl.BlockSpec((tm, a.shape[1]), lambda i: (i, 0))
    u_rows = lambda c: pl.BlockSpec((tm, BRANCH_WIDTH), lambda i, c=c: (i, c))
    u_halo = lambda c: pl.BlockSpec((HALO, BRANCH_WIDTH),
                                    lambda i, c=c: (jnp.maximum(i * (tm // HALO) - 1, 0), c))
    return pl.pallas_call(
        functools.partial(_merge_kernel, seq=seq),
        grid=(n // tm,),
        in_specs=[rows(x2), full(g), resident(w_gate), rows(y_a), rows(y_d),
                  u_rows(3), u_rows(4), u_rows(5), u_rows(6), u_halo(3), u_halo(5), u_halo(6),
                  full(conv_w), full(pool_w_bd), full(pool_scale), resident(bp), resident(w_out)],
        out_specs=pl.BlockSpec((tm, d), lambda i: (i, 0)),
        out_shape=jax.ShapeDtypeStruct((n, d), F32),
        scratch_shapes=[pltpu.VMEM((HALO + tm, BRANCH_WIDTH), F32),
                        pltpu.VMEM((HALO + tm, BRANCH_WIDTH), F32)],
        compiler_params=_params("parallel"),
        name="merge",
    )(x2, g, w_gate, y_a, y_d, u, u, u, u, u, u, u, conv_w, pool_w_bd, pool_scale, bp, w_out)


ROUTE_COLS = 8


def _pack_bf16_pairs(h):
    c = h.shape[1] // 2
    bits = lax.bitcast_convert_type(h.astype(BF16).astype(F32), jnp.uint32)
    return (bits[:, :c] >> 16) | (bits[:, c:] & jnp.uint32(0xFFFF0000))


def _unpack_bf16_pairs(w):
    lo = lax.bitcast_convert_type(w << 16, F32)
    hi = lax.bitcast_convert_type(w & jnp.uint32(0xFFFF0000), F32)
    return jnp.concatenate([lo, hi], axis=1)


def _router_kernel(x_ref, g_ref, r_ref, tri_ref, idx_ref, wts_ref, hpk_ref, cnt_ref, seen_ref, *, n_exp):
    @pl.when(pl.program_id(0) == 0)
    def _():
        seen_ref[...] = jnp.zeros(seen_ref.shape, F32)

    h = _rms(x_ref[...], g_ref[...], RMS_EPS)
    h_hi = h.astype(BF16)
    h_lo = (h - h_hi.astype(F32)).astype(BF16)
    r = r_ref[...]
    r_hi = r.astype(BF16)
    r_lo = (r - r_hi.astype(F32)).astype(BF16)
    logits = (jnp.dot(h_hi, r_hi, preferred_element_type=F32) + jnp.dot(h_hi, r_lo, preferred_element_type=F32)
              + jnp.dot(h_lo, r_hi, preferred_element_type=F32))
    lane = lax.broadcasted_iota(jnp.int32, logits.shape, 1).astype(F32)
    logits = jnp.where(lane < n_exp, logits, -jnp.inf)
    big = float(logits.shape[1])
    m1 = jnp.max(logits, axis=1, keepdims=True)
    i1 = jnp.min(jnp.where(logits == m1, lane, big), axis=1, keepdims=True)
    rest = jnp.where(lane == i1, -jnp.inf, logits)
    m2 = jnp.max(rest, axis=1, keepdims=True)
    i2 = jnp.min(jnp.where(rest == m2, lane, big), axis=1, keepdims=True)
    e2 = jnp.exp(m2 - m1)
    w1 = 1.0 / (1.0 + e2)
    w2 = e2 / (1.0 + e2)

    chosen = (lane == i1) | (lane == i2)
    before = seen_ref[...] + jnp.dot(tri_ref[...], chosen.astype(BF16), preferred_element_type=F32)
    r1 = jnp.sum(jnp.where(lane == i1, before, 0.0), axis=1, keepdims=True)
    r2 = jnp.sum(jnp.where(lane == i2, before, 0.0), axis=1, keepdims=True)
    seen_ref[...] += jnp.sum(chosen.astype(F32), axis=0, keepdims=True)
    cnt_ref[...] = seen_ref[...]

    pick = lambda col, a, b, c, d: jnp.where(col == 0, a, jnp.where(col == 1, b, jnp.where(col == 2, c, d)))
    record = pick(lane, i1, i2, r1, jnp.where(lane == 3, r2, 0.0))
    idx_ref[...] = record.T[0:idx_ref.shape[0], :].astype(jnp.int32)
    wts_ref[...] = pick(lax.broadcasted_iota(jnp.int32, wts_ref.shape, 1), w1, w2, 0.0, 0.0)
    hpk_ref[...] = _pack_bf16_pairs(h)


def _router(x2, g, router_pad, n_exp, tm):
    n, d = x2.shape
    tri = jnp.asarray(np.tril(np.ones((tm, tm), np.float32), -1), BF16)
    full = lambda a: pl.BlockSpec(a.shape, lambda i: (0,) * a.ndim)
    rows = lambda w: pl.BlockSpec((tm, w), lambda i: (i, 0))
    return pl.pallas_call(
        functools.partial(_router_kernel, n_exp=n_exp),
        grid=(n // tm,),
        in_specs=[rows(d), full(g), full(router_pad), full(tri)],
        out_specs=[pl.BlockSpec((ROUTE_COLS, tm), lambda i: (0, i)), rows(ROUTE_COLS), rows(d // 2),
                   pl.BlockSpec((1, router_pad.shape[1]), lambda i: (0, 0))],
        out_shape=[jax.ShapeDtypeStruct((ROUTE_COLS, n), jnp.int32),
                   jax.ShapeDtypeStruct((n, ROUTE_COLS), F32),
                   jax.ShapeDtypeStruct((n, d // 2), jnp.uint32),
                   jax.ShapeDtypeStruct((1, router_pad.shape[1]), F32)],
        scratch_shapes=[pltpu.VMEM((1, router_pad.shape[1]), F32)],
        compiler_params=_params("arbitrary"),
        name="router",
    )(x2, g, router_pad, tri)


V7X_SC_CORES = 2
V7X_SC_SUBCORES = 16
SC_GATHER_BYTES = 256 * 1024
SC_MAX_INDEX_VECTOR = 128


def _sc_gather(table, idx):
    n_rows, width = idx.shape[0], table.shape[1]
    workers = V7X_SC_CORES * V7X_SC_SUBCORES
    chunk = min(SC_MAX_INDEX_VECTOR, SC_GATHER_BYTES // (width * table.dtype.itemsize))
    assert table.dtype.itemsize == 4 and n_rows % (workers * chunk) == 0 and chunk % 8 == 0
    per_worker = n_rows // workers
    mesh = plsc.VectorSubcoreMesh(core_axis_name="c", subcore_axis_name="s",
                                  num_cores=V7X_SC_CORES, num_subcores=V7X_SC_SUBCORES)

    def body(table_hbm, idx_hbm, out_hbm, idx_v, rows_v, sem):
        base = (lax.axis_index("s") * V7X_SC_CORES + lax.axis_index("c")) * per_worker

        @pl.loop(0, per_worker // chunk)
        def _(i):
            off = base + i * chunk
            pltpu.sync_copy(idx_hbm.at[pl.ds(off, chunk)], idx_v)
            pltpu.async_copy(table_hbm.at[idx_v], rows_v, sem).wait()
            pltpu.sync_copy(rows_v, out_hbm.at[pl.ds(off, chunk)])

    return pl.kernel(
        body,
        out_type=jax.ShapeDtypeStruct((n_rows, width), table.dtype),
        mesh=mesh,
        scratch_types=[pltpu.VMEM((chunk,), jnp.int32), pltpu.VMEM((chunk, width), table.dtype),
                       pltpu.SemaphoreType.DMA],
        name="sc_gather",
    )(table, idx)


def _sc_scatter_pairs(table, pos):
    n, width = table.shape
    workers = V7X_SC_CORES * V7X_SC_SUBCORES
    chunk = min(SC_MAX_INDEX_VECTOR, SC_GATHER_BYTES // (width * table.dtype.itemsize))
    assert table.dtype.itemsize == 4 and n % (workers * chunk) == 0 and chunk % 8 == 0
    per_worker = n // workers
    mesh = plsc.VectorSubcoreMesh(core_axis_name="c", subcore_axis_name="s",
                                  num_cores=V7X_SC_CORES, num_subcores=V7X_SC_SUBCORES)

    def body(table_hbm, pos_hbm, out_hbm, idx_a, idx_b, rows_v, sem):
        base = (lax.axis_index("s") * V7X_SC_CORES + lax.axis_index("c")) * per_worker

        @pl.loop(0, per_worker // chunk)
        def _(i):
            off = base + i * chunk
            pltpu.sync_copy(table_hbm.at[pl.ds(off, chunk)], rows_v)
            pltpu.sync_copy(pos_hbm.at[pl.ds(off, chunk)], idx_a)
            pltpu.sync_copy(pos_hbm.at[pl.ds(n + off, chunk)], idx_b)
            pltpu.async_copy(rows_v, out_hbm.at[idx_a], sem).wait()
            pltpu.async_copy(rows_v, out_hbm.at[idx_b], sem).wait()

    return pl.kernel(
        body,
        out_type=jax.ShapeDtypeStruct((2 * n, width), table.dtype),
        mesh=mesh,
        scratch_types=[pltpu.VMEM((chunk,), jnp.int32), pltpu.VMEM((chunk,), jnp.int32),
                       pltpu.VMEM((chunk, width), table.dtype), pltpu.SemaphoreType.DMA],
        name="sc_scatter_pairs",
    )(table, pos)


def _swiglu_chunk(h, w1, w3, w2):
    a = jnp.dot(h, w1, preferred_element_type=F32)
    b = jnp.dot(h, w3, preferred_element_type=F32)
    t = (a * jax.nn.sigmoid(a) * b).astype(BF16)
    return jnp.dot(t, w2, preferred_element_type=F32)


def _ffn_kernel(x_ref, g_ref, w1_ref, w3_ref, w2_ref, fg_ref, o_ref, *, fc, final_norm):
    d_ff = w1_ref.shape[1]
    x = x_ref[...]
    h = _rms(x, g_ref[...], RMS_EPS).astype(BF16)
    acc = None
    for c0 in range(0, d_ff, fc):
        c1 = min(c0 + fc, d_ff)
        y = _swiglu_chunk(h, w1_ref[:, c0:c1], w3_ref[:, c0:c1], w2_ref[c0:c1, :])
        acc = y if acc is None else acc + y
    out = x + acc
    if final_norm:
        out = _rms(out, fg_ref[...], RMS_EPS)
    o_ref[...] = out


def _ffn(x2, g, w1, w3, w2, final_g, tm, fc, final_norm):
    n, d = x2.shape
    resident = lambda a: pl.BlockSpec(a.shape, lambda i: (0,) * a.ndim, pipeline_mode=pl.Buffered(1))
    return pl.pallas_call(
        functools.partial(_ffn_kernel, fc=fc, final_norm=final_norm),
        grid=(n // tm,),
        in_specs=[pl.BlockSpec((tm, d), lambda i: (i, 0)),
                  pl.BlockSpec((1, d), lambda i: (0, 0)),
                  resident(w1), resident(w3), resident(w2),
                  pl.BlockSpec((1, d), lambda i: (0, 0))],
        out_specs=pl.BlockSpec((tm, d), lambda i: (i, 0)),
        out_shape=jax.ShapeDtypeStruct((n, d), F32),
        compiler_params=_params("parallel"),
        name="dense_ffn",
    )(x2, g, w1, w3, w2, final_g)


EXPERT_SUBCHUNK = 512


def _expert_ffn_kernel(tile_ref, exp_ref, lo_ref, hi_ref, xs_ref, w1_ref, w3_ref, w2_ref, o_ref):
    it = pl.program_id(0)
    rows = o_ref.shape[0]
    lo, hi = lo_ref[it], hi_ref[it]

    @pl.when(hi > lo)
    def _():
        h = _unpack_bf16_pairs(xs_ref[...]).astype(BF16)
        d_ff = w1_ref.shape[2]
        y = None
        for c0 in range(0, d_ff, EXPERT_SUBCHUNK):
            c1 = min(c0 + EXPERT_SUBCHUNK, d_ff)
            part = _swiglu_chunk(h, w1_ref[0, :, c0:c1], w3_ref[0, :, c0:c1], w2_ref[0, c0:c1, :])
            y = part if y is None else y + part
        packed = _pack_bf16_pairs(y)
        first_row = lo - tile_ref[it] * rows

        @pl.when(first_row == 0)
        def _():
            o_ref[...] = packed

        @pl.when(first_row > 0)
        def _():
            row = lax.broadcasted_iota(jnp.int32, o_ref.shape, 0)
            o_ref[...] = jnp.where(row >= first_row, packed, o_ref[...])


def _expert_ffn(xs, items, w1, w3, w2, rows):
    n_pairs, half = xs.shape
    item_tile, item_expert, item_lo, item_hi = items
    expert_weights = lambda w: pl.BlockSpec((1,) + w.shape[1:], lambda i, t, e, lo, hi: (e[i], 0, 0))
    grid_spec = pltpu.PrefetchScalarGridSpec(
        num_scalar_prefetch=4,
        grid=(item_tile.shape[0],),
        in_specs=[pl.BlockSpec((rows, half), lambda i, t, e, lo, hi: (t[i], 0)),
                  expert_weights(w1), expert_weights(w3), expert_weights(w2)],
        out_specs=pl.BlockSpec((rows, half), lambda i, t, e, lo, hi: (t[i], 0)),
    )
    return pl.pallas_call(
        _expert_ffn_kernel,
        grid_spec=grid_spec,
        out_shape=jax.ShapeDtypeStruct((n_pairs, half), jnp.uint32),
        compiler_params=_params("arbitrary"),
        name="expert_ffn",
    )(item_tile, item_expert, item_lo, item_hi, xs, w1, w3, w2)


def _work_items(counts, n_pairs, rows):
    n_exp = counts.shape[0]
    n_tiles = n_pairs // rows
    ends = jnp.cumsum(counts)
    starts = ends - counts
    expert_cuts = starts[1:]
    expert_at = jnp.arange(n_exp - 1, dtype=jnp.int32) + jnp.minimum(expert_cuts // rows + 1, n_tiles)
    k = jnp.arange(n_tiles + n_exp - 1, dtype=jnp.int32)[:, None]
    experts_before = jnp.sum(expert_at[None, :] < k, axis=1, dtype=jnp.int32)
    is_expert_cut = expert_at[None, :] == k
    lo = jnp.where(jnp.any(is_expert_cut, axis=1), jnp.sum(jnp.where(is_expert_cut, expert_cuts[None, :], 0), axis=1),
                   (k[:, 0] - experts_before) * rows).astype(jnp.int32)
    hi = jnp.concatenate([lo[1:], jnp.full((1,), n_pairs, jnp.int32)])
    tile = jnp.minimum(lo // rows, n_tiles - 1)
    expert = jnp.minimum(jnp.sum(ends[None, :] <= lo[:, None], axis=1, dtype=jnp.int32), n_exp - 1)
    return (tile, expert, lo, hi), starts


def _combine_kernel(x_ref, y1_ref, y2_ref, w_ref, fg_ref, o_ref, *, final_norm):
    w = w_ref[...]
    out = x_ref[...] + (w[:, 0:1] * _unpack_bf16_pairs(y1_ref[...]) + w[:, 1:2] * _unpack_bf16_pairs(y2_ref[...]))
    if final_norm:
        out = _rms(out, fg_ref[...], RMS_EPS)
    o_ref[...] = out


def _combine(x2, y_pairs, wts, final_g, tm, final_norm):
    n, d = x2.shape
    nt = n // tm
    return pl.pallas_call(
        functools.partial(_combine_kernel, final_norm=final_norm),
        grid=(nt,),
        in_specs=[pl.BlockSpec((tm, d), lambda i: (i, 0)),
                  pl.BlockSpec((tm, d // 2), lambda i: (i, 0)),
                  pl.BlockSpec((tm, d // 2), lambda i: (i + nt, 0)),
                  pl.BlockSpec((tm, wts.shape[1]), lambda i: (i, 0)),
                  pl.BlockSpec((1, d), lambda i: (0, 0))],
        out_specs=pl.BlockSpec((tm, d), lambda i: (i, 0)),
        out_shape=jax.ShapeDtypeStruct((n, d), F32),
        compiler_params=_params("parallel"),
        name="moe_combine",
    )(x2, y_pairs, y_pairs, wts, final_g)


def _moe(x2, g_ffn, router, w1, w3, w2, final_g, tm, final_norm):
    n, d = x2.shape
    n_exp = router.shape[1]
    router_pad = jnp.pad(router, ((0, 0), (0, 128 - n_exp)))
    idx, wts, h_packed, seen = _router(x2, g_ffn, router_pad, n_exp, tm)

    rows = _pick_tile(2 * n, 512)
    counts = seen[0, :n_exp].astype(jnp.int32)
    items, starts = _work_items(counts, 2 * n, rows)
    pos = jnp.concatenate([starts[idx[0]] + idx[2], starts[idx[1]] + idx[3]])

    xs = _sc_scatter_pairs(h_packed, pos)
    ys = _expert_ffn(xs, items, w1, w3, w2, rows)
    y_pairs = _sc_gather(ys, pos)
    return _combine(x2, y_pairs, wts, final_g, tm, final_norm)


def _block_diag(w):
    g, c, _ = w.shape
    eye = jnp.eye(g, dtype=w.dtype)
    return (eye[:, None, :, None] * w[:, :, None, :]).reshape(g * c, g * c)


def _pick_tile(n, target):
    t = min(n, target)
    while n % t:
        t //= 2
    return t


def kernel(x, bias_table, mix_norm_g, w_in, conv_w, pool_w, pool_scale, diff_lambda, diff_subln_g,
           branch_proj, w_out, ffn_norm_g, dense_w1, dense_w3, dense_w2, moe_router, moe_w1, moe_w3,
           moe_w2, final_norm_g):
    batch, seq, d = x.shape
    depth = w_in.shape[0]
    n_mix = N_MIX_SLICES * BRANCH_WIDTH
    n = batch * seq
    assert seq % ATT_BLOCK == 0 and d % 128 == 0
    tm = _pick_tile(seq, 512)

    x2 = x.reshape(n, d)
    tiles = _bias_tiles(bias_table)
    row = lambda v: v.reshape(1, -1)
    final_g = row(final_norm_g)

    for i in range(depth):
        last_layer = i == depth - 1
        g_mix = row(mix_norm_g[i])
        w_mix = w_in[i, :, :n_mix].astype(BF16)
        w_gate = w_in[i, :, n_mix:].astype(BF16)
        u = _inproj(x2, g_mix, w_mix, _pick_tile(n, 2 * tm))

        lam_init = 0.8 - 0.6 * math.exp(-0.3 * i)
        y_a = _moba(u, tiles, batch, seq)
        y_d = _diff(u, tiles, diff_lambda[i], row(jnp.tile(diff_subln_g[i], N_HEADS)), batch, seq, lam_init)
        x2 = _merge(x2, g_mix, w_gate, y_a, y_d, u, conv_w[i], _block_diag(pool_w[i]).astype(BF16),
                    row(pool_scale[i]), branch_proj[i].reshape(-1, d).astype(BF16), w_out[i].astype(BF16),
                    seq, _pick_tile(seq, 2 * tm))

        g_ffn = row(ffn_norm_g[i])
        j = i // 2
        if i % 2 == 0:
            x2 = _ffn(x2, g_ffn, dense_w1[j].astype(BF16), dense_w3[j].astype(BF16),
                      dense_w2[j].astype(BF16), final_g, _pick_tile(n, 2 * tm), min(512, dense_w1.shape[2]),
                      final_norm=last_layer)
        else:
            x2 = _moe(x2, g_ffn, moe_router[j], moe_w1[j].astype(BF16), moe_w3[j].astype(BF16),
                      moe_w2[j].astype(BF16), final_g, tm, final_norm=last_layer)

    return x2.reshape(batch, seq, d)
```

```python
import functools
import math

import numpy as np
import jax
import jax.numpy as jnp
from jax import lax
from jax.experimental import pallas as pl
from jax.experimental.pallas import tpu as pltpu
from jax.experimental.pallas import tpu_sc as plsc

F32 = jnp.float32
BF16 = jnp.bfloat16

BRANCH_WIDTH = 256
N_MIX_SLICES = 10
HEAD_WIDTH = 64
N_HEADS = 4
DIFF_QK_DIM = 32
ATT_BLOCK = 256
MOBA_TOPK = 3
CONV_WIDTH = 3
POOL_WINDOWS = (2, 4, 8, 16)
POOL_GROUP = 64
HALO = 16
REL_BUCKETS = 32
REL_MAX_DIST = 128
TOP_K_EXPERTS = 2
RMS_EPS = 1e-6
SUBLN_EPS = 1e-5
NEG_INF = -1e30
LOG2_E = math.log2(math.e)
V7X_VMEM_BYTES = 64 * 1024 * 1024
VMEM_LIMIT = V7X_VMEM_BYTES - 8 * 1024 * 1024

_TRANS_B = (((1,), (1,)), ((), ()))


def _params(*sem):
    return pltpu.CompilerParams(dimension_semantics=sem, vmem_limit_bytes=VMEM_LIMIT)


def _rms(x, g, eps):
    r = lax.rsqrt(jnp.mean(x * x, axis=-1, keepdims=True) + eps)
    return x * r * g


def _inproj_kernel(x_ref, g_ref, w_ref, o_ref):
    h = _rms(x_ref[...], g_ref[...], RMS_EPS).astype(BF16)
    o_ref[...] = jnp.dot(h, w_ref[...], preferred_element_type=F32)


def _inproj(x2, g, w, tm):
    n, d = x2.shape
    wn = w.shape[1]
    return pl.pallas_call(
        _inproj_kernel,
        grid=(n // tm,),
        in_specs=[pl.BlockSpec((tm, d), lambda i: (i, 0)),
                  pl.BlockSpec((1, d), lambda i: (0, 0)),
                  pl.BlockSpec((d, wn), lambda i: (0, 0))],
        out_specs=pl.BlockSpec((tm, wn), lambda i: (i, 0)),
        out_shape=jax.ShapeDtypeStruct((n, wn), F32),
        compiler_params=_params("parallel"),
        name="inproj",
    )(x2, g, w)


def _rel_bucket_np(dist):
    n = np.maximum(dist, 0)
    max_exact = REL_BUCKETS // 2
    nf = np.maximum(n, max_exact).astype(np.float32)
    large = max_exact + (np.log(nf / np.float32(max_exact)) / np.float32(math.log(REL_MAX_DIST / max_exact))
                         * np.float32(REL_BUCKETS - max_exact)).astype(np.int32)
    large = np.minimum(large, REL_BUCKETS - 1)
    return np.where(n < max_exact, n, large).astype(np.int32)


def _bucket_tiles():
    i = np.arange(ATT_BLOCK)[:, None]
    j = np.arange(ATT_BLOCK)[None, :]
    return np.stack([_rel_bucket_np(i - j), _rel_bucket_np(ATT_BLOCK + i - j)])


def _bias_tiles_kernel(tab_ref, bkt_ref, o_ref):
    h = pl.program_id(0)
    bkt = bkt_ref[...]
    acc = jnp.zeros(bkt.shape, F32)
    for b in range(REL_BUCKETS):
        acc = jnp.where(bkt == b, tab_ref[b, h], acc)
    o_ref[0] = (acc - tab_ref[REL_BUCKETS - 1, h]) * LOG2_E


def _bias_tiles(bias_table):
    n_heads = bias_table.shape[1]
    bkt = jnp.asarray(_bucket_tiles())
    return pl.pallas_call(
        _bias_tiles_kernel,
        grid=(n_heads,),
        in_specs=[pl.BlockSpec(memory_space=pltpu.SMEM),
                  pl.BlockSpec((2, ATT_BLOCK, ATT_BLOCK), lambda h: (0, 0, 0))],
        out_specs=pl.BlockSpec((1, 2, ATT_BLOCK, ATT_BLOCK), lambda h: (h, 0, 0, 0)),
        out_shape=jax.ShapeDtypeStruct((n_heads, 2, ATT_BLOCK, ATT_BLOCK), F32),
        compiler_params=_params("arbitrary"),
        name="bias_tiles",
    )(bias_table, bkt)


LANE_TILE = 128
SUBLANES = 8


def _head_lanes(lane, h):
    return (lane >= h * HEAD_WIDTH) & (lane < (h + 1) * HEAD_WIDTH)


def _head_to_low_lanes(x, h):
    assert 2 * HEAD_WIDTH == LANE_TILE
    tile = x[:, (h // 2) * LANE_TILE:(h // 2 + 1) * LANE_TILE]
    if h % 2:
        tile = pltpu.roll(tile, HEAD_WIDTH, axis=1)
    return jnp.where(lax.broadcasted_iota(jnp.int32, tile.shape, 1) < HEAD_WIDTH, tile, 0.0)


def _fold(op, s):
    out = s[:, :LANE_TILE]
    for c in range(LANE_TILE, s.shape[1], LANE_TILE):
        out = op(out, s[:, c:c + LANE_TILE])
    return out


def _stage_values(v_ref, vm_ref):
    t = ATT_BLOCK
    lane = lax.broadcasted_iota(jnp.int32, (t, BRANCH_WIDTH), 1)
    for j in range(vm_ref.shape[0]):
        vj = v_ref[j * t:(j + 1) * t, :]
        for h in range(N_HEADS):
            vm_ref[j, h * t:(h + 1) * t, :] = jnp.where(_head_lanes(lane, h), vj, 0.0).astype(BF16)


def _softmax_attend(qb, q16, head_of, key_slot, groups, kb_ref, vm_ref, tiles_ref, s_ref, m_ref, l_ref, acc_ref):
    t = ATT_BLOCK
    chains = range(len(q16))
    pair_previous = len(q16) <= N_HEADS
    causal =(lax.broadcasted_iota(jnp.int32, (t, t), 0) >= lax.broadcasted_iota(jnp.int32, (t, t), 1))

    slots = sorted({key_slot(i) for i in chains})

    def keys(j):
        return {slot: kb_ref[slot, pl.ds(pl.multiple_of(j * t, t), t), :] for slot in slots}

    def score(i, kj):
        return lax.dot_general(q16[i], kj[key_slot(i)], _TRANS_B, preferred_element_type=F32)

    def in_pairs(count, fn):
        def pair(p, carry):
            fn([2 * p, 2 * p + 1])
            return carry

        lax.fori_loop(0, count >> 1, pair, 0)

        @pl.when((count & 1) == 1)
        def _():
            fn([count - 1])

    def near(own, prev):
        k_own = keys(qb) if own else None
        k_prev = keys(qb - 1) if prev else None
        for i in chains:
            row_max = None
            if own:
                s = jnp.where(causal, score(i, k_own) + tiles_ref[head_of(i), 0], NEG_INF)
                s_ref[i, qb] = s
                row_max = _fold(jnp.maximum, s)
            if prev:
                s = score(i, k_prev) + tiles_ref[head_of(i), 1]
                s_ref[i, qb - 1] = s
                prev_max = _fold(jnp.maximum, s)
                row_max = jnp.maximum(m_ref[i] if row_max is None else row_max, prev_max)
            m_ref[i] = row_max

    if pair_previous:
        pl.when(qb == 0)(functools.partial(near, True, False))
        pl.when(qb >= 1)(functools.partial(near, True, True))
    else:
        near(True, False)
        pl.when(qb >= 1)(functools.partial(near, False, True))

    def far(js):
        ks = [keys(j) for j in js]
        for i in chains:
            ss = [score(i, kj) for kj in ks]
            for j, s in zip(js, ss):
                s_ref[i, j] = s
            m_ref[i] = functools.reduce(jnp.maximum, [m_ref[i]] + [_fold(jnp.maximum, s) for s in ss])

    in_pairs(jnp.maximum(qb - 1, 0), far)

    for i in chains:
        m_ref[i] = jnp.broadcast_to(jnp.max(m_ref[i], axis=1, keepdims=True), (t, LANE_TILE))

    def accumulate(js, first=False):
        for g, members in enumerate(groups):
            parts = [[] for _ in js]
            for i in members:
                sums = []
                for a, j in enumerate(js):
                    s = s_ref[i, j]
                    p = [jnp.exp2(s[:, c:c + LANE_TILE] - m_ref[i]) for c in range(0, t, LANE_TILE)]
                    sums += p
                    parts[a] += [x.astype(BF16) for x in p]
                row_sum = functools.reduce(jnp.add, sums)
                l_ref[i] = row_sum if first else l_ref[i] + row_sum
            lhs = jnp.concatenate([x for tile_parts in parts for x in tile_parts], axis=1)
            rhs = jnp.concatenate([vm_ref[j] for j in js], axis=0) if len(js) > 1 else vm_ref[js[0]]
            pv = jnp.dot(lhs, rhs, preferred_element_type=F32)
            acc_ref[g] = pv if first else acc_ref[g] + pv

    if pair_previous:
        pl.when(qb == 0)(functools.partial(accumulate, [qb], first=True))
        pl.when(qb >= 1)(functools.partial(accumulate, [qb - 1, qb], first=True))
        in_pairs(jnp.maximum(qb - 1, 0), accumulate)
    else:
        accumulate([qb], first=True)
        in_pairs(qb, accumulate)


def _per_head_lanes(lane, cols):
    out = jnp.broadcast_to(cols[0], lane.shape)
    for h in range(1, N_HEADS):
        out = jnp.where(_head_lanes(lane, h), cols[h], out)
    return out


def _attention_scratch(seq, n_chains, n_groups, key_slots, key_width):
    t = ATT_BLOCK
    n_blk = seq // t
    return [pltpu.VMEM((key_slots, seq, key_width), BF16),
            pltpu.VMEM((n_blk, N_HEADS * t, BRANCH_WIDTH), BF16),
            pltpu.VMEM((n_chains, n_blk, t, t), F32),
            pltpu.VMEM((n_chains, t, LANE_TILE), F32),
            pltpu.VMEM((n_chains, t, LANE_TILE), F32),
            pltpu.VMEM((n_groups, t, BRANCH_WIDTH), F32)]


def _moba_kernel(q_ref, k_ref, v_ref, tiles_ref, o_ref, kb_ref, vm_ref, s_ref, m_ref, l_ref, acc_ref, kmean_ref):
    t = ATT_BLOCK
    n_blk = k_ref.shape[0] // t
    qb = pl.program_id(1)

    heads = range(N_HEADS)
    blk_rows = kmean_ref.shape[0] // N_HEADS
    assert n_blk <= blk_rows <= LANE_TILE

    assert HEAD_WIDTH + n_blk <= LANE_TILE
    low = lax.broadcasted_iota(jnp.int32, (t, LANE_TILE), 1)

    @pl.when(qb == 0)
    def _():
        _stage_values(v_ref, vm_ref)
        ch = lax.broadcasted_iota(jnp.int32, (1, BRANCH_WIDTH), 1)
        kmean_ref[...] = jnp.zeros(kmean_ref.shape, F32)
        for j in range(n_blk):
            kj = k_ref[j * t:(j + 1) * t, :]
            mean_j = jnp.mean(kj, axis=0, keepdims=True)
            for h in heads:
                kb_ref[h, j * t:(j + 1) * t, :] = jnp.where(low == HEAD_WIDTH + j, 1.0,
                                                            _head_to_low_lanes(kj, h)).astype(BF16)
                kmean_ref[h * blk_rows + j:h * blk_rows + j + 1, :] = jnp.where(_head_lanes(ch, h), mean_j, 0.0)

    lane = lax.broadcasted_iota(jnp.int32, (t, BRANCH_WIDTH), 1)
    q = q_ref[...]
    gate = lax.dot_general(kmean_ref[...], q * (HEAD_WIDTH ** -0.5), _TRANS_B, precision=lax.Precision.HIGHEST,
                           preferred_element_type=F32)
    blk = lax.broadcasted_iota(jnp.int32, (blk_rows, t), 0).astype(F32)
    drops = []
    for h in heads:
        g = gate[h * blk_rows:(h + 1) * blk_rows, :]
        avail = blk < qb.astype(F32)
        for _ in range(MOBA_TOPK):
            best = jnp.max(jnp.where(avail, g, -jnp.inf), axis=0, keepdims=True)
            first = jnp.min(jnp.where(avail & (g == best), blk, float(blk_rows)), axis=0, keepdims=True)
            avail = avail & (blk != first)
        drops += [jnp.zeros((HEAD_WIDTH, t), F32), jnp.where(avail, NEG_INF, 0.0),
                  jnp.zeros((LANE_TILE - HEAD_WIDTH - blk_rows, t), F32)]
    eye = (lax.broadcasted_iota(jnp.int32, (t, t), 0) == lax.broadcasted_iota(jnp.int32, (t, t), 1)).astype(BF16)
    drop = lax.dot_general(eye, jnp.concatenate(drops, axis=0).astype(BF16), _TRANS_B,
                           preferred_element_type=F32).astype(BF16)
    qf = q * (HEAD_WIDTH ** -0.5 * LOG2_E)
    q_aug = [jnp.where(low < HEAD_WIDTH, _head_to_low_lanes(qf, h).astype(BF16),
                       drop[:, h * LANE_TILE:(h + 1) * LANE_TILE]) for h in heads]

    _softmax_attend(qb, q_aug, lambda i: i, lambda i: i, [list(heads)], kb_ref, vm_ref, tiles_ref,
                    s_ref, m_ref, l_ref, acc_ref)
    row_sums = _per_head_lanes(lane, [jnp.sum(l_ref[h], axis=1, keepdims=True) for h in heads])
    o_ref[...] = (acc_ref[0] / row_sums).astype(o_ref.dtype)


def _moba(u, tiles, batch, seq):
    t = ATT_BLOCK
    nq = seq // t
    return pl.pallas_call(
        _moba_kernel,
        grid=(batch, nq),
        in_specs=[pl.BlockSpec((t, BRANCH_WIDTH), lambda b, q: (b * nq + q, 0)),
                  pl.BlockSpec((seq, BRANCH_WIDTH), lambda b, q: (b, 1)),
                  pl.BlockSpec((seq, BRANCH_WIDTH), lambda b, q: (b, 2)),
                  pl.BlockSpec((N_HEADS, 2, t, t), lambda b, q: (0, 0, 0, 0))],
        out_specs=pl.BlockSpec((t, BRANCH_WIDTH), lambda b, q: (b * nq + q, 0)),
        out_shape=jax.ShapeDtypeStruct((batch * seq, BRANCH_WIDTH), BF16),
        scratch_shapes=(_attention_scratch(seq, N_HEADS, 1, N_HEADS, LANE_TILE)
                        + [pltpu.VMEM((N_HEADS * SUBLANES * pl.cdiv(nq, SUBLANES), BRANCH_WIDTH), F32)]),
        compiler_params=_params("arbitrary", "arbitrary"),
        name="moba",
    )(u, u, u, tiles)


def _diff_kernel(lam_ref, g_ref, q_ref, k_ref, v_ref, tiles_ref, o_ref, kb_ref, vm_ref, s_ref, m_ref, l_ref,
                 acc_ref, *, lam_init):
    t = ATT_BLOCK
    qb = pl.program_id(1)

    @pl.when(qb == 0)
    def _():
        _stage_values(v_ref, vm_ref)
        kb_ref[0] = k_ref[...].astype(BF16)

    lp = lam_ref[...]
    lam = (jnp.exp(jnp.sum(lp[0:1] * lp[1:2], axis=1, keepdims=True))
           - jnp.exp(jnp.sum(lp[2:3] * lp[3:4], axis=1, keepdims=True)) + lam_init)

    lane = lax.broadcasted_iota(jnp.int32, (t, BRANCH_WIDTH), 1)
    qf = q_ref[...] * (DIFF_QK_DIM ** -0.5 * LOG2_E)
    q16 = []
    for h in range(N_HEADS):
        for c in range(2):
            lo = h * HEAD_WIDTH + c * DIFF_QK_DIM
            q16.append(jnp.where((lane >= lo) & (lane < lo + DIFF_QK_DIM), qf, 0.0).astype(BF16))
    groups = [[2 * h + c for h in range(N_HEADS)] for c in range(2)]
    _softmax_attend(qb, q16, lambda i: i // 2, lambda i: 0, groups, kb_ref, vm_ref, tiles_ref,
                    s_ref, m_ref, l_ref, acc_ref)

    row_sums = [_per_head_lanes(lane, [jnp.sum(l_ref[i], axis=1, keepdims=True) for i in members])
                for members in groups]
    o = acc_ref[0] / row_sums[0] - lam * (acc_ref[1] / row_sums[1])
    sq = o * o
    mean_sq = _per_head_lanes(lane, [jnp.sum(jnp.where(_head_lanes(lane, h), sq, 0.0), axis=1, keepdims=True)
                                     for h in range(N_HEADS)]) * (1.0 / HEAD_WIDTH)
    o_ref[...] = (o * lax.rsqrt(mean_sq + SUBLN_EPS) * g_ref[...] * (1.0 - lam_init)).astype(o_ref.dtype)


def _diff(u, tiles, lam_params, subln_g4, batch, seq, lam_init):
    t = ATT_BLOCK
    nq = seq // t
    return pl.pallas_call(
        functools.partial(_diff_kernel, lam_init=lam_init),
        grid=(batch, nq),
        in_specs=[pl.BlockSpec((4, DIFF_QK_DIM), lambda b, q: (0, 0)),
                  pl.BlockSpec((1, BRANCH_WIDTH), lambda b, q: (0, 0)),
                  pl.BlockSpec((t, BRANCH_WIDTH), lambda b, q: (b * nq + q, 7)),
                  pl.BlockSpec((seq, BRANCH_WIDTH), lambda b, q: (b, 8)),
                  pl.BlockSpec((seq, BRANCH_WIDTH), lambda b, q: (b, 9)),
                  pl.BlockSpec((N_HEADS, 2, t, t), lambda b, q: (1, 0, 0, 0))],
        out_specs=pl.BlockSpec((t, BRANCH_WIDTH), lambda b, q: (b * nq + q, 0)),
        out_shape=jax.ShapeDtypeStruct((batch * seq, BRANCH_WIDTH), BF16),
        scratch_shapes=_attention_scratch(seq, 2 * N_HEADS, 2, 1, BRANCH_WIDTH),
        compiler_params=_params("arbitrary", "arbitrary"),
        name="diff_attn",
    )(lam_params, subln_g4, u, u, u, tiles)


def _conv_pool_branches(xb_ref, bb_ref, cb_ref, pc_ref, xh_ref, ch_ref, ph_ref, cw_ref, pw_ref, ps_ref,
                        u_ref, s_ref, seq):
    tm = xb_ref.shape[0]
    pos0 = (pl.program_id(0) * tm) % seq
    has_history = pos0 > 0
    pos = pos0 + lax.broadcasted_iota(jnp.int32, (tm, 1), 0)
    lane = lax.broadcasted_iota(jnp.int32, (tm, BRANCH_WIDTH), 1)

    u_ref[0:HALO, :] = jnp.where(has_history, ch_ref[...] * xh_ref[...], 0.0)
    u_ref[HALO:, :] = cb_ref[...] * xb_ref[...]
    conv = cw_ref[CONV_WIDTH - 1:CONV_WIDTH, :] * u_ref[HALO:, :]
    for i in range(CONV_WIDTH - 1):
        shift = CONV_WIDTH - 1 - i
        conv = conv + cw_ref[i:i + 1, :] * u_ref[HALO - shift:HALO - shift + tm, :]
    y_b = (bb_ref[...] * conv).astype(BF16)

    s_ref[0:HALO, :] = jnp.where(has_history, ph_ref[...], 0.0)
    s_ref[HALO:, :] = pc_ref[...]
    pooled = jnp.zeros((tm, BRANCH_WIDTH), F32)
    done = 0
    for g, w in enumerate(POOL_WINDOWS):
        half = w // 2
        cur = s_ref[done + half:, :] + s_ref[done:HALO + tm - half, :]
        done += half
        s_ref[done:, :] = cur
        cnt = jnp.minimum(pos + 1, w).astype(F32)
        mean_w = s_ref[HALO:, :] / cnt
        pooled = jnp.where((lane >= g * POOL_GROUP) & (lane < (g + 1) * POOL_GROUP), mean_w, pooled)
    pooled = pooled - pc_ref[...]
    y_c = jnp.dot(pooled.astype(BF16), pw_ref[...], preferred_element_type=F32) * ps_ref[...]
    return y_b, y_c.astype(BF16)


def _merge_kernel(x_ref, g_ref, wg_ref, ya_ref, yd_ref, xb_ref, bb_ref, cb_ref, pc_ref, xh_ref, ch_ref, ph_ref,
                  cw_ref, pw_ref, ps_ref, bp_ref, wo_ref, o_ref, u_ref, s_ref, *, seq):
    d = x_ref.shape[1]
    x = x_ref[...]
    h = _rms(x, g_ref[...], RMS_EPS).astype(BF16)
    def gated(b, y):
        gate = jax.nn.sigmoid(jnp.dot(h, wg_ref[:, b * d:(b + 1) * d], preferred_element_type=F32))
        return gate * jnp.dot(y, bp_ref[b * BRANCH_WIDTH:(b + 1) * BRANCH_WIDTH, :], preferred_element_type=F32)

    merged = gated(0, ya_ref[...]) + gated(3, yd_ref[...])
    y_b, y_c = _conv_pool_branches(xb_ref, bb_ref, cb_ref, pc_ref, xh_ref, ch_ref, ph_ref, cw_ref, pw_ref, ps_ref,
                                   u_ref, s_ref, seq)
    merged = merged + gated(1, y_b) + gated(2, y_c)
    o_ref[...] = x + jnp.dot(merged.astype(BF16), wo_ref[...], preferred_element_type=F32)


def _merge(x2, g, w_gate, y_a, y_d, u, conv_w, pool_w_bd, pool_scale, bp, w_out, seq, tm):
    n, d = x2.shape
    full = lambda a: pl.BlockSpec(a.shape, lambda i: (0,) * a.ndim)
    resident = lambda a: pl.BlockSpec(a.shape, lambda i: (0,) * a.ndim, pipeline_mode=pl.Buffered(1))
    rows = lambda a: pl.BlockSpec((tm, a.shape[1]), lambda i: (i, 0))
    u_rows = lambda c: pl.BlockSpec((tm, BRANCH_WIDTH), lambda i, c=c: (i, c))
    u_halo = lambda c: pl.BlockSpec((HALO, BRANCH_WIDTH),
                                    lambda i, c=c: (jnp.maximum(i * (tm // HALO) - 1, 0), c))
    return pl.pallas_call(
        functools.partial(_merge_kernel, seq=seq),
        grid=(n // tm,),
        in_specs=[rows(x2), full(g), resident(w_gate), rows(y_a), rows(y_d),
                  u_rows(3), u_rows(4), u_rows(5), u_rows(6), u_halo(3), u_halo(5), u_halo(6),
                  full(conv_w), full(pool_w_bd), full(pool_scale), resident(bp), resident(w_out)],
        out_specs=pl.BlockSpec((tm, d), lambda i: (i, 0)),
        out_shape=jax.ShapeDtypeStruct((n, d), F32),
        scratch_shapes=[pltpu.VMEM((HALO + tm, BRANCH_WIDTH), F32),
                        pltpu.VMEM((HALO + tm, BRANCH_WIDTH), F32)],
        compiler_params=_params("parallel"),
        name="merge",
    )(x2, g, w_gate, y_a, y_d, u, u, u, u, u, u, u, conv_w, pool_w_bd, pool_scale, bp, w_out)


ROUTE_COLS = 8


def _pack_bf16_pairs(h):
    c = h.shape[1] // 2
    bits = lax.bitcast_convert_type(h.astype(BF16).astype(F32), jnp.uint32)
    return (bits[:, :c] >> 16) | (bits[:, c:] & jnp.uint32(0xFFFF0000))


def _unpack_bf16_pairs(w):
    lo = lax.bitcast_convert_type(w << 16, F32)
    hi = lax.bitcast_convert_type(w & jnp.uint32(0xFFFF0000), F32)
    return jnp.concatenate([lo, hi], axis=1)


def _router_kernel(x_ref, g_ref, r_ref, tri_ref, idx_ref, wts_ref, hpk_ref, cnt_ref, seen_ref, *, n_exp):
    @pl.when(pl.program_id(0) == 0)
    def _():
        seen_ref[...] = jnp.zeros(seen_ref.shape, F32)

    h = _rms(x_ref[...], g_ref[...], RMS_EPS)
    h_hi = h.astype(BF16)
    h_lo = (h - h_hi.astype(F32)).astype(BF16)
    r = r_ref[...]
    r_hi = r.astype(BF16)
    r_lo = (r - r_hi.astype(F32)).astype(BF16)
    logits = (jnp.dot(h_hi, r_hi, preferred_element_type=F32) + jnp.dot(h_hi, r_lo, preferred_element_type=F32)
              + jnp.dot(h_lo, r_hi, preferred_element_type=F32))
    lane = lax.broadcasted_iota(jnp.int32, logits.shape, 1).astype(F32)
    logits = jnp.where(lane < n_exp, logits, -jnp.inf)
    big = float(logits.shape[1])
    m1 = jnp.max(logits, axis=1, keepdims=True)
    i1 = jnp.min(jnp.where(logits == m1, lane, big), axis=1, keepdims=True)
    rest = jnp.where(lane == i1, -jnp.inf, logits)
    m2 = jnp.max(rest, axis=1, keepdims=True)
    i2 = jnp.min(jnp.where(rest == m2, lane, big), axis=1, keepdims=True)
    e2 = jnp.exp(m2 - m1)
    w1 = 1.0 / (1.0 + e2)
    w2 = e2 / (1.0 + e2)

    chosen = (lane == i1) | (lane == i2)
    before = seen_ref[...] + jnp.dot(tri_ref[...], chosen.astype(BF16), preferred_element_type=F32)
    r1 = jnp.sum(jnp.where(lane == i1, before, 0.0), axis=1, keepdims=True)
    r2 = jnp.sum(jnp.where(lane == i2, before, 0.0), axis=1, keepdims=True)
    seen_ref[...] += jnp.sum(chosen.astype(F32), axis=0, keepdims=True)
    cnt_ref[...] = seen_ref[...]

    pick = lambda col, a, b, c, d: jnp.where(col == 0, a, jnp.where(col == 1, b, jnp.where(col == 2, c, d)))
    record = pick(lane, i1, i2, r1, jnp.where(lane == 3, r2, 0.0))
    idx_ref[...] = record.T[0:idx_ref.shape[0], :].astype(jnp.int32)
    wts_ref[...] = pick(lax.broadcasted_iota(jnp.int32, wts_ref.shape, 1), w1, w2, 0.0, 0.0)
    hpk_ref[...] = _pack_bf16_pairs(h)


def _router(x2, g, router_pad, n_exp, tm):
    n, d = x2.shape
    tri = jnp.asarray(np.tril(np.ones((tm, tm), np.float32), -1), BF16)
    full = lambda a: pl.BlockSpec(a.shape, lambda i: (0,) * a.ndim)
    rows = lambda w: pl.BlockSpec((tm, w), lambda i: (i, 0))
    return pl.pallas_call(
        functools.partial(_router_kernel, n_exp=n_exp),
        grid=(n // tm,),
        in_specs=[rows(d), full(g), full(router_pad), full(tri)],
        out_specs=[pl.BlockSpec((ROUTE_COLS, tm), lambda i: (0, i)), rows(ROUTE_COLS), rows(d // 2),
                   pl.BlockSpec((1, router_pad.shape[1]), lambda i: (0, 0))],
        out_shape=[jax.ShapeDtypeStruct((ROUTE_COLS, n), jnp.int32),
                   jax.ShapeDtypeStruct((n, ROUTE_COLS), F32),
                   jax.ShapeDtypeStruct((n, d // 2), jnp.uint32),
                   jax.ShapeDtypeStruct((1, router_pad.shape[1]), F32)],
        scratch_shapes=[pltpu.VMEM((1, router_pad.shape[1]), F32)],
        compiler_params=_params("arbitrary"),
        name="router",
    )(x2, g, router_pad, tri)


V7X_SC_CORES = 2
V7X_SC_SUBCORES = 16
SC_GATHER_BYTES = 256 * 1024
SC_MAX_INDEX_VECTOR = 128


def _sc_gather(table, idx):
    n_rows, width = idx.shape[0], table.shape[1]
    workers = V7X_SC_CORES * V7X_SC_SUBCORES
    chunk = min(SC_MAX_INDEX_VECTOR, SC_GATHER_BYTES // (width * table.dtype.itemsize))
    assert table.dtype.itemsize == 4 and n_rows % (workers * chunk) == 0 and chunk % 8 == 0
    per_worker = n_rows // workers
    mesh = plsc.VectorSubcoreMesh(core_axis_name="c", subcore_axis_name="s",
                                  num_cores=V7X_SC_CORES, num_subcores=V7X_SC_SUBCORES)

    def body(table_hbm, idx_hbm, out_hbm, idx_v, rows_v, sem):
        base = (lax.axis_index("s") * V7X_SC_CORES + lax.axis_index("c")) * per_worker

        @pl.loop(0, per_worker // chunk)
        def _(i):
            off = base + i * chunk
            pltpu.sync_copy(idx_hbm.at[pl.ds(off, chunk)], idx_v)
            pltpu.async_copy(table_hbm.at[idx_v], rows_v, sem).wait()
            pltpu.sync_copy(rows_v, out_hbm.at[pl.ds(off, chunk)])

    return pl.kernel(
        body,
        out_type=jax.ShapeDtypeStruct((n_rows, width), table.dtype),
        mesh=mesh,
        scratch_types=[pltpu.VMEM((chunk,), jnp.int32), pltpu.VMEM((chunk, width), table.dtype),
                       pltpu.SemaphoreType.DMA],
        name="sc_gather",
    )(table, idx)


def _sc_scatter_pairs(table, pos):
    n, width = table.shape
    workers = V7X_SC_CORES * V7X_SC_SUBCORES
    chunk = min(SC_MAX_INDEX_VECTOR, SC_GATHER_BYTES // (width * table.dtype.itemsize))
    assert table.dtype.itemsize == 4 and n % (workers * chunk) == 0 and chunk % 8 == 0
    per_worker = n // workers
    mesh = plsc.VectorSubcoreMesh(core_axis_name="c", subcore_axis_name="s",
                                  num_cores=V7X_SC_CORES, num_subcores=V7X_SC_SUBCORES)

    def body(table_hbm, pos_hbm, out_hbm, idx_a, idx_b, rows_v, sem):
        base = (lax.axis_index("s") * V7X_SC_CORES + lax.axis_index("c")) * per_worker

        @pl.loop(0, per_worker // chunk)
        def _(i):
            off = base + i * chunk
            pltpu.sync_copy(table_hbm.at[pl.ds(off, chunk)], rows_v)
            pltpu.sync_copy(pos_hbm.at[pl.ds(off, chunk)], idx_a)
            pltpu.sync_copy(pos_hbm.at[pl.ds(n + off, chunk)], idx_b)
            pltpu.async_copy(rows_v, out_hbm.at[idx_a], sem).wait()
            pltpu.async_copy(rows_v, out_hbm.at[idx_b], sem).wait()

    return pl.kernel(
        body,
        out_type=jax.ShapeDtypeStruct((2 * n, width), table.dtype),
        mesh=mesh,
        scratch_types=[pltpu.VMEM((chunk,), jnp.int32), pltpu.VMEM((chunk,), jnp.int32),
                       pltpu.VMEM((chunk, width), table.dtype), pltpu.SemaphoreType.DMA],
        name="sc_scatter_pairs",
    )(table, pos)


def _swiglu_chunk(h, w1, w3, w2):
    a = jnp.dot(h, w1, preferred_element_type=F32)
    b = jnp.dot(h, w3, preferred_element_type=F32)
    t = (a * jax.nn.sigmoid(a) * b).astype(BF16)
    return jnp.dot(t, w2, preferred_element_type=F32)


def _ffn_kernel(x_ref, g_ref, w1_ref, w3_ref, w2_ref, fg_ref, o_ref, *, fc, final_norm):
    d_ff = w1_ref.shape[1]
    x = x_ref[...]
    h = _rms(x, g_ref[...], RMS_EPS).astype(BF16)
    acc = None
    for c0 in range(0, d_ff, fc):
        c1 = min(c0 + fc, d_ff)
        y = _swiglu_chunk(h, w1_ref[:, c0:c1], w3_ref[:, c0:c1], w2_ref[c0:c1, :])
        acc = y if acc is None else acc + y
    out = x + acc
    if final_norm:
        out = _rms(out, fg_ref[...], RMS_EPS)
    o_ref[...] = out


def _ffn(x2, g, w1, w3, w2, final_g, tm, fc, final_norm):
    n, d = x2.shape
    resident = lambda a: pl.BlockSpec(a.shape, lambda i: (0,) * a.ndim, pipeline_mode=pl.Buffered(1))
    return pl.pallas_call(
        functools.partial(_ffn_kernel, fc=fc, final_norm=final_norm),
        grid=(n // tm,),
        in_specs=[pl.BlockSpec((tm, d), lambda i: (i, 0)),
                  pl.BlockSpec((1, d), lambda i: (0, 0)),
                  resident(w1), resident(w3), resident(w2),
                  pl.BlockSpec((1, d), lambda i: (0, 0))],
        out_specs=pl.BlockSpec((tm, d), lambda i: (i, 0)),
        out_shape=jax.ShapeDtypeStruct((n, d), F32),
        compiler_params=_params("parallel"),
        name="dense_ffn",
    )(x2, g, w1, w3, w2, final_g)


EXPERT_SUBCHUNK = 512


def _expert_ffn_kernel(tile_ref, exp_ref, lo_ref, hi_ref, xs_ref, w1_ref, w3_ref, w2_ref, o_ref):
    it = pl.program_id(0)
    rows = o_ref.shape[0]
    lo, hi = lo_ref[it], hi_ref[it]

    @pl.when(hi > lo)
    def _():
        h = _unpack_bf16_pairs(xs_ref[...]).astype(BF16)
        d_ff = w1_ref.shape[2]
        y = None
        for c0 in range(0, d_ff, EXPERT_SUBCHUNK):
            c1 = min(c0 + EXPERT_SUBCHUNK, d_ff)
            part = _swiglu_chunk(h, w1_ref[0, :, c0:c1], w3_ref[0, :, c0:c1], w2_ref[0, c0:c1, :])
            y = part if y is None else y + part
        packed = _pack_bf16_pairs(y)
        first_row = lo - tile_ref[it] * rows

        @pl.when(first_row == 0)
        def _():
            o_ref[...] = packed

        @pl.when(first_row > 0)
        def _():
            row = lax.broadcasted_iota(jnp.int32, o_ref.shape, 0)
            o_ref[...] = jnp.where(row >= first_row, packed, o_ref[...])


def _expert_ffn(xs, items, w1, w3, w2, rows):
    n_pairs, half = xs.shape
    item_tile, item_expert, item_lo, item_hi = items
    expert_weights = lambda w: pl.BlockSpec((1,) + w.shape[1:], lambda i, t, e, lo, hi: (e[i], 0, 0))
    grid_spec = pltpu.PrefetchScalarGridSpec(
        num_scalar_prefetch=4,
        grid=(item_tile.shape[0],),
        in_specs=[pl.BlockSpec((rows, half), lambda i, t, e, lo, hi: (t[i], 0)),
                  expert_weights(w1), expert_weights(w3), expert_weights(w2)],
        out_specs=pl.BlockSpec((rows, half), lambda i, t, e, lo, hi: (t[i], 0)),
    )
    return pl.pallas_call(
        _expert_ffn_kernel,
        grid_spec=grid_spec,
        out_shape=jax.ShapeDtypeStruct((n_pairs, half), jnp.uint32),
        compiler_params=_params("arbitrary"),
        name="expert_ffn",
    )(item_tile, item_expert, item_lo, item_hi, xs, w1, w3, w2)


def _work_items(counts, n_pairs, rows):
    n_exp = counts.shape[0]
    n_tiles = n_pairs // rows
    ends = jnp.cumsum(counts)
    starts = ends - counts
    expert_cuts = starts[1:]
    expert_at = jnp.arange(n_exp - 1, dtype=jnp.int32) + jnp.minimum(expert_cuts // rows + 1, n_tiles)
    k = jnp.arange(n_tiles + n_exp - 1, dtype=jnp.int32)[:, None]
    experts_before = jnp.sum(expert_at[None, :] < k, axis=1, dtype=jnp.int32)
    is_expert_cut = expert_at[None, :] == k
    lo = jnp.where(jnp.any(is_expert_cut, axis=1), jnp.sum(jnp.where(is_expert_cut, expert_cuts[None, :], 0), axis=1),
                   (k[:, 0] - experts_before) * rows).astype(jnp.int32)
    hi = jnp.concatenate([lo[1:], jnp.full((1,), n_pairs, jnp.int32)])
    tile = jnp.minimum(lo // rows, n_tiles - 1)
    expert = jnp.minimum(jnp.sum(ends[None, :] <= lo[:, None], axis=1, dtype=jnp.int32), n_exp - 1)
    return (tile, expert, lo, hi), starts


def _combine_kernel(x_ref, y1_ref, y2_ref, w_ref, fg_ref, o_ref, *, final_norm):
    w = w_ref[...]
    out = x_ref[...] + (w[:, 0:1] * _unpack_bf16_pairs(y1_ref[...]) + w[:, 1:2] * _unpack_bf16_pairs(y2_ref[...]))
    if final_norm:
        out = _rms(out, fg_ref[...], RMS_EPS)
    o_ref[...] = out


def _combine(x2, y_pairs, wts, final_g, tm, final_norm):
    n, d = x2.shape
    nt = n // tm
    return pl.pallas_call(
        functools.partial(_combine_kernel, final_norm=final_norm),
        grid=(nt,),
        in_specs=[pl.BlockSpec((tm, d), lambda i: (i, 0)),
                  pl.BlockSpec((tm, d // 2), lambda i: (i, 0)),
                  pl.BlockSpec((tm, d // 2), lambda i: (i + nt, 0)),
                  pl.BlockSpec((tm, wts.shape[1]), lambda i: (i, 0)),
                  pl.BlockSpec((1, d), lambda i: (0, 0))],
        out_specs=pl.BlockSpec((tm, d), lambda i: (i, 0)),
        out_shape=jax.ShapeDtypeStruct((n, d), F32),
        compiler_params=_params("parallel"),
        name="moe_combine",
    )(x2, y_pairs, y_pairs, wts, final_g)


def _moe(x2, g_ffn, router, w1, w3, w2, final_g, tm, final_norm):
    n, d = x2.shape
    n_exp = router.shape[1]
    router_pad = jnp.pad(router, ((0, 0), (0, 128 - n_exp)))
    idx, wts, h_packed, seen = _router(x2, g_ffn, router_pad, n_exp, tm)

    rows = _pick_tile(2 * n, 512)
    counts = seen[0, :n_exp].astype(jnp.int32)
    items, starts = _work_items(counts, 2 * n, rows)
    pos = jnp.concatenate([starts[idx[0]] + idx[2], starts[idx[1]] + idx[3]])

    xs = _sc_scatter_pairs(h_packed, pos)
    ys = _expert_ffn(xs, items, w1, w3, w2, rows)
    y_pairs = _sc_gather(ys, pos)
    return _combine(x2, y_pairs, wts, final_g, tm, final_norm)


def _block_diag(w):
    g, c, _ = w.shape
    eye = jnp.eye(g, dtype=w.dtype)
    return (eye[:, None, :, None] * w[:, :, None, :]).reshape(g * c, g * c)


def _pick_tile(n, target):
    t = min(n, target)
    while n % t:
        t //= 2
    return t


def kernel(x, bias_table, mix_norm_g, w_in, conv_w, pool_w, pool_scale, diff_lambda, diff_subln_g,
           branch_proj, w_out, ffn_norm_g, dense_w1, dense_w3, dense_w2, moe_router, moe_w1, moe_w3,
           moe_w2, final_norm_g):
    batch, seq, d = x.shape
    depth = w_in.shape[0]
    n_mix = N_MIX_SLICES * BRANCH_WIDTH
    n = batch * seq
    assert seq % ATT_BLOCK == 0 and d % 128 == 0
    tm = _pick_tile(seq, 512)

    x2 = x.reshape(n, d)
    tiles = _bias_tiles(bias_table)
    row = lambda v: v.reshape(1, -1)
    final_g = row(final_norm_g)

    for i in range(depth):
        last_layer = i == depth - 1
        g_mix = row(mix_norm_g[i])
        w_mix = w_in[i, :, :n_mix].astype(BF16)
        w_gate = w_in[i, :, n_mix:].astype(BF16)
        u = _inproj(x2, g_mix, w_mix, _pick_tile(n, 2 * tm))

        lam_init = 0.8 - 0.6 * math.exp(-0.3 * i)
        y_a = _moba(u, tiles, batch, seq)
        y_d = _diff(u, tiles, diff_lambda[i], row(jnp.tile(diff_subln_g[i], N_HEADS)), batch, seq, lam_init)
        x2 = _merge(x2, g_mix, w_gate, y_a, y_d, u, conv_w[i], _block_diag(pool_w[i]).astype(BF16),
                    row(pool_scale[i]), branch_proj[i].reshape(-1, d).astype(BF16), w_out[i].astype(BF16),
                    seq, _pick_tile(seq, 2 * tm))

        g_ffn = row(ffn_norm_g[i])
        j = i // 2
        if i % 2 == 0:
            x2 = _ffn(x2, g_ffn, dense_w1[j].astype(BF16), dense_w3[j].astype(BF16),
                      dense_w2[j].astype(BF16), final_g, _pick_tile(n, 2 * tm), min(512, dense_w1.shape[2]),
                      final_norm=last_layer)
        else:
            x2 = _moe(x2, g_ffn, moe_router[j], moe_w1[j].astype(BF16), moe_w3[j].astype(BF16),
                      moe_w2[j].astype(BF16), final_g, tm, final_norm=last_layer)

    return x2.reshape(batch, seq, d)
```
